```python
import math
import jax
import jax.numpy as jnp
from jax import lax
import numpy as np

D_MODEL = 2048
BATCH = 4
SEQ = 8192
DEPTH = 1
DEC_BATCH = 1
DEC_SEQ = 8192
PAST_LEN = 128

GRID_W = 64
ROPE_THETA = 10000.0
Q_BLOCK = 128
EPS = 1e-6

MLA_HEADS = 8
Q_LORA = 512
KV_LORA = 256
QK_NOPE = 128
QK_ROPE = 64
V_HEAD = 128

GQA_HEADS = 8
GQA_KV_HEADS = 2
GQA_HEAD_DIM = 128

MLA_OUT = MLA_HEADS * V_HEAD
GQA_OUT = GQA_HEADS * GQA_HEAD_DIM
MIX_WIDTH = MLA_OUT + GQA_OUT
IN_SPLITS = (Q_LORA, KV_LORA, QK_ROPE, GQA_HEADS * GQA_HEAD_DIM,
             GQA_KV_HEADS * GQA_HEAD_DIM, GQA_KV_HEADS * GQA_HEAD_DIM)
IN_COLS = sum(IN_SPLITS)

N_GROUPS = 4
EXPERTS_PER_GROUP = 8
N_EXPERTS = N_GROUPS * EXPERTS_PER_GROUP
TOP_K = 2
D_EXPERT = 512
EXPERT_BLOCK = 128

kernel_name = "hybrid_mla_gqa_axial_hmoe_encoder"


def rmsnorm(x, g):
    xf = x.astype(jnp.float32)
    y = xf * lax.rsqrt(jnp.mean(xf * xf, axis=-1, keepdims=True) + EPS)
    return (y * g.astype(jnp.float32)).astype(x.dtype)


def axial_cos_sin(seq_len, rot_dim):
    rows = seq_len // GRID_W
    row = jnp.repeat(jnp.arange(rows, dtype=jnp.float32), GRID_W)
    col = jnp.tile(jnp.arange(GRID_W, dtype=jnp.float32), rows)
    n_pairs = rot_dim // 4
    freqs = ROPE_THETA ** (-jnp.arange(n_pairs, dtype=jnp.float32) * 2.0 / (rot_dim // 2))
    ang = jnp.concatenate([row[:, None] * freqs[None, :], col[:, None] * freqs[None, :]], axis=-1)
    return jnp.cos(ang), jnp.sin(ang)


def apply_rope(x, cos, sin):
    xr = x.reshape(x.shape[:-1] + (x.shape[-1] // 2, 2))
    x1, x2 = xr[..., 0], xr[..., 1]
    c = cos[:, None, :].astype(x.dtype)
    s = sin[:, None, :].astype(x.dtype)
    return jnp.stack([x1 * c - x2 * s, x1 * s + x2 * c], axis=-1).reshape(x.shape)


def block_attention(q, k, v):
    b, s, hq, dk = q.shape
    hkv = k.shape[2]
    rep = hq // hkv
    nb = s // Q_BLOCK
    scale = dk ** -0.5
    qb = q.reshape(b, nb, Q_BLOCK, hkv, rep, dk).transpose(1, 0, 2, 3, 4, 5)

    def one_block(qi):
        sc = jnp.einsum('bqgrd,bkgd->bgrqk', qi, k, preferred_element_type=jnp.float32) * scale
        p = jax.nn.softmax(sc, axis=-1)
        return jnp.einsum('bgrqk,bkgd->bqgrd', p.astype(v.dtype), v)

    ob = lax.map(one_block, qb)
    return ob.transpose(1, 0, 2, 3, 4, 5).reshape(b, s, hq * v.shape[-1])


def hybrid_mixer(h, w_in, q_a_norm, w_q_up, kv_a_norm, w_kv_up, q_norm, k_norm, w_out):
    b, s, _ = h.shape
    proj = h @ w_in
    idx = np.cumsum(IN_SPLITS)[:-1].tolist()
    q_lat, kv_lat, k_rope, gq, gk, gv = jnp.split(proj, idx, axis=-1)
    cos_a, sin_a = axial_cos_sin(s, QK_ROPE)
    cos_b, sin_b = axial_cos_sin(s, GQA_HEAD_DIM)

    q = (rmsnorm(q_lat, q_a_norm) @ w_q_up).reshape(b, s, MLA_HEADS, QK_NOPE + QK_ROPE)
    q_nope, q_pe = q[..., :QK_NOPE], apply_rope(q[..., QK_NOPE:], cos_a, sin_a)
    kv = (rmsnorm(kv_lat, kv_a_norm) @ w_kv_up).reshape(b, s, MLA_HEADS, QK_NOPE + V_HEAD)
    k_nope, v_a = kv[..., :QK_NOPE], kv[..., QK_NOPE:]
    k_pe = apply_rope(k_rope[:, :, None, :], cos_a, sin_a)
    k_pe = jnp.broadcast_to(k_pe, (b, s, MLA_HEADS, QK_ROPE))
    o_a = block_attention(jnp.concatenate([q_nope, q_pe], -1),
                          jnp.concatenate([k_nope, k_pe], -1), v_a)

    qb = apply_rope(rmsnorm(gq.reshape(b, s, GQA_HEADS, GQA_HEAD_DIM), q_norm), cos_b, sin_b)
    kb = apply_rope(rmsnorm(gk.reshape(b, s, GQA_KV_HEADS, GQA_HEAD_DIM), k_norm), cos_b, sin_b)
    vb = gv.reshape(b, s, GQA_KV_HEADS, GQA_HEAD_DIM)
    o_b = block_attention(qb, kb, vb)

    return jnp.concatenate([o_a, o_b], axis=-1) @ w_out


def hier_moe(h, w_group, b_group, w_expert, b_expert, w_gate, w_up, w_down):
    b, s, d = h.shape
    t = b * s
    xf = h.reshape(t, d)
    g_prob = jax.nn.softmax((xf @ w_group).astype(jnp.float32) + b_group.astype(jnp.float32), axis=-1)
    g_idx = jnp.argmax(g_prob, axis=-1)
    g_w = jnp.take_along_axis(g_prob, g_idx[:, None], axis=1)[:, 0]
    e_logits = ((xf @ w_expert).astype(jnp.float32) + b_expert.astype(jnp.float32)).reshape(t, N_GROUPS, EXPERTS_PER_GROUP)
    e_logits = jnp.take_along_axis(e_logits, g_idx[:, None, None], axis=1)[:, 0]
    top_w, top_i = lax.top_k(jax.nn.softmax(e_logits, axis=-1), TOP_K)
    top_w = top_w / jnp.sum(top_w, axis=-1, keepdims=True)
    gates = (g_w[:, None] * top_w).reshape(-1)
    expert_id = (g_idx[:, None] * EXPERTS_PER_GROUP + top_i).reshape(-1).astype(jnp.int32)
    token_id = jnp.repeat(jnp.arange(t, dtype=jnp.int32), TOP_K)
    a = t * TOP_K
    order = jnp.argsort(expert_id)
    sorted_e = expert_id[order]
    counts = jnp.bincount(expert_id, length=N_EXPERTS)
    offsets = jnp.cumsum(counts) - counts
    padded = ((counts + EXPERT_BLOCK - 1) // EXPERT_BLOCK) * EXPERT_BLOCK
    pad_ends = jnp.cumsum(padded)
    pad_offsets = pad_ends - padded
    dest = pad_offsets[sorted_e] + (jnp.arange(a, dtype=jnp.int32) - offsets[sorted_e])
    p_rows = a + N_EXPERTS * EXPERT_BLOCK
    nb = p_rows // EXPERT_BLOCK
    pad_tok = jnp.full((p_rows,), t, jnp.int32).at[dest].set(token_id[order])
    pad_gate = jnp.zeros((p_rows,), jnp.float32).at[dest].set(gates[order])
    blk_start = jnp.arange(nb, dtype=jnp.int32) * EXPERT_BLOCK
    blk_e = jnp.minimum(jnp.sum(blk_start[:, None] >= pad_ends[None, :], axis=1), N_EXPERTS - 1)
    x_pad = jnp.concatenate([xf, jnp.zeros((1, d), xf.dtype)], axis=0)

    def run_block(args):
        tok, gt, e = args
        xb = x_pad[tok]
        hid = jax.nn.silu(xb @ w_gate[e]) * (xb @ w_up[e])
        return (hid @ w_down[e]) * gt[:, None].astype(xf.dtype)

    out = lax.map(run_block, (pad_tok.reshape(nb, EXPERT_BLOCK), pad_gate.reshape(nb, EXPERT_BLOCK), blk_e))
    y = jnp.zeros((t + 1, d), xf.dtype).at[pad_tok].add(out.reshape(p_rows, d))[:t]
    return y.reshape(b, s, d)


def trunk(x, attn_norm, w_in, q_a_norm, w_q_up, kv_a_norm, w_kv_up, q_norm, k_norm, w_out,
          ffn_norm, w_group, b_group, w_expert, b_expert, w_gate, w_up, w_down, final_norm):
    for l in range(DEPTH):
        x = x + hybrid_mixer(rmsnorm(x, attn_norm[l]), w_in[l], q_a_norm[l], w_q_up[l], kv_a_norm[l],
                             w_kv_up[l], q_norm[l], k_norm[l], w_out[l])
        x = x + hier_moe(rmsnorm(x, ffn_norm[l]), w_group[l], b_group[l], w_expert[l], b_expert[l],
                         w_gate[l], w_up[l], w_down[l])
    return rmsnorm(x, final_norm)


def setup_inputs(seed: int = 0) -> dict:
    key = jax.random.key(seed)
    ks = jax.random.split(key, 24)
    f32 = jnp.float32

    def nrm(k, shape, fan_in):
        return jax.random.normal(k, shape, f32) * (fan_in ** -0.5)

    def gain(k, shape):
        return 1.0 + 0.01 * jax.random.normal(k, shape, f32)

    L = DEPTH
    return {
        'x_prompt': jax.random.normal(ks[0], (BATCH, SEQ, D_MODEL), f32),
        'x_sample': jax.random.normal(ks[1], (DEC_BATCH, DEC_SEQ, D_MODEL), f32),
        'attn_norm': gain(ks[2], (L, D_MODEL)),
        'w_in': nrm(ks[3], (L, D_MODEL, IN_COLS), D_MODEL),
        'q_a_norm': gain(ks[4], (L, Q_LORA)),
        'w_q_up': nrm(ks[5], (L, Q_LORA, MLA_HEADS * (QK_NOPE + QK_ROPE)), Q_LORA),
        'kv_a_norm': gain(ks[6], (L, KV_LORA)),
        'w_kv_up': nrm(ks[7], (L, KV_LORA, MLA_HEADS * (QK_NOPE + V_HEAD)), KV_LORA),
        'q_norm': gain(ks[8], (L, GQA_HEAD_DIM)),
        'k_norm': gain(ks[9], (L, GQA_HEAD_DIM)),
        'w_out': nrm(ks[10], (L, MIX_WIDTH, D_MODEL), MIX_WIDTH),
        'ffn_norm': gain(ks[11], (L, D_MODEL)),
        'w_group': nrm(ks[12], (L, D_MODEL, N_GROUPS), D_MODEL),
        'b_group': 0.01 * jax.random.normal(ks[13], (L, N_GROUPS), f32),
        'w_expert': nrm(ks[14], (L, D_MODEL, N_EXPERTS), D_MODEL),
        'b_expert': 0.01 * jax.random.normal(ks[15], (L, N_EXPERTS), f32),
        'w_gate': nrm(ks[16], (L, N_EXPERTS, D_MODEL, D_EXPERT), D_MODEL),
        'w_up': nrm(ks[17], (L, N_EXPERTS, D_MODEL, D_EXPERT), D_MODEL),
        'w_down': nrm(ks[18], (L, N_EXPERTS, D_EXPERT, D_MODEL), D_EXPERT),
        'final_norm': gain(ks[19], (D_MODEL,)),
    }


def reference(x_prompt, x_sample, attn_norm, w_in, q_a_norm, w_q_up, kv_a_norm, w_kv_up, q_norm, k_norm,
              w_out, ffn_norm, w_group, b_group, w_expert, b_expert, w_gate, w_up, w_down, final_norm):
    y_prompt = trunk(x_prompt, attn_norm, w_in, q_a_norm, w_q_up, kv_a_norm, w_kv_up, q_norm, k_norm, w_out,
                     ffn_norm, w_group, b_group, w_expert, b_expert, w_gate, w_up, w_down, final_norm)
    y_sample = trunk(x_sample, attn_norm, w_in, q_a_norm, w_q_up, kv_a_norm, w_kv_up, q_norm, k_norm, w_out,
                     ffn_norm, w_group, b_group, w_expert, b_expert, w_gate, w_up, w_down, final_norm)
    return (y_prompt, y_sample)
```

```python
import functools
import math

import numpy as np
import jax
import jax.numpy as jnp
from jax import lax
from jax.experimental import pallas as pl
from jax.experimental.pallas import tpu as pltpu

F32 = jnp.float32
BF16 = jnp.bfloat16

EPS = 1e-6
ROPE_THETA = 10000.0
GRID_W = 64
LANES = 128

MLA_HEADS = 8
Q_LORA = 512
KV_LORA = 256
QK_NOPE = 128
QK_ROPE = 64
V_HEAD = 128
MLA_DK = 256
GQA_HEADS = 8
GQA_KV_HEADS = 2
GQA_REP = GQA_HEADS // GQA_KV_HEADS
HEAD_DIM = 128
N_GROUPS = 4
EXPERTS_PER_GROUP = 8
N_EXPERTS = N_GROUPS * EXPERTS_PER_GROUP
TOP_K = 2
LOG2E = math.log2(math.e)

VMEM_LIMIT = 56 * 1024 * 1024


def _cparams(sem):
    return pltpu.CompilerParams(dimension_semantics=sem, vmem_limit_bytes=VMEM_LIMIT)


def _rms(x, gain):
    return x * lax.rsqrt(jnp.mean(x * x, axis=-1, keepdims=True) + EPS) * gain


def _rope(x, cos_t, sin_t):
    return x * cos_t + pltpu.roll(x, LANES // 2, axis=1) * sin_t


def _pre_kernel(x_ref, gattn_ref, win_ref, gqa_ref, wq_ref, gkv_ref, wkv_ref, gqn_ref, gkn_ref,
                ca_ref, sa_ref, cb_ref, sb_ref,
                qa_ref, ka_ref, va_ref, qb_ref, kb_ref, vb_ref):
    x = x_ref[0]
    h = _rms(x, gattn_ref[...]).astype(BF16)
    proj = jnp.dot(h, win_ref[...], preferred_element_type=F32)
    o = 0
    q_lat = proj[:, o:o + Q_LORA]; o += Q_LORA
    kv_lat = proj[:, o:o + KV_LORA]; o += KV_LORA
    k_rope = proj[:, o:o + LANES]; o += LANES
    gq = proj[:, o:o + GQA_HEADS * HEAD_DIM]; o += GQA_HEADS * HEAD_DIM
    gk = proj[:, o:o + GQA_KV_HEADS * HEAD_DIM]; o += GQA_KV_HEADS * HEAD_DIM
    gv = proj[:, o:o + GQA_KV_HEADS * HEAD_DIM]

    ca, sa, cb, sb = ca_ref[...], sa_ref[...], cb_ref[...], sb_ref[...]

    q = jnp.dot(_rms(q_lat, gqa_ref[...]).astype(BF16), wq_ref[...], preferred_element_type=F32)
    kv = jnp.dot(_rms(kv_lat, gkv_ref[...]).astype(BF16), wkv_ref[...], preferred_element_type=F32)
    k_pe = _rope(k_rope, ca, sa).astype(BF16)
    sc_a = np.float32((QK_NOPE + QK_ROPE) ** -0.5 * LOG2E)
    for hd in range(MLA_HEADS):
        qh = q[:, hd * MLA_DK:(hd + 1) * MLA_DK]
        qa_ref[0, hd, :, 0:LANES] = (qh[:, 0:LANES] * sc_a).astype(BF16)
        qa_ref[0, hd, :, LANES:MLA_DK] = (_rope(qh[:, LANES:MLA_DK], ca, sa) * sc_a).astype(BF16)
        kvh = kv[:, hd * 2 * LANES:(hd + 1) * 2 * LANES]
        ka_ref[0, hd, :, 0:LANES] = kvh[:, 0:LANES].astype(BF16)
        ka_ref[0, hd, :, LANES:MLA_DK] = k_pe
        va_ref[0, hd] = kvh[:, LANES:2 * LANES].astype(BF16)

    sc_b = np.float32(HEAD_DIM ** -0.5 * LOG2E)
    gqn, gkn = gqn_ref[...], gkn_ref[...]
    for hd in range(GQA_HEADS):
        qh = _rope(_rms(gq[:, hd * HEAD_DIM:(hd + 1) * HEAD_DIM], gqn), cb, sb)
        qb_ref[0, hd] = (qh * sc_b).astype(BF16)
    for hd in range(GQA_KV_HEADS):
        kh = _rope(_rms(gk[:, hd * HEAD_DIM:(hd + 1) * HEAD_DIM], gkn), cb, sb)
        kb_ref[0, hd] = kh.astype(BF16)
        vb_ref[0, hd] = gv[:, hd * HEAD_DIM:(hd + 1) * HEAD_DIM].astype(BF16)


def _pre_call(x, gattn, win, gqa, wq, gkv, wkv, gqn, gkn, ca, sa, cb, sb, tm):
    b, s, d = x.shape
    const = lambda shape: pl.BlockSpec(shape, lambda bi, si: (0,) * len(shape))
    tab = pl.BlockSpec((tm, LANES), lambda bi, si: (si, 0))
    head = lambda nh, w: pl.BlockSpec((1, nh, tm, w), lambda bi, si: (bi, 0, si, 0))
    return pl.pallas_call(
        _pre_kernel,
        grid=(b, s // tm),
        in_specs=[pl.BlockSpec((1, tm, d), lambda bi, si: (bi, si, 0)),
                  const(gattn.shape), const(win.shape), const(gqa.shape), const(wq.shape),
                  const(gkv.shape), const(wkv.shape), const(gqn.shape), const(gkn.shape),
                  tab, tab, tab, tab],
        out_specs=[head(MLA_HEADS, MLA_DK), head(MLA_HEADS, MLA_DK), head(MLA_HEADS, V_HEAD),
                   head(GQA_HEADS, HEAD_DIM), head(GQA_KV_HEADS, HEAD_DIM), head(GQA_KV_HEADS, HEAD_DIM)],
        out_shape=[jax.ShapeDtypeStruct((b, MLA_HEADS, s, MLA_DK), BF16),
                   jax.ShapeDtypeStruct((b, MLA_HEADS, s, MLA_DK), BF16),
                   jax.ShapeDtypeStruct((b, MLA_HEADS, s, V_HEAD), BF16),
                   jax.ShapeDtypeStruct((b, GQA_HEADS, s, HEAD_DIM), BF16),
                   jax.ShapeDtypeStruct((b, GQA_KV_HEADS, s, HEAD_DIM), BF16),
                   jax.ShapeDtypeStruct((b, GQA_KV_HEADS, s, HEAD_DIM), BF16)],
        compiler_params=_cparams(("parallel", "parallel")),
        name="pre_proj",
    )(x, gattn, win, gqa, wq, gkv, wkv, gqn, gkn, ca, sa, cb, sb)


def _flash_kernel(q_ref, k_ref, v_ref, o_ref, m_sc, l_sc, acc_sc, *, rep, tq):
    ki = pl.program_id(3)

    @pl.when(ki == 0)
    def _():
        m_sc[...] = jnp.full(m_sc.shape, -jnp.inf, F32)
        l_sc[...] = jnp.zeros(l_sc.shape, F32)
        acc_sc[...] = jnp.zeros(acc_sc.shape, F32)

    dk = q_ref.shape[-1]
    q = q_ref[0].reshape(rep * tq, dk)
    k = k_ref[0, 0]
    v = v_ref[0, 0]
    s = lax.dot_general(q, k, (((1,), (1,)), ((), ())), preferred_element_type=F32)
    m_prev = m_sc[...]
    m_new = jnp.maximum(m_prev, jnp.max(s, axis=-1, keepdims=True))
    alpha = jnp.exp2(m_prev - m_new)
    p = jnp.exp2(s - m_new[:, 0:1])
    l_sc[...] = alpha * l_sc[...] + jnp.sum(p, axis=-1, keepdims=True)
    acc_sc[...] = alpha[:, 0:1] * acc_sc[...] + jnp.dot(p.astype(v.dtype), v, preferred_element_type=F32)
    m_sc[...] = m_new

    @pl.when(ki == pl.num_programs(3) - 1)
    def _():
        out = acc_sc[...] / l_sc[:, 0:1]
        dv = out.shape[-1]
        for r in range(rep):
            o_ref[0, :, r * dv:(r + 1) * dv] = out[r * tq:(r + 1) * tq].astype(o_ref.dtype)


def _flash_call(q, k, v, tq, tk, name):
    b, hq, s, dk = q.shape
    hkv = k.shape[1]
    dv = v.shape[-1]
    rep = hq // hkv
    return pl.pallas_call(
        functools.partial(_flash_kernel, rep=rep, tq=tq),
        grid=(b, hkv, s // tq, s // tk),
        in_specs=[pl.BlockSpec((1, rep, tq, dk), lambda bi, g, qi, ki: (bi, g, qi, 0)),
                  pl.BlockSpec((1, 1, tk, dk), lambda bi, g, qi, ki: (bi, g, ki, 0)),
                  pl.BlockSpec((1, 1, tk, dv), lambda bi, g, qi, ki: (bi, g, ki, 0))],
        out_specs=pl.BlockSpec((1, tq, rep * dv), lambda bi, g, qi, ki: (bi, qi, g)),
        out_shape=jax.ShapeDtypeStruct((b, s, hq * dv), BF16),
        scratch_shapes=[pltpu.VMEM((rep * tq, LANES), F32),
                        pltpu.VMEM((rep * tq, LANES), F32),
                        pltpu.VMEM((rep * tq, dv), F32)],
        compiler_params=_cparams(("parallel", "parallel", "parallel", "arbitrary")),
        name=name,
    )(q, k, v)


def _post_kernel(oa_ref, ob_ref, x_ref, wout_ref, gffn_ref, wr_ref, br_ref, x1_ref, hp_ref, route_ref):
    half = oa_ref.shape[-1]
    mix = jnp.dot(oa_ref[...], wout_ref[0:half, :], preferred_element_type=F32)
    mix = mix + jnp.dot(ob_ref[...], wout_ref[half:2 * half, :], preferred_element_type=F32)
    x1 = x_ref[...] + mix
    x1_ref[...] = x1
    h = _rms(x1, gffn_ref[...])

    dh = h.shape[-1] // 2
    lo = pltpu.bitcast(h[:, 0:dh].astype(BF16).astype(F32), jnp.uint32)
    hi = pltpu.bitcast(h[:, dh:2 * dh].astype(BF16).astype(F32), jnp.uint32)
    hp_ref[...] = (hi & jnp.uint32(0xFFFF0000)) | (lo >> 16)

    lg = jnp.dot(h, wr_ref[...], preferred_element_type=F32, precision=lax.Precision.HIGHEST) + br_ref[...]
    lane = lax.broadcasted_iota(jnp.int32, lg.shape, 1)
    neg = jnp.float32(-jnp.inf)
    big = jnp.int32(4 * LANES)
    gl = jnp.where(lane < N_GROUPS, lg, neg)
    gmax = jnp.max(gl, axis=-1, keepdims=True)
    gsum = jnp.sum(jnp.exp(gl - gmax), axis=-1, keepdims=True)
    g_w = 1.0 / gsum
    g_idx = jnp.min(jnp.where(gl == gmax, lane, big), axis=-1, keepdims=True)
    lo_lane = N_GROUPS + EXPERTS_PER_GROUP * g_idx
    el = jnp.where((lane >= lo_lane) & (lane < lo_lane + EXPERTS_PER_GROUP), lg, neg)
    m1 = jnp.max(el, axis=-1, keepdims=True)
    i1 = jnp.min(jnp.where(el == m1, lane, big), axis=-1, keepdims=True)
    el2 = jnp.where(lane == i1, neg, el)
    m2 = jnp.max(el2, axis=-1, keepdims=True)
    i2 = jnp.min(jnp.where(el2 == m2, lane, big), axis=-1, keepdims=True)
    e2 = jnp.exp(m2 - m1)
    w1 = 1.0 / (1.0 + e2)
    w2 = e2 / (1.0 + e2)
    out = jnp.where(lane == 0, (i1 - N_GROUPS).astype(F32),
          jnp.where(lane == 1, (i2 - N_GROUPS).astype(F32),
          jnp.where(lane == 2, g_w * w1,
          jnp.where(lane == 3, g_w * w2, 0.0))))
    route_ref[...] = out


def _post_call(oa, ob, x, wout, gffn, wr, br, tm):
    t, d = x.shape
    half = oa.shape[-1]
    const = lambda shape: pl.BlockSpec(shape, lambda i: (0,) * len(shape))
    row = lambda w: pl.BlockSpec((tm, w), lambda i: (i, 0))
    return pl.pallas_call(
        _post_kernel,
        grid=(t // tm,),
        in_specs=[row(half), row(half), row(d), const(wout.shape), const(gffn.shape),
                  const(wr.shape), const(br.shape)],
        out_specs=[row(d), row(d // 2), row(LANES)],
        out_shape=[jax.ShapeDtypeStruct((t, d), F32),
                   jax.ShapeDtypeStruct((t, d // 2), jnp.uint32),
                   jax.ShapeDtypeStruct((t, LANES), F32)],
        compiler_params=_cparams(("parallel",)),
        name="post_proj_router",
    )(oa, ob, x, wout, gffn, wr, br)


def _row_gather(src_hbm, idx_ref, buf, sem, n_rows):
    def body(r, carry):
        pltpu.make_async_copy(src_hbm.at[pl.ds(idx_ref[0, 0, r], 1)], buf.at[pl.ds(r, 1)], sem).start()
        return carry
    lax.fori_loop(0, n_rows, body, 0)


def _moe_kernel(blk_e_ref, n_used_ref, tok_cur_ref, tok_next_ref, hp_hbm, wg_ref, wu_ref, wd_ref,
                out_ref, buf, sem):
    i = pl.program_id(0)
    slot = i % 2
    tb = buf.shape[1]
    n_used = n_used_ref[0]

    @pl.when(i == 0)
    def _():
        _row_gather(hp_hbm, tok_cur_ref, buf.at[0], sem.at[0], tb)

    @pl.when(i + 1 < n_used)
    def _():
        _row_gather(hp_hbm, tok_next_ref, buf.at[1 - slot], sem.at[1 - slot], tb)

    @pl.when(i < n_used)
    def _():
        pltpu.make_async_copy(hp_hbm.at[pl.ds(0, tb)], buf.at[slot], sem.at[slot]).wait()
        w = buf[slot]
        dh = w.shape[-1]
        x_lo = pltpu.bitcast(w << 16, F32).astype(BF16)
        x_hi = pltpu.bitcast(w & jnp.uint32(0xFFFF0000), F32).astype(BF16)
        g = jnp.dot(x_lo, wg_ref[0, 0:dh, :], preferred_element_type=F32)
        g = g + jnp.dot(x_hi, wg_ref[0, dh:2 * dh, :], preferred_element_type=F32)
        u = jnp.dot(x_lo, wu_ref[0, 0:dh, :], preferred_element_type=F32)
        u = u + jnp.dot(x_hi, wu_ref[0, dh:2 * dh, :], preferred_element_type=F32)
        hid = (g * jax.nn.sigmoid(g)) * u
        out_ref[...] = jnp.dot(hid.astype(BF16), wd_ref[0], preferred_element_type=F32)

    @pl.when(i >= n_used)
    def _():
        out_ref[...] = jnp.zeros(out_ref.shape, out_ref.dtype)


def _moe_call(blk_e, n_used, pad_tok, hp, wg, wu, wd, tb):
    nb = pad_tok.shape[0]
    dh = hp.shape[-1]
    d = 2 * dh
    de = wg.shape[-1]
    smem_blk = lambda f: pl.BlockSpec((1, 1, tb), f, memory_space=pltpu.SMEM)
    wspec = lambda shape: pl.BlockSpec(shape, lambda i, be, nu: (be[i], 0, 0))
    grid_spec = pltpu.PrefetchScalarGridSpec(
        num_scalar_prefetch=2,
        grid=(nb,),
        in_specs=[smem_blk(lambda i, be, nu: (i, 0, 0)),
                  smem_blk(lambda i, be, nu: (jnp.minimum(i + 1, nb - 1), 0, 0)),
                  pl.BlockSpec(memory_space=pl.ANY),
                  wspec((1, d, de)), wspec((1, d, de)), wspec((1, de, d))],
        out_specs=pl.BlockSpec((tb, d), lambda i, be, nu: (i, 0)),
        scratch_shapes=[pltpu.VMEM((2, tb, dh), jnp.uint32), pltpu.SemaphoreType.DMA((2,))],
    )
    return pl.pallas_call(
        _moe_kernel,
        grid_spec=grid_spec,
        out_shape=jax.ShapeDtypeStruct((nb * tb, d), F32),
        compiler_params=_cparams(("arbitrary",)),
        name="moe_experts",
    )(blk_e, n_used, pad_tok, pad_tok, hp, wg, wu, wd)


def _final_kernel(dest_cur_ref, dest_next_ref, x1_ref, route_ref, rows_hbm, gfin_ref, y_ref, buf, sem):
    i = pl.program_id(0)
    nsteps = pl.num_programs(0)
    slot = i % 2
    n_rows = buf.shape[1]
    tm = n_rows // TOP_K

    @pl.when(i == 0)
    def _():
        _row_gather(rows_hbm, dest_cur_ref, buf.at[0], sem.at[0], n_rows)

    @pl.when(i + 1 < nsteps)
    def _():
        _row_gather(rows_hbm, dest_next_ref, buf.at[1 - slot], sem.at[1 - slot], n_rows)

    pltpu.make_async_copy(rows_hbm.at[pl.ds(0, n_rows)], buf.at[slot], sem.at[slot]).wait()
    route = route_ref[...]
    r0 = buf[slot, 0:tm, :] * route[:, 2:3]
    r1 = buf[slot, tm:2 * tm, :] * route[:, 3:4]
    y_ref[...] = _rms(x1_ref[...] + (r0 + r1), gfin_ref[...])


def _final_call(dest, x1, route, rows, gfin, tm):
    t, d = x1.shape
    nt = t // tm
    smem_blk = lambda f: pl.BlockSpec((1, 1, TOP_K * tm), f, memory_space=pltpu.SMEM)
    return pl.pallas_call(
        _final_kernel,
        grid=(nt,),
        in_specs=[smem_blk(lambda i: (i, 0, 0)),
                  smem_blk(lambda i: (jnp.minimum(i + 1, nt - 1), 0, 0)),
                  pl.BlockSpec((tm, d), lambda i: (i, 0)),
                  pl.BlockSpec((tm, LANES), lambda i: (i, 0)),
                  pl.BlockSpec(memory_space=pl.ANY),
                  pl.BlockSpec(gfin.shape, lambda i: (0, 0))],
        out_specs=pl.BlockSpec((tm, d), lambda i: (i, 0)),
        out_shape=jax.ShapeDtypeStruct((t, d), F32),
        scratch_shapes=[pltpu.VMEM((2, TOP_K * tm, d), F32), pltpu.SemaphoreType.DMA((2,))],
        compiler_params=_cparams(("arbitrary",)),
        name="combine_final_norm",
    )(dest, dest, x1, route, rows, gfin)


def _deinterleave(n):
    return np.concatenate([np.arange(0, n, 2), np.arange(1, n, 2)])


def _rope_pad_cols(base):
    pad = -np.ones(LANES // 4, np.int64)
    return np.concatenate([base + np.arange(0, QK_ROPE, 2), pad, base + np.arange(1, QK_ROPE, 2), pad])


def _take_cols(w, cols):
    w_ext = jnp.concatenate([w, jnp.zeros((w.shape[0], 1), w.dtype)], axis=1)
    return w_ext[:, np.where(cols < 0, w.shape[1], cols)]


def _prep_weights(w_in, w_q_up, q_norm, k_norm):
    o_kr = Q_LORA + KV_LORA
    o_gq = o_kr + QK_ROPE
    o_gk = o_gq + GQA_HEADS * HEAD_DIM
    o_gv = o_gk + GQA_KV_HEADS * HEAD_DIM
    cols = [np.arange(0, o_kr), _rope_pad_cols(o_kr)]
    for hd in range(GQA_HEADS):
        cols.append(o_gq + hd * HEAD_DIM + _deinterleave(HEAD_DIM))
    for hd in range(GQA_KV_HEADS):
        cols.append(o_gk + hd * HEAD_DIM + _deinterleave(HEAD_DIM))
    cols.append(np.arange(o_gv, o_gv + GQA_KV_HEADS * HEAD_DIM))
    win = _take_cols(w_in, np.concatenate(cols)).astype(BF16)

    qcols = []
    for hd in range(MLA_HEADS):
        base = hd * (QK_NOPE + QK_ROPE)
        qcols += [base + np.arange(QK_NOPE), _rope_pad_cols(base + QK_NOPE)]
    wq = _take_cols(w_q_up, np.concatenate(qcols)).astype(BF16)
    perm = _deinterleave(HEAD_DIM)
    return win, wq, q_norm[perm][None, :], k_norm[perm][None, :]


def _rope_tables(s):
    rows = s // GRID_W
    row = jnp.repeat(jnp.arange(rows, dtype=F32), GRID_W)
    col = jnp.tile(jnp.arange(GRID_W, dtype=F32), rows)

    def cos_sin(rot_dim):
        n_pairs = rot_dim // 4
        freqs = ROPE_THETA ** (-jnp.arange(n_pairs, dtype=F32) * 2.0 / (rot_dim // 2))
        ang = jnp.concatenate([row[:, None] * freqs[None, :], col[:, None] * freqs[None, :]], axis=-1)
        return jnp.cos(ang), jnp.sin(ang)

    ca, sa = cos_sin(QK_ROPE)
    cb, sb = cos_sin(HEAD_DIM)
    return (jnp.concatenate([ca, ca, ca, ca], -1), jnp.concatenate([-sa, -sa, sa, sa], -1),
            jnp.concatenate([cb, cb], -1), jnp.concatenate([-sb, sb], -1))


def _pick(n, pref):
    return pref if n % pref == 0 else n


def _route_plan(route, tb):
    t = route.shape[0]
    a = t * TOP_K
    expert_id = route[:, 0:TOP_K].astype(jnp.int32).reshape(a)
    onehot = (expert_id[:, None] == jnp.arange(N_EXPERTS, dtype=jnp.int32)[None, :]).astype(jnp.int32)
    csum = jnp.cumsum(onehot, axis=0)
    counts = csum[-1]
    rank = jnp.sum(onehot * csum, axis=1) - 1
    padded = ((counts + tb - 1) // tb) * tb
    pad_ends = jnp.cumsum(padded)
    pad_off = pad_ends - padded
    dest = pad_off[expert_id] + rank
    nb = (a + N_EXPERTS * tb) // tb
    token_id = jnp.repeat(jnp.arange(t, dtype=jnp.int32), TOP_K)
    pad_tok = jnp.zeros((nb * tb,), jnp.int32).at[dest].set(token_id, unique_indices=True)
    blk_start = jnp.arange(nb, dtype=jnp.int32) * tb
    blk_e = jnp.minimum(jnp.sum(blk_start[:, None] >= pad_ends[None, :], axis=1), N_EXPERTS - 1)
    n_used = (pad_ends[-1] // tb).astype(jnp.int32).reshape(1)
    last_e = blk_e[jnp.maximum(n_used[0] - 1, 0)]
    blk_e = jnp.where(jnp.arange(nb) < n_used[0], blk_e, last_e).astype(jnp.int32)
    return pad_tok.reshape(nb, 1, tb), dest.reshape(t, TOP_K), blk_e, n_used


def _trunk(x, p):
    b, s, d = x.shape
    t = b * s
    tm = _pick(s, 256)
    qa, ka, va, qb, kb, vb = _pre_call(x, p["gattn"], p["win"], p["gqa"], p["wq"], p["gkv"], p["wkv"],
                                       p["gqn"], p["gkn"], *p["tables"], tm)
    oa = _flash_call(qa, ka, va, _pick(s, 1024), _pick(s, 512), "mla_flash")
    ob = _flash_call(qb, kb, vb, _pick(s, 256), _pick(s, 512), "gqa_flash")
    x1, hp, route = _post_call(oa.reshape(t, -1), ob.reshape(t, -1), x.reshape(t, d), p["wout"], p["gffn"],
                               p["wr"], p["br"], tm)
    tb = 256
    pad_tok, dest, blk_e, n_used = _route_plan(route, tb)
    rows = _moe_call(blk_e, n_used, pad_tok, hp, p["wg"], p["wu"], p["wd"], tb)
    nt = t // tm
    dest_blk = dest.reshape(nt, tm, TOP_K).transpose(0, 2, 1).reshape(nt, 1, TOP_K * tm)
    y = _final_call(dest_blk, x1, route, rows, p["gfin"], tm)
    return y.reshape(b, s, d)


def kernel(x_prompt, x_sample, attn_norm, w_in, q_a_norm, w_q_up, kv_a_norm, w_kv_up, q_norm, k_norm, w_out,
           ffn_norm, w_group, b_group, w_expert, b_expert, w_gate, w_up, w_down, final_norm):
    assert attn_norm.shape[0] == 1, "single-layer trunk"
    win, wq, gqn, gkn = _prep_weights(w_in[0], w_q_up[0], q_norm[0], k_norm[0])
    d = w_in.shape[1]
    wr = jnp.zeros((d, LANES), F32).at[:, 0:N_GROUPS].set(w_group[0])
    wr = wr.at[:, N_GROUPS:N_GROUPS + N_EXPERTS].set(w_expert[0])
    br = jnp.zeros((1, LANES), F32).at[0, 0:N_GROUPS].set(b_group[0])
    br = br.at[0, N_GROUPS:N_GROUPS + N_EXPERTS].set(b_expert[0])
    p = dict(gattn=attn_norm, win=win, gqa=q_a_norm, wq=wq, gkv=kv_a_norm, wkv=w_kv_up[0].astype(BF16),
             gqn=gqn, gkn=gkn, wout=w_out[0].astype(BF16), gffn=ffn_norm, wr=wr, br=br,
             wg=w_gate[0].astype(BF16), wu=w_up[0].astype(BF16), wd=w_down[0].astype(BF16),
             gfin=final_norm[None, :])
    outs = []
    for x in (x_prompt, x_sample):
        p["tables"] = _rope_tables(x.shape[1])
        outs.append(_trunk(x, p))
    return tuple(outs)
```

```python
import functools
import math

import numpy as np
import jax
import jax.numpy as jnp
from jax import lax
from jax.experimental import pallas as pl
from jax.experimental.pallas import tpu as pltpu

F32 = jnp.float32
BF16 = jnp.bfloat16

EPS = 1e-6
ROPE_THETA = 10000.0
GRID_W = 64
LANES = 128

MLA_HEADS = 8
Q_LORA = 512
KV_LORA = 256
QK_NOPE = 128
QK_ROPE = 64
V_HEAD = 128
MLA_DK = 256
GQA_HEADS = 8
GQA_KV_HEADS = 2
GQA_REP = GQA_HEADS // GQA_KV_HEADS
HEAD_DIM = 128
N_GROUPS = 4
EXPERTS_PER_GROUP = 8
N_EXPERTS = N_GROUPS * EXPERTS_PER_GROUP
TOP_K = 2
LOG2E = math.log2(math.e)

VMEM_LIMIT = 56 * 1024 * 1024


def _cparams(sem):
    return pltpu.CompilerParams(dimension_semantics=sem, vmem_limit_bytes=VMEM_LIMIT)


def _rms(x, gain):
    return x * lax.rsqrt(jnp.mean(x * x, axis=-1, keepdims=True) + EPS) * gain


def _rope(x, cos_t, sin_t):
    return x * cos_t + pltpu.roll(x, LANES // 2, axis=1) * sin_t


def _pre_kernel(x_ref, gattn_ref, win_ref, gqa_ref, wq_ref, gkv_ref, wkv_ref, gqn_ref, gkn_ref,
                ca_ref, sa_ref, cb_ref, sb_ref,
                qa_ref, ka_ref, va_ref, qb_ref, kb_ref, vb_ref):
    x = x_ref[0]
    h = _rms(x, gattn_ref[...]).astype(BF16)
    proj = jnp.dot(h, win_ref[...], preferred_element_type=F32)
    o = 0
    q_lat = proj[:, o:o + Q_LORA]; o += Q_LORA
    kv_lat = proj[:, o:o + KV_LORA]; o += KV_LORA
    k_rope = proj[:, o:o + LANES]; o += LANES
    gq = proj[:, o:o + GQA_HEADS * HEAD_DIM]; o += GQA_HEADS * HEAD_DIM
    gk = proj[:, o:o + GQA_KV_HEADS * HEAD_DIM]; o += GQA_KV_HEADS * HEAD_DIM
    gv = proj[:, o:o + GQA_KV_HEADS * HEAD_DIM]

    ca, sa, cb, sb = ca_ref[...], sa_ref[...], cb_ref[...], sb_ref[...]

    q = jnp.dot(_rms(q_lat, gqa_ref[...]).astype(BF16), wq_ref[...], preferred_element_type=F32)
    kv = jnp.dot(_rms(kv_lat, gkv_ref[...]).astype(BF16), wkv_ref[...], preferred_element_type=F32)
    k_pe = _rope(k_rope, ca, sa).astype(BF16)
    sc_a = np.float32((QK_NOPE + QK_ROPE) ** -0.5 * LOG2E)
    for hd in range(MLA_HEADS):
        qh = q[:, hd * MLA_DK:(hd + 1) * MLA_DK]
        qa_ref[0, hd, :, 0:LANES] = (qh[:, 0:LANES] * sc_a).astype(BF16)
        qa_ref[0, hd, :, LANES:MLA_DK] = (_rope(qh[:, LANES:MLA_DK], ca, sa) * sc_a).astype(BF16)
        kvh = kv[:, hd * 2 * LANES:(hd + 1) * 2 * LANES]
        ka_ref[0, hd, :, 0:LANES] = kvh[:, 0:LANES].astype(BF16)
        ka_ref[0, hd, :, LANES:MLA_DK] = k_pe
        va_ref[0, hd, 0] = kvh[:, LANES:2 * LANES].T.astype(BF16)

    sc_b = np.float32(HEAD_DIM ** -0.5 * LOG2E)
    gqn, gkn = gqn_ref[...], gkn_ref[...]
    for hd in range(GQA_HEADS):
        qh = _rope(_rms(gq[:, hd * HEAD_DIM:(hd + 1) * HEAD_DIM], gqn), cb, sb)
        qb_ref[0, hd] = (qh * sc_b).astype(BF16)
    for hd in range(GQA_KV_HEADS):
        kh = _rope(_rms(gk[:, hd * HEAD_DIM:(hd + 1) * HEAD_DIM], gkn), cb, sb)
        kb_ref[0, hd] = kh.astype(BF16)
        vb_ref[0, hd, 0] = gv[:, hd * HEAD_DIM:(hd + 1) * HEAD_DIM].T.astype(BF16)


def _pre_call(x, gattn, win, gqa, wq, gkv, wkv, gqn, gkn, ca, sa, cb, sb, tm):
    b, s, d = x.shape
    const = lambda shape: pl.BlockSpec(shape, lambda bi, si: (0,) * len(shape))
    tab = pl.BlockSpec((tm, LANES), lambda bi, si: (si, 0))
    head = lambda nh, w: pl.BlockSpec((1, nh, tm, w), lambda bi, si: (bi, 0, si, 0))
    head_t = lambda nh, w: pl.BlockSpec((1, nh, 1, w, tm), lambda bi, si: (bi, 0, si, 0, 0))
    return pl.pallas_call(
        _pre_kernel,
        grid=(b, s // tm),
        in_specs=[pl.BlockSpec((1, tm, d), lambda bi, si: (bi, si, 0)),
                  const(gattn.shape), const(win.shape), const(gqa.shape), const(wq.shape),
                  const(gkv.shape), const(wkv.shape), const(gqn.shape), const(gkn.shape),
                  tab, tab, tab, tab],
        out_specs=[head(MLA_HEADS, MLA_DK), head(MLA_HEADS, MLA_DK), head_t(MLA_HEADS, V_HEAD),
                   head(GQA_HEADS, HEAD_DIM), head(GQA_KV_HEADS, HEAD_DIM), head_t(GQA_KV_HEADS, HEAD_DIM)],
        out_shape=[jax.ShapeDtypeStruct((b, MLA_HEADS, s, MLA_DK), BF16),
                   jax.ShapeDtypeStruct((b, MLA_HEADS, s, MLA_DK), BF16),
                   jax.ShapeDtypeStruct((b, MLA_HEADS, s // tm, V_HEAD, tm), BF16),
                   jax.ShapeDtypeStruct((b, GQA_HEADS, s, HEAD_DIM), BF16),
                   jax.ShapeDtypeStruct((b, GQA_KV_HEADS, s, HEAD_DIM), BF16),
                   jax.ShapeDtypeStruct((b, GQA_KV_HEADS, s // tm, HEAD_DIM, tm), BF16)],
        compiler_params=_cparams(("parallel", "parallel")),
        name="pre_proj",
    )(x, gattn, win, gqa, wq, gkv, wkv, gqn, gkn, ca, sa, cb, sb)


def _flash_kernel(q_ref, k_ref, vt_ref, o_ref, acc_sc, s_sc, *, rep, tk):
    s_len = k_ref.shape[2]
    dv, cw = vt_ref.shape[3], vt_ref.shape[4]
    tq = q_ref.shape[2]
    n_sub = tk // cw
    n_steps = s_len // tk
    unroll = 4 if n_steps % 4 == 0 else 2

    for r in range(rep):
        q = q_ref[0, r]

        def scores(j):
            k = k_ref[0, 0, pl.ds(pl.multiple_of(j * tk, tk), tk), :]
            return lax.dot_general(k, q, (((1,), (1,)), ((), ())), preferred_element_type=F32)

        acc_sc[...] = jnp.zeros(acc_sc.shape, F32)
        s_sc[0] = scores(0)

        def body(jj, carry):
            m_prev, l_prev = carry
            for si in range(unroll):
                j = jj * unroll + si
                st = s_sc[si % 2]
                s_sc[1 - si % 2] = scores(jnp.minimum(j + 1, n_steps - 1))
                m_new = jnp.maximum(m_prev, jnp.max(st, axis=0, keepdims=True))
                alpha = jnp.exp2(m_prev - m_new)
                p = jnp.exp2(st - m_new)
                l_prev = alpha * l_prev + jnp.sum(p, axis=0, keepdims=True)
                pb = p.astype(BF16)
                pv = jnp.dot(vt_ref[0, 0, j * n_sub], pb[0:cw], preferred_element_type=F32)
                for c in range(1, n_sub):
                    pv = pv + jnp.dot(vt_ref[0, 0, j * n_sub + c], pb[c * cw:(c + 1) * cw],
                                      preferred_element_type=F32)
                acc_sc[...] = alpha * acc_sc[...] + pv
                m_prev = m_new
            return m_prev, l_prev

        init = (jnp.full((1, tq), -jnp.inf, F32), jnp.zeros((1, tq), F32))
        _, l_fin = lax.fori_loop(0, n_steps // unroll, body, init)
        out_t = acc_sc[...] / l_fin
        o_ref[0, :, r * dv:(r + 1) * dv] = out_t.T.astype(o_ref.dtype)


def _flash_call(q, k, vt, tq, tk, name):
    b, hq, s, dk = q.shape
    hkv = k.shape[1]
    _, _, nchunk, dv, cw = vt.shape
    rep = hq // hkv
    assert tk % cw == 0 and (s // tk) % 2 == 0 and s % tq == 0
    return pl.pallas_call(
        functools.partial(_flash_kernel, rep=rep, tk=tk),
        grid=(b, hkv, s // tq),
        in_specs=[pl.BlockSpec((1, rep, tq, dk), lambda bi, g, qi: (bi, g, qi, 0)),
                  pl.BlockSpec((1, 1, s, dk), lambda bi, g, qi: (bi, g, 0, 0)),
                  pl.BlockSpec((1, 1, nchunk, dv, cw), lambda bi, g, qi: (bi, g, 0, 0, 0))],
        out_specs=pl.BlockSpec((1, tq, rep * dv), lambda bi, g, qi: (bi, qi, g)),
        out_shape=jax.ShapeDtypeStruct((b, s, hq * dv), BF16),
        scratch_shapes=[pltpu.VMEM((dv, tq), F32), pltpu.VMEM((2, tk, tq), F32)],
        compiler_params=_cparams(("parallel", "parallel", "arbitrary")),
        name=name,
    )(q, k, vt)


def _post_kernel(oa_ref, ob_ref, x_ref, wout_ref, gffn_ref, wr_ref, br_ref, x1_ref, hp_ref, route_ref):
    half = oa_ref.shape[-1]
    mix = jnp.dot(oa_ref[...], wout_ref[0:half, :], preferred_element_type=F32)
    mix = mix + jnp.dot(ob_ref[...], wout_ref[half:2 * half, :], preferred_element_type=F32)
    x1 = x_ref[...] + mix
    x1_ref[...] = x1
    h = _rms(x1, gffn_ref[...])

    dh = h.shape[-1] // 2
    lo = pltpu.bitcast(h[:, 0:dh].astype(BF16).astype(F32), jnp.uint32)
    hi = pltpu.bitcast(h[:, dh:2 * dh].astype(BF16).astype(F32), jnp.uint32)
    hp_ref[...] = (hi & jnp.uint32(0xFFFF0000)) | (lo >> 16)

    lg = jnp.dot(h, wr_ref[...], preferred_element_type=F32, precision=lax.Precision.HIGHEST) + br_ref[...]
    lane = lax.broadcasted_iota(jnp.int32, lg.shape, 1)
    neg = jnp.float32(-jnp.inf)
    big = jnp.int32(4 * LANES)
    gl = jnp.where(lane < N_GROUPS, lg, neg)
    gmax = jnp.max(gl, axis=-1, keepdims=True)
    gsum = jnp.sum(jnp.exp(gl - gmax), axis=-1, keepdims=True)
    g_w = 1.0 / gsum
    g_idx = jnp.min(jnp.where(gl == gmax, lane, big), axis=-1, keepdims=True)
    lo_lane = N_GROUPS + EXPERTS_PER_GROUP * g_idx
    el = jnp.where((lane >= lo_lane) & (lane < lo_lane + EXPERTS_PER_GROUP), lg, neg)
    m1 = jnp.max(el, axis=-1, keepdims=True)
    i1 = jnp.min(jnp.where(el == m1, lane, big), axis=-1, keepdims=True)
    el2 = jnp.where(lane == i1, neg, el)
    m2 = jnp.max(el2, axis=-1, keepdims=True)
    i2 = jnp.min(jnp.where(el2 == m2, lane, big), axis=-1, keepdims=True)
    e2 = jnp.exp(m2 - m1)
    w1 = 1.0 / (1.0 + e2)
    w2 = e2 / (1.0 + e2)
    out = jnp.where(lane == 0, (i1 - N_GROUPS).astype(F32),
          jnp.where(lane == 1, (i2 - N_GROUPS).astype(F32),
          jnp.where(lane == 2, g_w * w1,
          jnp.where(lane == 3, g_w * w2, 0.0))))
    route_ref[...] = out


def _post_call(oa, ob, x, wout, gffn, wr, br, tm):
    t, d = x.shape
    half = oa.shape[-1]
    const = lambda shape: pl.BlockSpec(shape, lambda i: (0,) * len(shape))
    row = lambda w: pl.BlockSpec((tm, w), lambda i: (i, 0))
    return pl.pallas_call(
        _post_kernel,
        grid=(t // tm,),
        in_specs=[row(half), row(half), row(d), const(wout.shape), const(gffn.shape),
                  const(wr.shape), const(br.shape)],
        out_specs=[row(d), row(d // 2), row(LANES)],
        out_shape=[jax.ShapeDtypeStruct((t, d), F32),
                   jax.ShapeDtypeStruct((t, d // 2), jnp.uint32),
                   jax.ShapeDtypeStruct((t, LANES), F32)],
        compiler_params=_cparams(("parallel",)),
        name="post_proj_router",
    )(oa, ob, x, wout, gffn, wr, br)


def _row_gather(src_hbm, idx_ref, buf, sem, n_rows):
    def body(r, carry):
        pltpu.make_async_copy(src_hbm.at[pl.ds(idx_ref[0, 0, r], 1)], buf.at[pl.ds(r, 1)], sem).start()
        return carry
    lax.fori_loop(0, n_rows, body, 0)


def _moe_kernel(blk_e_ref, n_used_ref, tok_cur_ref, tok_next_ref, hp_hbm, wg_ref, wu_ref, wd_ref,
                out_ref, buf, sem):
    i = pl.program_id(0)
    slot = i % 2
    tb = buf.shape[1]
    n_used = n_used_ref[0]

    @pl.when(i == 0)
    def _():
        _row_gather(hp_hbm, tok_cur_ref, buf.at[0], sem.at[0], tb)

    @pl.when(i + 1 < n_used)
    def _():
        _row_gather(hp_hbm, tok_next_ref, buf.at[1 - slot], sem.at[1 - slot], tb)

    @pl.when(i < n_used)
    def _():
        pltpu.make_async_copy(hp_hbm.at[pl.ds(0, tb)], buf.at[slot], sem.at[slot]).wait()
        w = buf[slot]
        dh = w.shape[-1]
        x_lo = pltpu.bitcast(w << 16, F32).astype(BF16)
        x_hi = pltpu.bitcast(w & jnp.uint32(0xFFFF0000), F32).astype(BF16)
        g = jnp.dot(x_lo, wg_ref[0, 0:dh, :], preferred_element_type=F32)
        g = g + jnp.dot(x_hi, wg_ref[0, dh:2 * dh, :], preferred_element_type=F32)
        u = jnp.dot(x_lo, wu_ref[0, 0:dh, :], preferred_element_type=F32)
        u = u + jnp.dot(x_hi, wu_ref[0, dh:2 * dh, :], preferred_element_type=F32)
        hid = (g * jax.nn.sigmoid(g)) * u
        out_ref[...] = jnp.dot(hid.astype(BF16), wd_ref[0], preferred_element_type=F32)

    @pl.when(i >= n_used)
    def _():
        out_ref[...] = jnp.zeros(out_ref.shape, out_ref.dtype)


def _moe_call(blk_e, n_used, pad_tok, hp, wg, wu, wd, tb):
    nb = pad_tok.shape[0]
    dh = hp.shape[-1]
    d = 2 * dh
    de = wg.shape[-1]
    smem_blk = lambda f: pl.BlockSpec((1, 1, tb), f, memory_space=pltpu.SMEM)
    wspec = lambda shape: pl.BlockSpec(shape, lambda i, be, nu: (be[i], 0, 0))
    grid_spec = pltpu.PrefetchScalarGridSpec(
        num_scalar_prefetch=2,
        grid=(nb,),
        in_specs=[smem_blk(lambda i, be, nu: (i, 0, 0)),
                  smem_blk(lambda i, be, nu: (jnp.minimum(i + 1, nb - 1), 0, 0)),
                  pl.BlockSpec(memory_space=pl.ANY),
                  wspec((1, d, de)), wspec((1, d, de)), wspec((1, de, d))],
        out_specs=pl.BlockSpec((tb, d), lambda i, be, nu: (i, 0)),
        scratch_shapes=[pltpu.VMEM((2, tb, dh), jnp.uint32), pltpu.SemaphoreType.DMA((2,))],
    )
    return pl.pallas_call(
        _moe_kernel,
        grid_spec=grid_spec,
        out_shape=jax.ShapeDtypeStruct((nb * tb, d), F32),
        compiler_params=_cparams(("arbitrary",)),
        name="moe_experts",
    )(blk_e, n_used, pad_tok, pad_tok, hp, wg, wu, wd)


def _final_kernel(dest_cur_ref, dest_next_ref, x1_ref, route_ref, rows_hbm, gfin_ref, y_ref, buf, sem):
    i = pl.program_id(0)
    nsteps = pl.num_programs(0)
    slot = i % 2
    n_rows = buf.shape[1]
    tm = n_rows // TOP_K

    @pl.when(i == 0)
    def _():
        _row_gather(rows_hbm, dest_cur_ref, buf.at[0], sem.at[0], n_rows)

    @pl.when(i + 1 < nsteps)
    def _():
        _row_gather(rows_hbm, dest_next_ref, buf.at[1 - slot], sem.at[1 - slot], n_rows)

    pltpu.make_async_copy(rows_hbm.at[pl.ds(0, n_rows)], buf.at[slot], sem.at[slot]).wait()
    route = route_ref[...]
    r0 = buf[slot, 0:tm, :] * route[:, 2:3]
    r1 = buf[slot, tm:2 * tm, :] * route[:, 3:4]
    y_ref[...] = _rms(x1_ref[...] + (r0 + r1), gfin_ref[...])


def _final_call(dest, x1, route, rows, gfin, tm):
    t, d = x1.shape
    nt = t // tm
    smem_blk = lambda f: pl.BlockSpec((1, 1, TOP_K * tm), f, memory_space=pltpu.SMEM)
    return pl.pallas_call(
        _final_kernel,
        grid=(nt,),
        in_specs=[smem_blk(lambda i: (i, 0, 0)),
                  smem_blk(lambda i: (jnp.minimum(i + 1, nt - 1), 0, 0)),
                  pl.BlockSpec((tm, d), lambda i: (i, 0)),
                  pl.BlockSpec((tm, LANES), lambda i: (i, 0)),
                  pl.BlockSpec(memory_space=pl.ANY),
                  pl.BlockSpec(gfin.shape, lambda i: (0, 0))],
        out_specs=pl.BlockSpec((tm, d), lambda i: (i, 0)),
        out_shape=jax.ShapeDtypeStruct((t, d), F32),
        scratch_shapes=[pltpu.VMEM((2, TOP_K * tm, d), F32), pltpu.SemaphoreType.DMA((2,))],
        compiler_params=_cparams(("arbitrary",)),
        name="combine_final_norm",
    )(dest, dest, x1, route, rows, gfin)


def _deinterleave(n):
    return np.concatenate([np.arange(0, n, 2), np.arange(1, n, 2)])


def _rope_pad_cols(base):
    pad = -np.ones(LANES // 4, np.int64)
    return np.concatenate([base + np.arange(0, QK_ROPE, 2), pad, base + np.arange(1, QK_ROPE, 2), pad])


def _take_cols(w, cols):
    w_ext = jnp.concatenate([w, jnp.zeros((w.shape[0], 1), w.dtype)], axis=1)
    return w_ext[:, np.where(cols < 0, w.shape[1], cols)]


def _prep_weights(w_in, w_q_up, q_norm, k_norm):
    o_kr = Q_LORA + KV_LORA
    o_gq = o_kr + QK_ROPE
    o_gk = o_gq + GQA_HEADS * HEAD_DIM
    o_gv = o_gk + GQA_KV_HEADS * HEAD_DIM
    cols = [np.arange(0, o_kr), _rope_pad_cols(o_kr)]
    for hd in range(GQA_HEADS):
        cols.append(o_gq + hd * HEAD_DIM + _deinterleave(HEAD_DIM))
    for hd in range(GQA_KV_HEADS):
        cols.append(o_gk + hd * HEAD_DIM + _deinterleave(HEAD_DIM))
    cols.append(np.arange(o_gv, o_gv + GQA_KV_HEADS * HEAD_DIM))
    win = _take_cols(w_in, np.concatenate(cols)).astype(BF16)

    qcols = []
    for hd in range(MLA_HEADS):
        base = hd * (QK_NOPE + QK_ROPE)
        qcols += [base + np.arange(QK_NOPE), _rope_pad_cols(base + QK_NOPE)]
    wq = _take_cols(w_q_up, np.concatenate(qcols)).astype(BF16)
    perm = _deinterleave(HEAD_DIM)
    return win, wq, q_norm[perm][None, :], k_norm[perm][None, :]


def _rope_tables(s):
    rows = s // GRID_W
    row = jnp.repeat(jnp.arange(rows, dtype=F32), GRID_W)
    col = jnp.tile(jnp.arange(GRID_W, dtype=F32), rows)

    def cos_sin(rot_dim):
        n_pairs = rot_dim // 4
        freqs = ROPE_THETA ** (-jnp.arange(n_pairs, dtype=F32) * 2.0 / (rot_dim // 2))
        ang = jnp.concatenate([row[:, None] * freqs[None, :], col[:, None] * freqs[None, :]], axis=-1)
        return jnp.cos(ang), jnp.sin(ang)

    ca, sa = cos_sin(QK_ROPE)
    cb, sb = cos_sin(HEAD_DIM)
    return (jnp.concatenate([ca, ca, ca, ca], -1), jnp.concatenate([-sa, -sa, sa, sa], -1),
            jnp.concatenate([cb, cb], -1), jnp.concatenate([-sb, sb], -1))


def _pick(n, pref):
    return pref if n % pref == 0 else n


def _route_plan(route, tb):
    t = route.shape[0]
    a = t * TOP_K
    expert_id = route[:, 0:TOP_K].astype(jnp.int32).reshape(a)
    onehot = (expert_id[:, None] == jnp.arange(N_EXPERTS, dtype=jnp.int32)[None, :]).astype(jnp.int32)
    csum = jnp.cumsum(onehot, axis=0)
    counts = csum[-1]
    rank = jnp.sum(onehot * csum, axis=1) - 1
    padded = ((counts + tb - 1) // tb) * tb
    pad_ends = jnp.cumsum(padded)
    pad_off = pad_ends - padded
    dest = pad_off[expert_id] + rank
    nb = (a + N_EXPERTS * tb) // tb
    token_id = jnp.repeat(jnp.arange(t, dtype=jnp.int32), TOP_K)
    pad_tok = jnp.zeros((nb * tb,), jnp.int32).at[dest].set(token_id, unique_indices=True)
    blk_start = jnp.arange(nb, dtype=jnp.int32) * tb
    blk_e = jnp.minimum(jnp.sum(blk_start[:, None] >= pad_ends[None, :], axis=1), N_EXPERTS - 1)
    n_used = (pad_ends[-1] // tb).astype(jnp.int32).reshape(1)
    last_e = blk_e[jnp.maximum(n_used[0] - 1, 0)]
    blk_e = jnp.where(jnp.arange(nb) < n_used[0], blk_e, last_e).astype(jnp.int32)
    return pad_tok.reshape(nb, 1, tb), dest.reshape(t, TOP_K), blk_e, n_used


def _trunk(x, p):
    b, s, d = x.shape
    t = b * s
    tm = _pick(s, 256)
    qa, ka, va, qb, kb, vb = _pre_call(x, p["gattn"], p["win"], p["gqa"], p["wq"], p["gkv"], p["wkv"],
                                       p["gqn"], p["gkn"], *p["tables"], tm)
    tk = 2 * tm if s % (4 * tm) == 0 else s // 2
    oa = _flash_call(qa, ka, va, _pick(s, 512), tk, "mla_flash")
    ob = _flash_call(qb, kb, vb, _pick(s, 512), tk, "gqa_flash")
    x1, hp, route = _post_call(oa.reshape(t, -1), ob.reshape(t, -1), x.reshape(t, d), p["wout"], p["gffn"],
                               p["wr"], p["br"], tm)
    tb = 256
    pad_tok, dest, blk_e, n_used = _route_plan(route, tb)
    rows = _moe_call(blk_e, n_used, pad_tok, hp, p["wg"], p["wu"], p["wd"], tb)
    nt = t // tm
    dest_blk = dest.reshape(nt, tm, TOP_K).transpose(0, 2, 1).reshape(nt, 1, TOP_K * tm)
    y = _final_call(dest_blk, x1, route, rows, p["gfin"], tm)
    return y.reshape(b, s, d)


def kernel(x_prompt, x_sample, attn_norm, w_in, q_a_norm, w_q_up, kv_a_norm, w_kv_up, q_norm, k_norm, w_out,
           ffn_norm, w_group, b_group, w_expert, b_expert, w_gate, w_up, w_down, final_norm):
    assert attn_norm.shape[0] == 1, "single-layer trunk"
    win, wq, gqn, gkn = _prep_weights(w_in[0], w_q_up[0], q_norm[0], k_norm[0])
    d = w_in.shape[1]
    wr = jnp.zeros((d, LANES), F32).at[:, 0:N_GROUPS].set(w_group[0])
    wr = wr.at[:, N_GROUPS:N_GROUPS + N_EXPERTS].set(w_expert[0])
    br = jnp.zeros((1, LANES), F32).at[0, 0:N_GROUPS].set(b_group[0])
    br = br.at[0, N_GROUPS:N_GROUPS + N_EXPERTS].set(b_expert[0])
    p = dict(gattn=attn_norm, win=win, gqa=q_a_norm, wq=wq, gkv=kv_a_norm, wkv=w_kv_up[0].astype(BF16),
             gqn=gqn, gkn=gkn, wout=w_out[0].astype(BF16), gffn=ffn_norm, wr=wr, br=br,
             wg=w_gate[0].astype(BF16), wu=w_up[0].astype(BF16), wd=w_down[0].astype(BF16),
             gfin=final_norm[None, :])
    outs = []
    for x in (x_prompt, x_sample):
        p["tables"] = _rope_tables(x.shape[1])
        outs.append(_trunk(x, p))
    return tuple(outs)
```

```python
import functools
import math

import numpy as np
import jax
import jax.numpy as jnp
from jax import lax
from jax.experimental import pallas as pl
from jax.experimental.pallas import tpu as pltpu

F32 = jnp.float32
BF16 = jnp.bfloat16

EPS = 1e-6
ROPE_THETA = 10000.0
GRID_W = 64
LANES = 128
BF16_ROWS = 16

MLA_HEADS = 8
Q_LORA = 512
KV_LORA = 256
QK_NOPE = 128
QK_ROPE = 64
V_HEAD = 128
MLA_DK = 256
GQA_HEADS = 8
GQA_KV_HEADS = 2
GQA_REP = GQA_HEADS // GQA_KV_HEADS
HEAD_DIM = 128
N_GROUPS = 4
EXPERTS_PER_GROUP = 8
N_EXPERTS = N_GROUPS * EXPERTS_PER_GROUP
TOP_K = 2
LOG2E = math.log2(math.e)

VMEM_LIMIT = 56 * 1024 * 1024


def _cparams(sem):
    return pltpu.CompilerParams(dimension_semantics=sem, vmem_limit_bytes=VMEM_LIMIT)


def _rms(x, gain):
    return x * lax.rsqrt(jnp.mean(x * x, axis=-1, keepdims=True) + EPS) * gain


def _rope(x, cos_t, sin_t):
    return x * cos_t + pltpu.roll(x, LANES // 2, axis=1) * sin_t


def _pre_kernel(x_ref, gattn_ref, win_ref, gqa_ref, wq_ref, gkv_ref, wkv_ref, gqn_ref, gkn_ref,
                ca_ref, sa_ref, cb_ref, sb_ref,
                qa_ref, ka_ref, va_ref, qb_ref, kb_ref, vb_ref):
    x = x_ref[0]
    h = _rms(x, gattn_ref[...]).astype(BF16)
    proj = jnp.dot(h, win_ref[...], preferred_element_type=F32)
    o = 0
    q_lat = proj[:, o:o + Q_LORA]; o += Q_LORA
    kv_lat = proj[:, o:o + KV_LORA]; o += KV_LORA
    k_rope = proj[:, o:o + LANES]; o += LANES
    gq = proj[:, o:o + GQA_HEADS * HEAD_DIM]; o += GQA_HEADS * HEAD_DIM
    gk = proj[:, o:o + GQA_KV_HEADS * HEAD_DIM]; o += GQA_KV_HEADS * HEAD_DIM
    gv = proj[:, o:o + GQA_KV_HEADS * HEAD_DIM]

    ca, sa, cb, sb = ca_ref[...], sa_ref[...], cb_ref[...], sb_ref[...]

    q = jnp.dot(_rms(q_lat, gqa_ref[...]).astype(BF16), wq_ref[...], preferred_element_type=F32)
    kv = jnp.dot(_rms(kv_lat, gkv_ref[...]).astype(BF16), wkv_ref[...], preferred_element_type=F32)
    k_pe = _rope(k_rope, ca, sa).astype(BF16)
    sc_a = np.float32((QK_NOPE + QK_ROPE) ** -0.5 * LOG2E)
    for hd in range(MLA_HEADS):
        qh = q[:, hd * MLA_DK:(hd + 1) * MLA_DK]
        qa_ref[0, hd, :, 0:LANES] = (qh[:, 0:LANES] * sc_a).astype(BF16)
        qa_ref[0, hd, :, LANES:MLA_DK] = (_rope(qh[:, LANES:MLA_DK], ca, sa) * sc_a).astype(BF16)
        kvh = kv[:, hd * 2 * LANES:(hd + 1) * 2 * LANES]
        ka_ref[0, hd, :, 0:LANES] = kvh[:, 0:LANES].astype(BF16)
        ka_ref[0, hd, :, LANES:MLA_DK] = k_pe
        va_ref[0, hd, 0] = kvh[:, LANES:2 * LANES].T.astype(BF16)

    sc_b = np.float32(HEAD_DIM ** -0.5 * LOG2E)
    gqn, gkn = gqn_ref[...], gkn_ref[...]
    for hd in range(GQA_HEADS):
        qh = _rope(_rms(gq[:, hd * HEAD_DIM:(hd + 1) * HEAD_DIM], gqn), cb, sb)
        qb_ref[0, hd] = (qh * sc_b).astype(BF16)
    for hd in range(GQA_KV_HEADS):
        kh = _rope(_rms(gk[:, hd * HEAD_DIM:(hd + 1) * HEAD_DIM], gkn), cb, sb)
        kb_ref[0, hd] = kh.astype(BF16)
        vb_ref[0, hd, 0] = gv[:, hd * HEAD_DIM:(hd + 1) * HEAD_DIM].T.astype(BF16)


def _pre_call(x, gattn, win, gqa, wq, gkv, wkv, gqn, gkn, ca, sa, cb, sb, tm):
    b, s, d = x.shape
    const = lambda shape: pl.BlockSpec(shape, lambda bi, si: (0,) * len(shape))
    tab = pl.BlockSpec((tm, LANES), lambda bi, si: (si, 0))
    head = lambda nh, w: pl.BlockSpec((1, nh, tm, w), lambda bi, si: (bi, 0, si, 0))
    head_t = lambda nh, w: pl.BlockSpec((1, nh, 1, w, tm), lambda bi, si: (bi, 0, si, 0, 0))
    return pl.pallas_call(
        _pre_kernel,
        grid=(b, s // tm),
        in_specs=[pl.BlockSpec((1, tm, d), lambda bi, si: (bi, si, 0)),
                  const(gattn.shape), const(win.shape), const(gqa.shape), const(wq.shape),
                  const(gkv.shape), const(wkv.shape), const(gqn.shape), const(gkn.shape),
                  tab, tab, tab, tab],
        out_specs=[head(MLA_HEADS, MLA_DK), head(MLA_HEADS, MLA_DK), head_t(MLA_HEADS, V_HEAD),
                   head(GQA_HEADS, HEAD_DIM), head(GQA_KV_HEADS, HEAD_DIM), head_t(GQA_KV_HEADS, HEAD_DIM)],
        out_shape=[jax.ShapeDtypeStruct((b, MLA_HEADS, s, MLA_DK), BF16),
                   jax.ShapeDtypeStruct((b, MLA_HEADS, s, MLA_DK), BF16),
                   jax.ShapeDtypeStruct((b, MLA_HEADS, s // tm, V_HEAD, tm), BF16),
                   jax.ShapeDtypeStruct((b, GQA_HEADS, s, HEAD_DIM), BF16),
                   jax.ShapeDtypeStruct((b, GQA_KV_HEADS, s, HEAD_DIM), BF16),
                   jax.ShapeDtypeStruct((b, GQA_KV_HEADS, s // tm, HEAD_DIM, tm), BF16)],
        compiler_params=_cparams(("parallel", "parallel")),
        name="pre_proj",
    )(x, gattn, win, gqa, wq, gkv, wkv, gqn, gkn, ca, sa, cb, sb)


def _flash_kernel(q_ref, k_ref, vt_ref, o_ref, acc_sc, s_sc, *, rep, tk):
    s_len = k_ref.shape[2]
    dv, cw = vt_ref.shape[3], vt_ref.shape[4]
    tq = q_ref.shape[2]
    n_sub = tk // cw
    n_steps = s_len // tk
    unroll = 8 if n_steps % 8 == 0 else 2
    ones = jnp.ones((BF16_ROWS, cw), BF16)

    for r in range(rep):
        q = q_ref[0, r]

        def scores(j, slot):
            k = k_ref[0, 0, pl.ds(pl.multiple_of(j * tk, tk), tk), :]
            st = lax.dot_general(k, q, (((1,), (1,)), ((), ())), preferred_element_type=F32)
            s_sc[slot] = st
            return jnp.max(st, axis=0, keepdims=True)

        acc_sc[...] = jnp.zeros(acc_sc.shape, F32)
        mx0 = scores(0, 0)

        def body(jj, carry):
            m_prev, mx = carry
            for si in range(unroll):
                j = jj * unroll + si
                mx_next = scores(jnp.minimum(j + 1, n_steps - 1), 1 - si % 2)
                m_new = jnp.maximum(m_prev, mx)
                alpha = jnp.exp2(m_prev - m_new)
                pb = jnp.exp2(s_sc[si % 2] - m_new).astype(BF16)
                pv = None
                for c in range(n_sub):
                    vt1 = jnp.concatenate([vt_ref[0, 0, j * n_sub + c], ones], axis=0)
                    d = jnp.dot(vt1, pb[c * cw:(c + 1) * cw], preferred_element_type=F32)
                    pv = d if pv is None else pv + d
                acc_sc[...] = alpha * acc_sc[...] + pv
                m_prev, mx = m_new, mx_next
            return m_prev, mx

        lax.fori_loop(0, n_steps // unroll, body, (jnp.full((1, tq), -jnp.inf, F32), mx0))
        out_t = acc_sc[0:dv, :] / acc_sc[dv:dv + 1, :]
        o_ref[0, :, r * dv:(r + 1) * dv] = out_t.T.astype(o_ref.dtype)


def _flash_call(q, k, vt, tq, tk, name):
    b, hq, s, dk = q.shape
    hkv = k.shape[1]
    _, _, nchunk, dv, cw = vt.shape
    rep = hq // hkv
    assert tk % cw == 0 and (s // tk) % 2 == 0 and s % tq == 0
    return pl.pallas_call(
        functools.partial(_flash_kernel, rep=rep, tk=tk),
        grid=(b, hkv, s // tq),
        in_specs=[pl.BlockSpec((1, rep, tq, dk), lambda bi, g, qi: (bi, g, qi, 0)),
                  pl.BlockSpec((1, 1, s, dk), lambda bi, g, qi: (bi, g, 0, 0)),
                  pl.BlockSpec((1, 1, nchunk, dv, cw), lambda bi, g, qi: (bi, g, 0, 0, 0))],
        out_specs=pl.BlockSpec((1, tq, rep * dv), lambda bi, g, qi: (bi, qi, g)),
        out_shape=jax.ShapeDtypeStruct((b, s, hq * dv), BF16),
        scratch_shapes=[pltpu.VMEM((dv + BF16_ROWS, tq), F32), pltpu.VMEM((2, tk, tq), F32)],
        compiler_params=_cparams(("parallel", "parallel", "arbitrary")),
        name=name,
    )(q, k, vt)


def _post_kernel(oa_ref, ob_ref, x_ref, wout_ref, gffn_ref, wr_ref, br_ref, x1_ref, hp_ref, route_ref):
    half = oa_ref.shape[-1]
    mix = jnp.dot(oa_ref[...], wout_ref[0:half, :], preferred_element_type=F32)
    mix = mix + jnp.dot(ob_ref[...], wout_ref[half:2 * half, :], preferred_element_type=F32)
    x1 = x_ref[...] + mix
    x1_ref[...] = x1
    h = _rms(x1, gffn_ref[...])

    dh = h.shape[-1] // 2
    lo = pltpu.bitcast(h[:, 0:dh].astype(BF16).astype(F32), jnp.uint32)
    hi = pltpu.bitcast(h[:, dh:2 * dh].astype(BF16).astype(F32), jnp.uint32)
    hp_ref[...] = (hi & jnp.uint32(0xFFFF0000)) | (lo >> 16)

    lg = jnp.dot(h, wr_ref[...], preferred_element_type=F32, precision=lax.Precision.HIGHEST) + br_ref[...]
    lane = lax.broadcasted_iota(jnp.int32, lg.shape, 1)
    neg = jnp.float32(-jnp.inf)
    big = jnp.int32(4 * LANES)
    gl = jnp.where(lane < N_GROUPS, lg, neg)
    gmax = jnp.max(gl, axis=-1, keepdims=True)
    gsum = jnp.sum(jnp.exp(gl - gmax), axis=-1, keepdims=True)
    g_w = 1.0 / gsum
    g_idx = jnp.min(jnp.where(gl == gmax, lane, big), axis=-1, keepdims=True)
    lo_lane = N_GROUPS + EXPERTS_PER_GROUP * g_idx
    el = jnp.where((lane >= lo_lane) & (lane < lo_lane + EXPERTS_PER_GROUP), lg, neg)
    m1 = jnp.max(el, axis=-1, keepdims=True)
    i1 = jnp.min(jnp.where(el == m1, lane, big), axis=-1, keepdims=True)
    el2 = jnp.where(lane == i1, neg, el)
    m2 = jnp.max(el2, axis=-1, keepdims=True)
    i2 = jnp.min(jnp.where(el2 == m2, lane, big), axis=-1, keepdims=True)
    e2 = jnp.exp(m2 - m1)
    w1 = 1.0 / (1.0 + e2)
    w2 = e2 / (1.0 + e2)
    out = jnp.where(lane == 0, (i1 - N_GROUPS).astype(F32),
          jnp.where(lane == 1, (i2 - N_GROUPS).astype(F32),
          jnp.where(lane == 2, g_w * w1,
          jnp.where(lane == 3, g_w * w2, 0.0))))
    route_ref[...] = out


def _post_call(oa, ob, x, wout, gffn, wr, br, tm):
    t, d = x.shape
    half = oa.shape[-1]
    const = lambda shape: pl.BlockSpec(shape, lambda i: (0,) * len(shape))
    row = lambda w: pl.BlockSpec((tm, w), lambda i: (i, 0))
    return pl.pallas_call(
        _post_kernel,
        grid=(t // tm,),
        in_specs=[row(half), row(half), row(d), const(wout.shape), const(gffn.shape),
                  const(wr.shape), const(br.shape)],
        out_specs=[row(d), row(d // 2), row(LANES)],
        out_shape=[jax.ShapeDtypeStruct((t, d), F32),
                   jax.ShapeDtypeStruct((t, d // 2), jnp.uint32),
                   jax.ShapeDtypeStruct((t, LANES), F32)],
        compiler_params=_cparams(("parallel",)),
        name="post_proj_router",
    )(oa, ob, x, wout, gffn, wr, br)


def _row_gather(src_hbm, idx_ref, buf, sem, n_rows):
    def body(r, carry):
        pltpu.make_async_copy(src_hbm.at[pl.ds(idx_ref[0, 0, r], 1)], buf.at[pl.ds(r, 1)], sem).start()
        return carry
    lax.fori_loop(0, n_rows, body, 0)


def _moe_kernel(blk_e_ref, n_used_ref, tok_cur_ref, tok_next_ref, hp_hbm, wg_ref, wu_ref, wd_ref,
                out_ref, buf, sem):
    i = pl.program_id(0)
    slot = i % 2
    tb = buf.shape[1]
    n_used = n_used_ref[0]

    @pl.when(i == 0)
    def _():
        _row_gather(hp_hbm, tok_cur_ref, buf.at[0], sem.at[0], tb)

    @pl.when(i + 1 < n_used)
    def _():
        _row_gather(hp_hbm, tok_next_ref, buf.at[1 - slot], sem.at[1 - slot], tb)

    @pl.when(i < n_used)
    def _():
        pltpu.make_async_copy(hp_hbm.at[pl.ds(0, tb)], buf.at[slot], sem.at[slot]).wait()
        w = buf[slot]
        dh = w.shape[-1]
        x_lo = pltpu.bitcast(w << 16, F32).astype(BF16)
        x_hi = pltpu.bitcast(w & jnp.uint32(0xFFFF0000), F32).astype(BF16)
        g = jnp.dot(x_lo, wg_ref[0, 0:dh, :], preferred_element_type=F32)
        g = g + jnp.dot(x_hi, wg_ref[0, dh:2 * dh, :], preferred_element_type=F32)
        u = jnp.dot(x_lo, wu_ref[0, 0:dh, :], preferred_element_type=F32)
        u = u + jnp.dot(x_hi, wu_ref[0, dh:2 * dh, :], preferred_element_type=F32)
        hid = (g * jax.nn.sigmoid(g)) * u
        out_ref[...] = jnp.dot(hid.astype(BF16), wd_ref[0], preferred_element_type=F32)

    @pl.when(i >= n_used)
    def _():
        out_ref[...] = jnp.zeros(out_ref.shape, out_ref.dtype)


def _moe_call(blk_e, n_used, pad_tok, hp, wg, wu, wd, tb):
    nb = pad_tok.shape[0]
    dh = hp.shape[-1]
    d = 2 * dh
    de = wg.shape[-1]
    smem_blk = lambda f: pl.BlockSpec((1, 1, tb), f, memory_space=pltpu.SMEM)
    wspec = lambda shape: pl.BlockSpec(shape, lambda i, be, nu: (be[i], 0, 0))
    grid_spec = pltpu.PrefetchScalarGridSpec(
        num_scalar_prefetch=2,
        grid=(nb,),
        in_specs=[smem_blk(lambda i, be, nu: (i, 0, 0)),
                  smem_blk(lambda i, be, nu: (jnp.minimum(i + 1, nb - 1), 0, 0)),
                  pl.BlockSpec(memory_space=pl.ANY),
                  wspec((1, d, de)), wspec((1, d, de)), wspec((1, de, d))],
        out_specs=pl.BlockSpec((tb, d), lambda i, be, nu: (i, 0)),
        scratch_shapes=[pltpu.VMEM((2, tb, dh), jnp.uint32), pltpu.SemaphoreType.DMA((2,))],
    )
    return pl.pallas_call(
        _moe_kernel,
        grid_spec=grid_spec,
        out_shape=jax.ShapeDtypeStruct((nb * tb, d), F32),
        compiler_params=_cparams(("arbitrary",)),
        name="moe_experts",
    )(blk_e, n_used, pad_tok, pad_tok, hp, wg, wu, wd)


def _final_kernel(dest_cur_ref, dest_next_ref, x1_ref, route_ref, rows_hbm, gfin_ref, y_ref, buf, sem):
    i = pl.program_id(0)
    nsteps = pl.num_programs(0)
    slot = i % 2
    n_rows = buf.shape[1]
    tm = n_rows // TOP_K

    @pl.when(i == 0)
    def _():
        _row_gather(rows_hbm, dest_cur_ref, buf.at[0], sem.at[0], n_rows)

    @pl.when(i + 1 < nsteps)
    def _():
        _row_gather(rows_hbm, dest_next_ref, buf.at[1 - slot], sem.at[1 - slot], n_rows)

    pltpu.make_async_copy(rows_hbm.at[pl.ds(0, n_rows)], buf.at[slot], sem.at[slot]).wait()
    route = route_ref[...]
    r0 = buf[slot, 0:tm, :] * route[:, 2:3]
    r1 = buf[slot, tm:2 * tm, :] * route[:, 3:4]
    y_ref[...] = _rms(x1_ref[...] + (r0 + r1), gfin_ref[...])


def _final_call(dest, x1, route, rows, gfin, tm):
    t, d = x1.shape
    nt = t // tm
    smem_blk = lambda f: pl.BlockSpec((1, 1, TOP_K * tm), f, memory_space=pltpu.SMEM)
    return pl.pallas_call(
        _final_kernel,
        grid=(nt,),
        in_specs=[smem_blk(lambda i: (i, 0, 0)),
                  smem_blk(lambda i: (jnp.minimum(i + 1, nt - 1), 0, 0)),
                  pl.BlockSpec((tm, d), lambda i: (i, 0)),
                  pl.BlockSpec((tm, LANES), lambda i: (i, 0)),
                  pl.BlockSpec(memory_space=pl.ANY),
                  pl.BlockSpec(gfin.shape, lambda i: (0, 0))],
        out_specs=pl.BlockSpec((tm, d), lambda i: (i, 0)),
        out_shape=jax.ShapeDtypeStruct((t, d), F32),
        scratch_shapes=[pltpu.VMEM((2, TOP_K * tm, d), F32), pltpu.SemaphoreType.DMA((2,))],
        compiler_params=_cparams(("arbitrary",)),
        name="combine_final_norm",
    )(dest, dest, x1, route, rows, gfin)


def _deinterleave(n):
    return np.concatenate([np.arange(0, n, 2), np.arange(1, n, 2)])


def _rope_pad_cols(base):
    pad = -np.ones(LANES // 4, np.int64)
    return np.concatenate([base + np.arange(0, QK_ROPE, 2), pad, base + np.arange(1, QK_ROPE, 2), pad])


def _take_cols(w, cols):
    w_ext = jnp.concatenate([w, jnp.zeros((w.shape[0], 1), w.dtype)], axis=1)
    return w_ext[:, np.where(cols < 0, w.shape[1], cols)]


def _prep_weights(w_in, w_q_up, q_norm, k_norm):
    o_kr = Q_LORA + KV_LORA
    o_gq = o_kr + QK_ROPE
    o_gk = o_gq + GQA_HEADS * HEAD_DIM
    o_gv = o_gk + GQA_KV_HEADS * HEAD_DIM
    cols = [np.arange(0, o_kr), _rope_pad_cols(o_kr)]
    for hd in range(GQA_HEADS):
        cols.append(o_gq + hd * HEAD_DIM + _deinterleave(HEAD_DIM))
    for hd in range(GQA_KV_HEADS):
        cols.append(o_gk + hd * HEAD_DIM + _deinterleave(HEAD_DIM))
    cols.append(np.arange(o_gv, o_gv + GQA_KV_HEADS * HEAD_DIM))
    win = _take_cols(w_in, np.concatenate(cols)).astype(BF16)

    qcols = []
    for hd in range(MLA_HEADS):
        base = hd * (QK_NOPE + QK_ROPE)
        qcols += [base + np.arange(QK_NOPE), _rope_pad_cols(base + QK_NOPE)]
    wq = _take_cols(w_q_up, np.concatenate(qcols)).astype(BF16)
    perm = _deinterleave(HEAD_DIM)
    return win, wq, q_norm[perm][None, :], k_norm[perm][None, :]


def _rope_tables(s):
    rows = s // GRID_W
    row = jnp.repeat(jnp.arange(rows, dtype=F32), GRID_W)
    col = jnp.tile(jnp.arange(GRID_W, dtype=F32), rows)

    def cos_sin(rot_dim):
        n_pairs = rot_dim // 4
        freqs = ROPE_THETA ** (-jnp.arange(n_pairs, dtype=F32) * 2.0 / (rot_dim // 2))
        ang = jnp.concatenate([row[:, None] * freqs[None, :], col[:, None] * freqs[None, :]], axis=-1)
        return jnp.cos(ang), jnp.sin(ang)

    ca, sa = cos_sin(QK_ROPE)
    cb, sb = cos_sin(HEAD_DIM)
    return (jnp.concatenate([ca, ca, ca, ca], -1), jnp.concatenate([-sa, -sa, sa, sa], -1),
            jnp.concatenate([cb, cb], -1), jnp.concatenate([-sb, sb], -1))


def _pick(n, pref):
    return pref if n % pref == 0 else n


def _route_plan(route, tb):
    t = route.shape[0]
    a = t * TOP_K
    expert_id = route[:, 0:TOP_K].astype(jnp.int32).reshape(a)
    onehot = (expert_id[:, None] == jnp.arange(N_EXPERTS, dtype=jnp.int32)[None, :]).astype(jnp.int32)
    csum = jnp.cumsum(onehot, axis=0)
    counts = csum[-1]
    rank = jnp.sum(onehot * csum, axis=1) - 1
    padded = ((counts + tb - 1) // tb) * tb
    pad_ends = jnp.cumsum(padded)
    pad_off = pad_ends - padded
    dest = pad_off[expert_id] + rank
    nb = (a + N_EXPERTS * tb) // tb
    token_id = jnp.repeat(jnp.arange(t, dtype=jnp.int32), TOP_K)
    pad_tok = jnp.zeros((nb * tb,), jnp.int32).at[dest].set(token_id, unique_indices=True)
    blk_start = jnp.arange(nb, dtype=jnp.int32) * tb
    blk_e = jnp.minimum(jnp.sum(blk_start[:, None] >= pad_ends[None, :], axis=1), N_EXPERTS - 1)
    n_used = (pad_ends[-1] // tb).astype(jnp.int32).reshape(1)
    last_e = blk_e[jnp.maximum(n_used[0] - 1, 0)]
    blk_e = jnp.where(jnp.arange(nb) < n_used[0], blk_e, last_e).astype(jnp.int32)
    return pad_tok.reshape(nb, 1, tb), dest.reshape(t, TOP_K), blk_e, n_used


def _trunk(x, p):
    b, s, d = x.shape
    t = b * s
    tm = _pick(s, 256)
    qa, ka, va, qb, kb, vb = _pre_call(x, p["gattn"], p["win"], p["gqa"], p["wq"], p["gkv"], p["wkv"],
                                       p["gqn"], p["gkn"], *p["tables"], tm)
    tk = 2 * tm if s % (4 * tm) == 0 else s // 2
    oa = _flash_call(qa, ka, va, _pick(s, 512), tk, "mla_flash")
    ob = _flash_call(qb, kb, vb, _pick(s, 512), tk, "gqa_flash")
    x1, hp, route = _post_call(oa.reshape(t, -1), ob.reshape(t, -1), x.reshape(t, d), p["wout"], p["gffn"],
                               p["wr"], p["br"], tm)
    tb = 256
    pad_tok, dest, blk_e, n_used = _route_plan(route, tb)
    rows = _moe_call(blk_e, n_used, pad_tok, hp, p["wg"], p["wu"], p["wd"], tb)
    nt = t // tm
    dest_blk = dest.reshape(nt, tm, TOP_K).transpose(0, 2, 1).reshape(nt, 1, TOP_K * tm)
    y = _final_call(dest_blk, x1, route, rows, p["gfin"], tm)
    return y.reshape(b, s, d)


def kernel(x_prompt, x_sample, attn_norm, w_in, q_a_norm, w_q_up, kv_a_norm, w_kv_up, q_norm, k_norm, w_out,
           ffn_norm, w_group, b_group, w_expert, b_expert, w_gate, w_up, w_down, final_norm):
    assert attn_norm.shape[0] == 1, "single-layer trunk"
    win, wq, gqn, gkn = _prep_weights(w_in[0], w_q_up[0], q_norm[0], k_norm[0])
    d = w_in.shape[1]
    wr = jnp.zeros((d, LANES), F32).at[:, 0:N_GROUPS].set(w_group[0])
    wr = wr.at[:, N_GROUPS:N_GROUPS + N_EXPERTS].set(w_expert[0])
    br = jnp.zeros((1, LANES), F32).at[0, 0:N_GROUPS].set(b_group[0])
    br = br.at[0, N_GROUPS:N_GROUPS + N_EXPERTS].set(b_expert[0])
    p = dict(gattn=attn_norm, win=win, gqa=q_a_norm, wq=wq, gkv=kv_a_norm, wkv=w_kv_up[0].astype(BF16),
             gqn=gqn, gkn=gkn, wout=w_out[0].astype(BF16), gffn=ffn_norm, wr=wr, br=br,
             wg=w_gate[0].astype(BF16), wu=w_up[0].astype(BF16), wd=w_down[0].astype(BF16),
             gfin=final_norm[None, :])
    outs = []
    for x in (x_prompt, x_sample):
        p["tables"] = _rope_tables(x.shape[1])
        outs.append(_trunk(x, p))
    return tuple(outs)
```

```python
import functools
import math

import numpy as np
import jax
import jax.numpy as jnp
from jax import lax
from jax.experimental import pallas as pl
from jax.experimental.pallas import tpu as pltpu

F32 = jnp.float32
BF16 = jnp.bfloat16

EPS = 1e-6
ROPE_THETA = 10000.0
GRID_W = 64
LANES = 128
BF16_ROWS = 16
GATHER_UNROLL = 8

MLA_HEADS = 8
Q_LORA = 512
KV_LORA = 256
QK_NOPE = 128
QK_ROPE = 64
V_HEAD = 128
MLA_DK = 256
GQA_HEADS = 8
GQA_KV_HEADS = 2
GQA_REP = GQA_HEADS // GQA_KV_HEADS
HEAD_DIM = 128
N_GROUPS = 4
EXPERTS_PER_GROUP = 8
N_EXPERTS = N_GROUPS * EXPERTS_PER_GROUP
TOP_K = 2
LOG2E = math.log2(math.e)

VMEM_LIMIT = 56 * 1024 * 1024


def _cparams(sem):
    return pltpu.CompilerParams(dimension_semantics=sem, vmem_limit_bytes=VMEM_LIMIT)


def _rms(x, gain):
    return x * lax.rsqrt(jnp.mean(x * x, axis=-1, keepdims=True) + EPS) * gain


def _rope(x, cos_t, sin_t):
    return x * cos_t + pltpu.roll(x, LANES // 2, axis=1) * sin_t


def _pre_kernel(x_ref, gattn_ref, win_ref, gqa_ref, wq_ref, gkv_ref, wkv_ref, gqn_ref, gkn_ref,
                ca_ref, sa_ref, cb_ref, sb_ref,
                qa_ref, ka_ref, va_ref, qb_ref, kb_ref, vb_ref):
    x = x_ref[0]
    h = _rms(x, gattn_ref[...]).astype(BF16)
    proj = jnp.dot(h, win_ref[...], preferred_element_type=F32)
    o = 0
    q_lat = proj[:, o:o + Q_LORA]; o += Q_LORA
    kv_lat = proj[:, o:o + KV_LORA]; o += KV_LORA
    k_rope = proj[:, o:o + LANES]; o += LANES
    gq = proj[:, o:o + GQA_HEADS * HEAD_DIM]; o += GQA_HEADS * HEAD_DIM
    gk = proj[:, o:o + GQA_KV_HEADS * HEAD_DIM]; o += GQA_KV_HEADS * HEAD_DIM
    gv = proj[:, o:o + GQA_KV_HEADS * HEAD_DIM]

    ca, sa, cb, sb = ca_ref[...], sa_ref[...], cb_ref[...], sb_ref[...]

    q = jnp.dot(_rms(q_lat, gqa_ref[...]).astype(BF16), wq_ref[...], preferred_element_type=F32)
    kv = jnp.dot(_rms(kv_lat, gkv_ref[...]).astype(BF16), wkv_ref[...], preferred_element_type=F32)
    k_pe = _rope(k_rope, ca, sa).astype(BF16)
    sc_a = np.float32((QK_NOPE + QK_ROPE) ** -0.5 * LOG2E)
    for hd in range(MLA_HEADS):
        qh = q[:, hd * MLA_DK:(hd + 1) * MLA_DK]
        qa_ref[0, hd, :, 0:LANES] = (qh[:, 0:LANES] * sc_a).astype(BF16)
        qa_ref[0, hd, :, LANES:MLA_DK] = (_rope(qh[:, LANES:MLA_DK], ca, sa) * sc_a).astype(BF16)
        kvh = kv[:, hd * 2 * LANES:(hd + 1) * 2 * LANES]
        ka_ref[0, hd, :, 0:LANES] = kvh[:, 0:LANES].astype(BF16)
        ka_ref[0, hd, :, LANES:MLA_DK] = k_pe
        va_ref[0, hd, 0] = kvh[:, LANES:2 * LANES].T.astype(BF16)

    sc_b = np.float32(HEAD_DIM ** -0.5 * LOG2E)
    gqn, gkn = gqn_ref[...], gkn_ref[...]
    for hd in range(GQA_HEADS):
        qh = _rope(_rms(gq[:, hd * HEAD_DIM:(hd + 1) * HEAD_DIM], gqn), cb, sb)
        qb_ref[0, hd] = (qh * sc_b).astype(BF16)
    for hd in range(GQA_KV_HEADS):
        kh = _rope(_rms(gk[:, hd * HEAD_DIM:(hd + 1) * HEAD_DIM], gkn), cb, sb)
        kb_ref[0, hd] = kh.astype(BF16)
        vb_ref[0, hd, 0] = gv[:, hd * HEAD_DIM:(hd + 1) * HEAD_DIM].T.astype(BF16)


def _pre_call(x, gattn, win, gqa, wq, gkv, wkv, gqn, gkn, ca, sa, cb, sb, tm):
    b, s, d = x.shape
    const = lambda shape: pl.BlockSpec(shape, lambda bi, si: (0,) * len(shape))
    tab = pl.BlockSpec((tm, LANES), lambda bi, si: (si, 0))
    head = lambda nh, w: pl.BlockSpec((1, nh, tm, w), lambda bi, si: (bi, 0, si, 0))
    head_t = lambda nh, w: pl.BlockSpec((1, nh, 1, w, tm), lambda bi, si: (bi, 0, si, 0, 0))
    return pl.pallas_call(
        _pre_kernel,
        grid=(b, s // tm),
        in_specs=[pl.BlockSpec((1, tm, d), lambda bi, si: (bi, si, 0)),
                  const(gattn.shape), const(win.shape), const(gqa.shape), const(wq.shape),
                  const(gkv.shape), const(wkv.shape), const(gqn.shape), const(gkn.shape),
                  tab, tab, tab, tab],
        out_specs=[head(MLA_HEADS, MLA_DK), head(MLA_HEADS, MLA_DK), head_t(MLA_HEADS, V_HEAD),
                   head(GQA_HEADS, HEAD_DIM), head(GQA_KV_HEADS, HEAD_DIM), head_t(GQA_KV_HEADS, HEAD_DIM)],
        out_shape=[jax.ShapeDtypeStruct((b, MLA_HEADS, s, MLA_DK), BF16),
                   jax.ShapeDtypeStruct((b, MLA_HEADS, s, MLA_DK), BF16),
                   jax.ShapeDtypeStruct((b, MLA_HEADS, s // tm, V_HEAD, tm), BF16),
                   jax.ShapeDtypeStruct((b, GQA_HEADS, s, HEAD_DIM), BF16),
                   jax.ShapeDtypeStruct((b, GQA_KV_HEADS, s, HEAD_DIM), BF16),
                   jax.ShapeDtypeStruct((b, GQA_KV_HEADS, s // tm, HEAD_DIM, tm), BF16)],
        compiler_params=_cparams(("parallel", "parallel")),
        name="pre_proj",
    )(x, gattn, win, gqa, wq, gkv, wkv, gqn, gkn, ca, sa, cb, sb)


def _flash_kernel(q_ref, k_ref, vt_ref, o_ref, acc_sc, s_sc, *, rep, tk):
    s_len = k_ref.shape[2]
    dv, cw = vt_ref.shape[3], vt_ref.shape[4]
    tq = q_ref.shape[2]
    n_sub = tk // cw
    n_steps = s_len // tk
    unroll = 8 if n_steps % 8 == 0 else 2
    ones = jnp.ones((BF16_ROWS, cw), BF16)

    for r in range(rep):
        q = q_ref[0, r]

        def scores(j, slot):
            k = k_ref[0, 0, pl.ds(pl.multiple_of(j * tk, tk), tk), :]
            st = lax.dot_general(k, q, (((1,), (1,)), ((), ())), preferred_element_type=F32)
            s_sc[slot] = st
            return jnp.max(st, axis=0, keepdims=True)

        acc_sc[...] = jnp.zeros(acc_sc.shape, F32)
        mx0 = scores(0, 0)

        def body(jj, carry):
            m_prev, mx = carry
            for si in range(unroll):
                j = jj * unroll + si
                mx_next = scores(jnp.minimum(j + 1, n_steps - 1), 1 - si % 2)
                m_new = jnp.maximum(m_prev, mx)
                alpha = jnp.exp2(m_prev - m_new)
                pb = jnp.exp2(s_sc[si % 2] - m_new).astype(BF16)
                pv = None
                for c in range(n_sub):
                    vt1 = jnp.concatenate([vt_ref[0, 0, j * n_sub + c], ones], axis=0)
                    d = jnp.dot(vt1, pb[c * cw:(c + 1) * cw], preferred_element_type=F32)
                    pv = d if pv is None else pv + d
                acc_sc[...] = alpha * acc_sc[...] + pv
                m_prev, mx = m_new, mx_next
            return m_prev, mx

        lax.fori_loop(0, n_steps // unroll, body, (jnp.full((1, tq), -jnp.inf, F32), mx0))
        out_t = acc_sc[0:dv, :] / acc_sc[dv:dv + 1, :]
        o_ref[0, :, r * dv:(r + 1) * dv] = out_t.T.astype(o_ref.dtype)


def _flash_call(q, k, vt, tq, tk, name):
    b, hq, s, dk = q.shape
    hkv = k.shape[1]
    _, _, nchunk, dv, cw = vt.shape
    rep = hq // hkv
    assert tk % cw == 0 and (s // tk) % 2 == 0 and s % tq == 0
    return pl.pallas_call(
        functools.partial(_flash_kernel, rep=rep, tk=tk),
        grid=(b, hkv, s // tq),
        in_specs=[pl.BlockSpec((1, rep, tq, dk), lambda bi, g, qi: (bi, g, qi, 0)),
                  pl.BlockSpec((1, 1, s, dk), lambda bi, g, qi: (bi, g, 0, 0)),
                  pl.BlockSpec((1, 1, nchunk, dv, cw), lambda bi, g, qi: (bi, g, 0, 0, 0))],
        out_specs=pl.BlockSpec((1, tq, rep * dv), lambda bi, g, qi: (bi, qi, g)),
        out_shape=jax.ShapeDtypeStruct((b, s, hq * dv), BF16),
        scratch_shapes=[pltpu.VMEM((dv + BF16_ROWS, tq), F32), pltpu.VMEM((2, tk, tq), F32)],
        compiler_params=_cparams(("parallel", "parallel", "arbitrary")),
        name=name,
    )(q, k, vt)


def _post_kernel(oa_ref, ob_ref, x_ref, wout_ref, gffn_ref, wr_ref, br_ref, x1_ref, hp_ref, route_ref):
    half = oa_ref.shape[-1]
    mix = jnp.dot(oa_ref[...], wout_ref[0:half, :], preferred_element_type=F32)
    mix = mix + jnp.dot(ob_ref[...], wout_ref[half:2 * half, :], preferred_element_type=F32)
    x1 = x_ref[...] + mix
    x1_ref[...] = x1
    h = _rms(x1, gffn_ref[...])

    dh = h.shape[-1] // 2
    h_hi = h.astype(BF16)
    h_hi32 = h_hi.astype(F32)
    lo = pltpu.bitcast(h_hi32[:, 0:dh], jnp.uint32)
    hi = pltpu.bitcast(h_hi32[:, dh:2 * dh], jnp.uint32)
    hp_ref[...] = hi | (lo >> 16)

    h_lo = (h - h_hi32).astype(BF16)
    lg2 = (jnp.dot(h_hi, wr_ref[...], preferred_element_type=F32)
           + jnp.dot(h_lo, wr_ref[...], preferred_element_type=F32))
    lg = lg2[:, 0:LANES] + lg2[:, LANES:2 * LANES] + br_ref[...]
    lane = lax.broadcasted_iota(jnp.int32, lg.shape, 1)
    neg = jnp.float32(-jnp.inf)
    big = jnp.int32(4 * LANES)
    gl = jnp.where(lane < N_GROUPS, lg, neg)
    gmax = jnp.max(gl, axis=-1, keepdims=True)
    gsum = jnp.sum(jnp.exp(gl - gmax), axis=-1, keepdims=True)
    g_w = 1.0 / gsum
    g_idx = jnp.min(jnp.where(gl == gmax, lane, big), axis=-1, keepdims=True)
    lo_lane = N_GROUPS + EXPERTS_PER_GROUP * g_idx
    el = jnp.where((lane >= lo_lane) & (lane < lo_lane + EXPERTS_PER_GROUP), lg, neg)
    m1 = jnp.max(el, axis=-1, keepdims=True)
    i1 = jnp.min(jnp.where(el == m1, lane, big), axis=-1, keepdims=True)
    el2 = jnp.where(lane == i1, neg, el)
    m2 = jnp.max(el2, axis=-1, keepdims=True)
    i2 = jnp.min(jnp.where(el2 == m2, lane, big), axis=-1, keepdims=True)
    e2 = jnp.exp(m2 - m1)
    w1 = 1.0 / (1.0 + e2)
    w2 = e2 / (1.0 + e2)
    out = jnp.where(lane == 0, (i1 - N_GROUPS).astype(F32),
          jnp.where(lane == 1, (i2 - N_GROUPS).astype(F32),
          jnp.where(lane == 2, g_w * w1,
          jnp.where(lane == 3, g_w * w2, 0.0))))
    route_ref[...] = out


def _post_call(oa, ob, x, wout, gffn, wr, br, tm):
    t, d = x.shape
    half = oa.shape[-1]
    const = lambda shape: pl.BlockSpec(shape, lambda i: (0,) * len(shape))
    row = lambda w: pl.BlockSpec((tm, w), lambda i: (i, 0))
    return pl.pallas_call(
        _post_kernel,
        grid=(t // tm,),
        in_specs=[row(half), row(half), row(d), const(wout.shape), const(gffn.shape),
                  const(wr.shape), const(br.shape)],
        out_specs=[row(d), row(d // 2), row(LANES)],
        out_shape=[jax.ShapeDtypeStruct((t, d), F32),
                   jax.ShapeDtypeStruct((t, d // 2), jnp.uint32),
                   jax.ShapeDtypeStruct((t, LANES), F32)],
        compiler_params=_cparams(("parallel",)),
        name="post_proj_router",
    )(oa, ob, x, wout, gffn, wr, br)


def _row_gather(src_hbm, idx_ref, buf, sem, n_rows):
    def body(r, carry):
        pltpu.make_async_copy(src_hbm.at[pl.ds(idx_ref[0, 0, r], 1)], buf.at[pl.ds(r, 1)], sem).start()
        return carry
    lax.fori_loop(0, n_rows, body, 0, unroll=GATHER_UNROLL)


def _moe_kernel(blk_e_ref, n_used_ref, tok_cur_ref, tok_next_ref, hp_hbm, wg_ref, wu_ref, wd_ref,
                out_ref, buf, sem):
    i = pl.program_id(0)
    slot = i % 2
    tb = buf.shape[1]
    n_used = n_used_ref[0]

    @pl.when(i == 0)
    def _():
        _row_gather(hp_hbm, tok_cur_ref, buf.at[0], sem.at[0], tb)

    @pl.when(i + 1 < n_used)
    def _():
        _row_gather(hp_hbm, tok_next_ref, buf.at[1 - slot], sem.at[1 - slot], tb)

    @pl.when(i < n_used)
    def _():
        pltpu.make_async_copy(hp_hbm.at[pl.ds(0, tb)], buf.at[slot], sem.at[slot]).wait()
        w = buf[slot]
        dh = w.shape[-1]
        x_lo = pltpu.bitcast(w << 16, F32).astype(BF16)
        x_hi = pltpu.bitcast(w & jnp.uint32(0xFFFF0000), F32).astype(BF16)
        g = jnp.dot(x_lo, wg_ref[0, 0:dh, :], preferred_element_type=F32)
        g = g + jnp.dot(x_hi, wg_ref[0, dh:2 * dh, :], preferred_element_type=F32)
        u = jnp.dot(x_lo, wu_ref[0, 0:dh, :], preferred_element_type=F32)
        u = u + jnp.dot(x_hi, wu_ref[0, dh:2 * dh, :], preferred_element_type=F32)
        hid = (g * jax.nn.sigmoid(g)) * u
        out_ref[...] = jnp.dot(hid.astype(BF16), wd_ref[0], preferred_element_type=F32)

    @pl.when(i >= n_used)
    def _():
        out_ref[...] = jnp.zeros(out_ref.shape, out_ref.dtype)


def _moe_call(blk_e, n_used, pad_tok, hp, wg, wu, wd, tb):
    nb = pad_tok.shape[0]
    dh = hp.shape[-1]
    d = 2 * dh
    de = wg.shape[-1]
    smem_blk = lambda f: pl.BlockSpec((1, 1, tb), f, memory_space=pltpu.SMEM)
    wspec = lambda shape: pl.BlockSpec(shape, lambda i, be, nu: (be[i], 0, 0))
    grid_spec = pltpu.PrefetchScalarGridSpec(
        num_scalar_prefetch=2,
        grid=(nb,),
        in_specs=[smem_blk(lambda i, be, nu: (i, 0, 0)),
                  smem_blk(lambda i, be, nu: (jnp.minimum(i + 1, nb - 1), 0, 0)),
                  pl.BlockSpec(memory_space=pl.ANY),
                  wspec((1, d, de)), wspec((1, d, de)), wspec((1, de, d))],
        out_specs=pl.BlockSpec((tb, d), lambda i, be, nu: (i, 0)),
        scratch_shapes=[pltpu.VMEM((2, tb, dh), jnp.uint32), pltpu.SemaphoreType.DMA((2,))],
    )
    return pl.pallas_call(
        _moe_kernel,
        grid_spec=grid_spec,
        out_shape=jax.ShapeDtypeStruct((nb * tb, d), F32),
        compiler_params=_cparams(("arbitrary",)),
        name="moe_experts",
    )(blk_e, n_used, pad_tok, pad_tok, hp, wg, wu, wd)


def _final_kernel(dest_cur_ref, dest_next_ref, x1_ref, route_ref, rows_hbm, gfin_ref, y_ref, buf, sem):
    i = pl.program_id(0)
    nsteps = pl.num_programs(0)
    slot = i % 2
    n_rows = buf.shape[1]
    tm = n_rows // TOP_K

    @pl.when(i == 0)
    def _():
        _row_gather(rows_hbm, dest_cur_ref, buf.at[0], sem.at[0], n_rows)

    @pl.when(i + 1 < nsteps)
    def _():
        _row_gather(rows_hbm, dest_next_ref, buf.at[1 - slot], sem.at[1 - slot], n_rows)

    pltpu.make_async_copy(rows_hbm.at[pl.ds(0, n_rows)], buf.at[slot], sem.at[slot]).wait()
    route = route_ref[...]
    r0 = buf[slot, 0:tm, :] * route[:, 2:3]
    r1 = buf[slot, tm:2 * tm, :] * route[:, 3:4]
    y_ref[...] = _rms(x1_ref[...] + (r0 + r1), gfin_ref[...])


def _final_call(dest, x1, route, rows, gfin, tm):
    t, d = x1.shape
    nt = t // tm
    smem_blk = lambda f: pl.BlockSpec((1, 1, TOP_K * tm), f, memory_space=pltpu.SMEM)
    return pl.pallas_call(
        _final_kernel,
        grid=(nt,),
        in_specs=[smem_blk(lambda i: (i, 0, 0)),
                  smem_blk(lambda i: (jnp.minimum(i + 1, nt - 1), 0, 0)),
                  pl.BlockSpec((tm, d), lambda i: (i, 0)),
                  pl.BlockSpec((tm, LANES), lambda i: (i, 0)),
                  pl.BlockSpec(memory_space=pl.ANY),
                  pl.BlockSpec(gfin.shape, lambda i: (0, 0))],
        out_specs=pl.BlockSpec((tm, d), lambda i: (i, 0)),
        out_shape=jax.ShapeDtypeStruct((t, d), F32),
        scratch_shapes=[pltpu.VMEM((2, TOP_K * tm, d), F32), pltpu.SemaphoreType.DMA((2,))],
        compiler_params=_cparams(("arbitrary",)),
        name="combine_final_norm",
    )(dest, dest, x1, route, rows, gfin)


def _deinterleave(n):
    return np.concatenate([np.arange(0, n, 2), np.arange(1, n, 2)])


def _rope_pad_cols(base):
    pad = -np.ones(LANES // 4, np.int64)
    return np.concatenate([base + np.arange(0, QK_ROPE, 2), pad, base + np.arange(1, QK_ROPE, 2), pad])


def _take_cols(w, cols):
    w_ext = jnp.concatenate([w, jnp.zeros((w.shape[0], 1), w.dtype)], axis=1)
    return w_ext[:, np.where(cols < 0, w.shape[1], cols)]


def _prep_weights(w_in, w_q_up, q_norm, k_norm):
    o_kr = Q_LORA + KV_LORA
    o_gq = o_kr + QK_ROPE
    o_gk = o_gq + GQA_HEADS * HEAD_DIM
    o_gv = o_gk + GQA_KV_HEADS * HEAD_DIM
    cols = [np.arange(0, o_kr), _rope_pad_cols(o_kr)]
    for hd in range(GQA_HEADS):
        cols.append(o_gq + hd * HEAD_DIM + _deinterleave(HEAD_DIM))
    for hd in range(GQA_KV_HEADS):
        cols.append(o_gk + hd * HEAD_DIM + _deinterleave(HEAD_DIM))
    cols.append(np.arange(o_gv, o_gv + GQA_KV_HEADS * HEAD_DIM))
    win = _take_cols(w_in, np.concatenate(cols)).astype(BF16)

    qcols = []
    for hd in range(MLA_HEADS):
        base = hd * (QK_NOPE + QK_ROPE)
        qcols += [base + np.arange(QK_NOPE), _rope_pad_cols(base + QK_NOPE)]
    wq = _take_cols(w_q_up, np.concatenate(qcols)).astype(BF16)
    perm = _deinterleave(HEAD_DIM)
    return win, wq, q_norm[perm][None, :], k_norm[perm][None, :]


def _rope_tables(s):
    rows = s // GRID_W
    row = jnp.repeat(jnp.arange(rows, dtype=F32), GRID_W)
    col = jnp.tile(jnp.arange(GRID_W, dtype=F32), rows)

    def cos_sin(rot_dim):
        n_pairs = rot_dim // 4
        freqs = ROPE_THETA ** (-jnp.arange(n_pairs, dtype=F32) * 2.0 / (rot_dim // 2))
        ang = jnp.concatenate([row[:, None] * freqs[None, :], col[:, None] * freqs[None, :]], axis=-1)
        return jnp.cos(ang), jnp.sin(ang)

    ca, sa = cos_sin(QK_ROPE)
    cb, sb = cos_sin(HEAD_DIM)
    return (jnp.concatenate([ca, ca, ca, ca], -1), jnp.concatenate([-sa, -sa, sa, sa], -1),
            jnp.concatenate([cb, cb], -1), jnp.concatenate([-sb, sb], -1))


def _pick(n, pref):
    return pref if n % pref == 0 else n


def _route_plan(route, tb):
    t = route.shape[0]
    a = t * TOP_K
    expert_id = route[:, 0:TOP_K].astype(jnp.int32).reshape(a)
    onehot = (expert_id[:, None] == jnp.arange(N_EXPERTS, dtype=jnp.int32)[None, :]).astype(jnp.int32)
    csum = jnp.cumsum(onehot, axis=0)
    counts = csum[-1]
    rank = jnp.sum(onehot * csum, axis=1) - 1
    padded = ((counts + tb - 1) // tb) * tb
    pad_ends = jnp.cumsum(padded)
    pad_off = pad_ends - padded
    dest = pad_off[expert_id] + rank
    nb = (a + N_EXPERTS * tb) // tb
    token_id = jnp.repeat(jnp.arange(t, dtype=jnp.int32), TOP_K)
    pad_tok = jnp.zeros((nb * tb,), jnp.int32).at[dest].set(token_id, unique_indices=True)
    blk_start = jnp.arange(nb, dtype=jnp.int32) * tb
    blk_e = jnp.minimum(jnp.sum(blk_start[:, None] >= pad_ends[None, :], axis=1), N_EXPERTS - 1)
    n_used = (pad_ends[-1] // tb).astype(jnp.int32).reshape(1)
    last_e = blk_e[jnp.maximum(n_used[0] - 1, 0)]
    blk_e = jnp.where(jnp.arange(nb) < n_used[0], blk_e, last_e).astype(jnp.int32)
    return pad_tok.reshape(nb, 1, tb), dest.reshape(t, TOP_K), blk_e, n_used


def _trunk(x, p):
    b, s, d = x.shape
    t = b * s
    tm = _pick(s, 256)
    qa, ka, va, qb, kb, vb = _pre_call(x, p["gattn"], p["win"], p["gqa"], p["wq"], p["gkv"], p["wkv"],
                                       p["gqn"], p["gkn"], *p["tables"], tm)
    tk = 2 * tm if s % (4 * tm) == 0 else s // 2
    oa = _flash_call(qa, ka, va, _pick(s, 1024), tk, "mla_flash")
    ob = _flash_call(qb, kb, vb, _pick(s, 512), tk, "gqa_flash")
    x1, hp, route = _post_call(oa.reshape(t, -1), ob.reshape(t, -1), x.reshape(t, d), p["wout"], p["gffn"],
                               p["wr"], p["br"], tm)
    tb = 256
    pad_tok, dest, blk_e, n_used = _route_plan(route, tb)
    rows = _moe_call(blk_e, n_used, pad_tok, hp, p["wg"], p["wu"], p["wd"], tb)
    nt = t // tm
    dest_blk = dest.reshape(nt, tm, TOP_K).transpose(0, 2, 1).reshape(nt, 1, TOP_K * tm)
    y = _final_call(dest_blk, x1, route, rows, p["gfin"], tm)
    return y.reshape(b, s, d)


def kernel(x_prompt, x_sample, attn_norm, w_in, q_a_norm, w_q_up, kv_a_norm, w_kv_up, q_norm, k_norm, w_out,
           ffn_norm, w_group, b_group, w_expert, b_expert, w_gate, w_up, w_down, final_norm):
    assert attn_norm.shape[0] == 1, "single-layer trunk"
    win, wq, gqn, gkn = _prep_weights(w_in[0], w_q_up[0], q_norm[0], k_norm[0])
    d = w_in.shape[1]
    wr = jnp.zeros((d, LANES), F32).at[:, 0:N_GROUPS].set(w_group[0])
    wr = wr.at[:, N_GROUPS:N_GROUPS + N_EXPERTS].set(w_expert[0])
    br = jnp.zeros((1, LANES), F32).at[0, 0:N_GROUPS].set(b_group[0])
    br = br.at[0, N_GROUPS:N_GROUPS + N_EXPERTS].set(b_expert[0])
    wr_hi = wr.astype(BF16)
    wr = jnp.concatenate([wr_hi, (wr - wr_hi.astype(F32)).astype(BF16)], axis=1)
    p = dict(gattn=attn_norm, win=win, gqa=q_a_norm, wq=wq, gkv=kv_a_norm, wkv=w_kv_up[0].astype(BF16),
             gqn=gqn, gkn=gkn, wout=w_out[0].astype(BF16), gffn=ffn_norm, wr=wr, br=br,
             wg=w_gate[0].astype(BF16), wu=w_up[0].astype(BF16), wd=w_down[0].astype(BF16),
             gfin=final_norm[None, :])
    outs = []
    for x in (x_prompt, x_sample):
        p["tables"] = _rope_tables(x.shape[1])
        outs.append(_trunk(x, p))
    return tuple(outs)
```

```python
import functools
import math

import numpy as np
import jax
import jax.numpy as jnp
from jax import lax
from jax.experimental import pallas as pl
from jax.experimental.pallas import tpu as pltpu

F32 = jnp.float32
BF16 = jnp.bfloat16

EPS = 1e-6
ROPE_THETA = 10000.0
GRID_W = 64
LANES = 128
BF16_ROWS = 16
GATHER_UNROLL = 8

MLA_HEADS = 8
Q_LORA = 512
KV_LORA = 256
QK_NOPE = 128
QK_ROPE = 64
V_HEAD = 128
MLA_DK = 256
GQA_HEADS = 8
GQA_KV_HEADS = 2
GQA_REP = GQA_HEADS // GQA_KV_HEADS
HEAD_DIM = 128
N_GROUPS = 4
EXPERTS_PER_GROUP = 8
N_EXPERTS = N_GROUPS * EXPERTS_PER_GROUP
TOP_K = 2
LOG2E = math.log2(math.e)

VMEM_LIMIT = 56 * 1024 * 1024


def _cparams(sem):
    return pltpu.CompilerParams(dimension_semantics=sem, vmem_limit_bytes=VMEM_LIMIT)


def _rms(x, gain):
    return x * lax.rsqrt(jnp.mean(x * x, axis=-1, keepdims=True) + EPS) * gain


def _rope(x, cos_t, sin_t):
    return x * cos_t + pltpu.roll(x, LANES // 2, axis=1) * sin_t


def _pre_kernel(x_ref, gattn_ref, win_ref, gqa_ref, wq_ref, gkv_ref, wkv_ref, gqn_ref, gkn_ref,
                ca_ref, sa_ref, cb_ref, sb_ref,
                qa_ref, ka_ref, va_ref, qb_ref, kb_ref, vb_ref):
    x = x_ref[0]
    h = _rms(x, gattn_ref[...]).astype(BF16)
    proj = jnp.dot(h, win_ref[...], preferred_element_type=F32)
    o = 0
    q_lat = proj[:, o:o + Q_LORA]; o += Q_LORA
    kv_lat = proj[:, o:o + KV_LORA]; o += KV_LORA
    k_rope = proj[:, o:o + LANES]; o += LANES
    gq = proj[:, o:o + GQA_HEADS * HEAD_DIM]; o += GQA_HEADS * HEAD_DIM
    gk = proj[:, o:o + GQA_KV_HEADS * HEAD_DIM]; o += GQA_KV_HEADS * HEAD_DIM
    gv = proj[:, o:o + GQA_KV_HEADS * HEAD_DIM]

    ca, sa, cb, sb = ca_ref[...], sa_ref[...], cb_ref[...], sb_ref[...]

    q = jnp.dot(_rms(q_lat, gqa_ref[...]).astype(BF16), wq_ref[...], preferred_element_type=F32)
    kv = jnp.dot(_rms(kv_lat, gkv_ref[...]).astype(BF16), wkv_ref[...], preferred_element_type=F32)
    k_pe = _rope(k_rope, ca, sa).astype(BF16)
    sc_a = np.float32((QK_NOPE + QK_ROPE) ** -0.5 * LOG2E)
    for hd in range(MLA_HEADS):
        qh = q[:, hd * MLA_DK:(hd + 1) * MLA_DK]
        qa_ref[0, hd, 0, 0:LANES, :] = (qh[:, 0:LANES] * sc_a).T.astype(BF16)
        qa_ref[0, hd, 0, LANES:MLA_DK, :] = (_rope(qh[:, LANES:MLA_DK], ca, sa) * sc_a).T.astype(BF16)
        kvh = kv[:, hd * 2 * LANES:(hd + 1) * 2 * LANES]
        ka_ref[0, hd, :, 0:LANES] = kvh[:, 0:LANES].astype(BF16)
        ka_ref[0, hd, :, LANES:MLA_DK] = k_pe
        va_ref[0, hd, 0] = kvh[:, LANES:2 * LANES].T.astype(BF16)

    sc_b = np.float32(HEAD_DIM ** -0.5 * LOG2E)
    gqn, gkn = gqn_ref[...], gkn_ref[...]
    for hd in range(GQA_HEADS):
        qh = _rope(_rms(gq[:, hd * HEAD_DIM:(hd + 1) * HEAD_DIM], gqn), cb, sb)
        qb_ref[0, hd, 0] = (qh * sc_b).T.astype(BF16)
    for hd in range(GQA_KV_HEADS):
        kh = _rope(_rms(gk[:, hd * HEAD_DIM:(hd + 1) * HEAD_DIM], gkn), cb, sb)
        kb_ref[0, hd] = kh.astype(BF16)
        vb_ref[0, hd, 0] = gv[:, hd * HEAD_DIM:(hd + 1) * HEAD_DIM].T.astype(BF16)


def _pre_call(x, gattn, win, gqa, wq, gkv, wkv, gqn, gkn, ca, sa, cb, sb, tm):
    b, s, d = x.shape
    const = lambda shape: pl.BlockSpec(shape, lambda bi, si: (0,) * len(shape))
    tab = pl.BlockSpec((tm, LANES), lambda bi, si: (si, 0))
    head = lambda nh, w: pl.BlockSpec((1, nh, tm, w), lambda bi, si: (bi, 0, si, 0))
    head_t = lambda nh, w: pl.BlockSpec((1, nh, 1, w, tm), lambda bi, si: (bi, 0, si, 0, 0))
    return pl.pallas_call(
        _pre_kernel,
        grid=(b, s // tm),
        in_specs=[pl.BlockSpec((1, tm, d), lambda bi, si: (bi, si, 0)),
                  const(gattn.shape), const(win.shape), const(gqa.shape), const(wq.shape),
                  const(gkv.shape), const(wkv.shape), const(gqn.shape), const(gkn.shape),
                  tab, tab, tab, tab],
        out_specs=[head_t(MLA_HEADS, MLA_DK), head(MLA_HEADS, MLA_DK), head_t(MLA_HEADS, V_HEAD),
                   head_t(GQA_HEADS, HEAD_DIM), head(GQA_KV_HEADS, HEAD_DIM), head_t(GQA_KV_HEADS, HEAD_DIM)],
        out_shape=[jax.ShapeDtypeStruct((b, MLA_HEADS, s // tm, MLA_DK, tm), BF16),
                   jax.ShapeDtypeStruct((b, MLA_HEADS, s, MLA_DK), BF16),
                   jax.ShapeDtypeStruct((b, MLA_HEADS, s // tm, V_HEAD, tm), BF16),
                   jax.ShapeDtypeStruct((b, GQA_HEADS, s // tm, HEAD_DIM, tm), BF16),
                   jax.ShapeDtypeStruct((b, GQA_KV_HEADS, s, HEAD_DIM), BF16),
                   jax.ShapeDtypeStruct((b, GQA_KV_HEADS, s // tm, HEAD_DIM, tm), BF16)],
        compiler_params=_cparams(("parallel", "parallel")),
        name="pre_proj",
    )(x, gattn, win, gqa, wq, gkv, wkv, gqn, gkn, ca, sa, cb, sb)


def _flash_kernel(q_ref, k_ref, vt_ref, o_ref, acc_sc, s_sc, *, rep, tk):
    s_len = k_ref.shape[2]
    dv, cw = vt_ref.shape[3], vt_ref.shape[4]
    n_qc, qw = q_ref.shape[2], q_ref.shape[4]
    tq = n_qc * qw
    n_sub = tk // cw
    n_steps = s_len // tk
    unroll = 8 if n_steps % 8 == 0 else 2
    ones = jnp.ones((BF16_ROWS, cw), BF16)

    for r in range(rep):
        def scores(j, slot, r=r):
            k = k_ref[0, 0, pl.ds(pl.multiple_of(j * tk, tk), tk), :]
            st = jnp.concatenate([jnp.dot(k, q_ref[0, r, c], preferred_element_type=F32)
                                  for c in range(n_qc)], axis=1)
            s_sc[slot] = st
            return jnp.max(st, axis=0, keepdims=True)

        acc_sc[...] = jnp.zeros(acc_sc.shape, F32)
        mx0 = scores(0, 0)

        def body(jj, carry):
            m_prev, mx = carry
            for si in range(unroll):
                j = jj * unroll + si
                mx_next = scores(jnp.minimum(j + 1, n_steps - 1), 1 - si % 2)
                m_new = jnp.maximum(m_prev, mx)
                alpha = jnp.exp2(m_prev - m_new)
                pb = jnp.exp2(s_sc[si % 2] - m_new).astype(BF16)
                pv = None
                for c in range(n_sub):
                    vt1 = jnp.concatenate([vt_ref[0, 0, j * n_sub + c], ones], axis=0)
                    d = jnp.dot(vt1, pb[c * cw:(c + 1) * cw], preferred_element_type=F32)
                    pv = d if pv is None else pv + d
                acc_sc[...] = alpha * acc_sc[...] + pv
                m_prev, mx = m_new, mx_next
            return m_prev, mx

        lax.fori_loop(0, n_steps // unroll, body, (jnp.full((1, tq), -jnp.inf, F32), mx0))
        out_t = acc_sc[0:dv, :] / acc_sc[dv:dv + 1, :]
        o_ref[0, :, r * dv:(r + 1) * dv] = out_t.T.astype(o_ref.dtype)


def _flash_call(qt, k, vt, tq, tk, name):
    b, hq, _, dk, qw = qt.shape
    s = k.shape[2]
    hkv = k.shape[1]
    _, _, nchunk, dv, cw = vt.shape
    rep = hq // hkv
    assert tk % cw == 0 and (s // tk) % 2 == 0 and s % tq == 0 and tq % qw == 0
    return pl.pallas_call(
        functools.partial(_flash_kernel, rep=rep, tk=tk),
        grid=(b, hkv, s // tq),
        in_specs=[pl.BlockSpec((1, rep, tq // qw, dk, qw), lambda bi, g, qi: (bi, g, qi, 0, 0)),
                  pl.BlockSpec((1, 1, s, dk), lambda bi, g, qi: (bi, g, 0, 0)),
                  pl.BlockSpec((1, 1, nchunk, dv, cw), lambda bi, g, qi: (bi, g, 0, 0, 0))],
        out_specs=pl.BlockSpec((1, tq, rep * dv), lambda bi, g, qi: (bi, qi, g)),
        out_shape=jax.ShapeDtypeStruct((b, s, hq * dv), BF16),
        scratch_shapes=[pltpu.VMEM((dv + BF16_ROWS, tq), F32), pltpu.VMEM((2, tk, tq), F32)],
        compiler_params=_cparams(("parallel", "parallel", "arbitrary")),
        name=name,
    )(qt, k, vt)


def _post_kernel(oa_ref, ob_ref, x_ref, wout_ref, gffn_ref, wr_ref, br_ref, x1_ref, hp_ref, route_ref):
    half = oa_ref.shape[-1]
    mix = jnp.dot(oa_ref[...], wout_ref[0:half, :], preferred_element_type=F32)
    mix = mix + jnp.dot(ob_ref[...], wout_ref[half:2 * half, :], preferred_element_type=F32)
    x1 = x_ref[...] + mix
    x1_ref[...] = x1
    h = _rms(x1, gffn_ref[...])

    dh = h.shape[-1] // 2
    h_hi = h.astype(BF16)
    h_hi32 = h_hi.astype(F32)
    lo = pltpu.bitcast(h_hi32[:, 0:dh], jnp.uint32)
    hi = pltpu.bitcast(h_hi32[:, dh:2 * dh], jnp.uint32)
    hp_ref[...] = hi | (lo >> 16)

    h_lo = (h - h_hi32).astype(BF16)
    lg2 = (jnp.dot(h_hi, wr_ref[...], preferred_element_type=F32)
           + jnp.dot(h_lo, wr_ref[...], preferred_element_type=F32))
    lg = lg2[:, 0:LANES] + lg2[:, LANES:2 * LANES] + br_ref[...]
    lane = lax.broadcasted_iota(jnp.int32, lg.shape, 1)
    neg = jnp.float32(-jnp.inf)
    big = jnp.int32(4 * LANES)
    gl = jnp.where(lane < N_GROUPS, lg, neg)
    gmax = jnp.max(gl, axis=-1, keepdims=True)
    gsum = jnp.sum(jnp.exp(gl - gmax), axis=-1, keepdims=True)
    g_w = 1.0 / gsum
    g_idx = jnp.min(jnp.where(gl == gmax, lane, big), axis=-1, keepdims=True)
    lo_lane = N_GROUPS + EXPERTS_PER_GROUP * g_idx
    el = jnp.where((lane >= lo_lane) & (lane < lo_lane + EXPERTS_PER_GROUP), lg, neg)
    m1 = jnp.max(el, axis=-1, keepdims=True)
    i1 = jnp.min(jnp.where(el == m1, lane, big), axis=-1, keepdims=True)
    el2 = jnp.where(lane == i1, neg, el)
    m2 = jnp.max(el2, axis=-1, keepdims=True)
    i2 = jnp.min(jnp.where(el2 == m2, lane, big), axis=-1, keepdims=True)
    e2 = jnp.exp(m2 - m1)
    w1 = 1.0 / (1.0 + e2)
    w2 = e2 / (1.0 + e2)
    out = jnp.where(lane == 0, (i1 - N_GROUPS).astype(F32),
          jnp.where(lane == 1, (i2 - N_GROUPS).astype(F32),
          jnp.where(lane == 2, g_w * w1,
          jnp.where(lane == 3, g_w * w2, 0.0))))
    route_ref[...] = out


def _post_call(oa, ob, x, wout, gffn, wr, br, tm):
    t, d = x.shape
    half = oa.shape[-1]
    const = lambda shape: pl.BlockSpec(shape, lambda i: (0,) * len(shape))
    row = lambda w: pl.BlockSpec((tm, w), lambda i: (i, 0))
    return pl.pallas_call(
        _post_kernel,
        grid=(t // tm,),
        in_specs=[row(half), row(half), row(d), const(wout.shape), const(gffn.shape),
                  const(wr.shape), const(br.shape)],
        out_specs=[row(d), row(d // 2), row(LANES)],
        out_shape=[jax.ShapeDtypeStruct((t, d), F32),
                   jax.ShapeDtypeStruct((t, d // 2), jnp.uint32),
                   jax.ShapeDtypeStruct((t, LANES), F32)],
        compiler_params=_cparams(("parallel",)),
        name="post_proj_router",
    )(oa, ob, x, wout, gffn, wr, br)


def _row_gather(src_hbm, idx_ref, buf, sem, n_rows):
    def body(r, carry):
        pltpu.make_async_copy(src_hbm.at[pl.ds(idx_ref[0, 0, r], 1)], buf.at[pl.ds(r, 1)], sem).start()
        return carry
    lax.fori_loop(0, n_rows, body, 0, unroll=GATHER_UNROLL)


def _moe_kernel(blk_e_ref, n_used_ref, tok_cur_ref, tok_next_ref, hp_hbm, wg_ref, wu_ref, wd_ref,
                out_ref, buf, sem):
    i = pl.program_id(0)
    slot = i % 2
    tb = buf.shape[1]
    n_used = n_used_ref[0]

    @pl.when(i == 0)
    def _():
        _row_gather(hp_hbm, tok_cur_ref, buf.at[0], sem.at[0], tb)

    @pl.when(i + 1 < n_used)
    def _():
        _row_gather(hp_hbm, tok_next_ref, buf.at[1 - slot], sem.at[1 - slot], tb)

    @pl.when(i < n_used)
    def _():
        pltpu.make_async_copy(hp_hbm.at[pl.ds(0, tb)], buf.at[slot], sem.at[slot]).wait()
        w = buf[slot]
        dh = w.shape[-1]
        x_lo = pltpu.bitcast(w << 16, F32).astype(BF16)
        x_hi = pltpu.bitcast(w & jnp.uint32(0xFFFF0000), F32).astype(BF16)
        g = jnp.dot(x_lo, wg_ref[0, 0:dh, :], preferred_element_type=F32)
        g = g + jnp.dot(x_hi, wg_ref[0, dh:2 * dh, :], preferred_element_type=F32)
        u = jnp.dot(x_lo, wu_ref[0, 0:dh, :], preferred_element_type=F32)
        u = u + jnp.dot(x_hi, wu_ref[0, dh:2 * dh, :], preferred_element_type=F32)
        hid = (g * jax.nn.sigmoid(g)) * u
        out_ref[...] = jnp.dot(hid.astype(BF16), wd_ref[0], preferred_element_type=F32)

    @pl.when(i >= n_used)
    def _():
        out_ref[...] = jnp.zeros(out_ref.shape, out_ref.dtype)


def _moe_call(blk_e, n_used, pad_tok, hp, wg, wu, wd, tb):
    nb = pad_tok.shape[0]
    dh = hp.shape[-1]
    d = 2 * dh
    de = wg.shape[-1]
    smem_blk = lambda f: pl.BlockSpec((1, 1, tb), f, memory_space=pltpu.SMEM)
    wspec = lambda shape: pl.BlockSpec(shape, lambda i, be, nu: (be[i], 0, 0))
    grid_spec = pltpu.PrefetchScalarGridSpec(
        num_scalar_prefetch=2,
        grid=(nb,),
        in_specs=[smem_blk(lambda i, be, nu: (i, 0, 0)),
                  smem_blk(lambda i, be, nu: (jnp.minimum(i + 1, nb - 1), 0, 0)),
                  pl.BlockSpec(memory_space=pl.ANY),
                  wspec((1, d, de)), wspec((1, d, de)), wspec((1, de, d))],
        out_specs=pl.BlockSpec((tb, d), lambda i, be, nu: (i, 0)),
        scratch_shapes=[pltpu.VMEM((2, tb, dh), jnp.uint32), pltpu.SemaphoreType.DMA((2,))],
    )
    return pl.pallas_call(
        _moe_kernel,
        grid_spec=grid_spec,
        out_shape=jax.ShapeDtypeStruct((nb * tb, d), F32),
        compiler_params=_cparams(("arbitrary",)),
        name="moe_experts",
    )(blk_e, n_used, pad_tok, pad_tok, hp, wg, wu, wd)


def _final_kernel(dest_cur_ref, dest_next_ref, x1_ref, route_ref, rows_hbm, gfin_ref, y_ref, buf, sem):
    i = pl.program_id(0)
    nsteps = pl.num_programs(0)
    slot = i % 2
    n_rows = buf.shape[1]
    tm = n_rows // TOP_K

    @pl.when(i == 0)
    def _():
        _row_gather(rows_hbm, dest_cur_ref, buf.at[0], sem.at[0], n_rows)

    @pl.when(i + 1 < nsteps)
    def _():
        _row_gather(rows_hbm, dest_next_ref, buf.at[1 - slot], sem.at[1 - slot], n_rows)

    pltpu.make_async_copy(rows_hbm.at[pl.ds(0, n_rows)], buf.at[slot], sem.at[slot]).wait()
    route = route_ref[...]
    r0 = buf[slot, 0:tm, :] * route[:, 2:3]
    r1 = buf[slot, tm:2 * tm, :] * route[:, 3:4]
    y_ref[...] = _rms(x1_ref[...] + (r0 + r1), gfin_ref[...])


def _final_call(dest, x1, route, rows, gfin, tm):
    t, d = x1.shape
    nt = t // tm
    smem_blk = lambda f: pl.BlockSpec((1, 1, TOP_K * tm), f, memory_space=pltpu.SMEM)
    return pl.pallas_call(
        _final_kernel,
        grid=(nt,),
        in_specs=[smem_blk(lambda i: (i, 0, 0)),
                  smem_blk(lambda i: (jnp.minimum(i + 1, nt - 1), 0, 0)),
                  pl.BlockSpec((tm, d), lambda i: (i, 0)),
                  pl.BlockSpec((tm, LANES), lambda i: (i, 0)),
                  pl.BlockSpec(memory_space=pl.ANY),
                  pl.BlockSpec(gfin.shape, lambda i: (0, 0))],
        out_specs=pl.BlockSpec((tm, d), lambda i: (i, 0)),
        out_shape=jax.ShapeDtypeStruct((t, d), F32),
        scratch_shapes=[pltpu.VMEM((2, TOP_K * tm, d), F32), pltpu.SemaphoreType.DMA((2,))],
        compiler_params=_cparams(("arbitrary",)),
        name="combine_final_norm",
    )(dest, dest, x1, route, rows, gfin)


def _deinterleave(n):
    return np.concatenate([np.arange(0, n, 2), np.arange(1, n, 2)])


def _rope_pad_cols(base):
    pad = -np.ones(LANES // 4, np.int64)
    return np.concatenate([base + np.arange(0, QK_ROPE, 2), pad, base + np.arange(1, QK_ROPE, 2), pad])


def _take_cols(w, cols):
    w_ext = jnp.concatenate([w, jnp.zeros((w.shape[0], 1), w.dtype)], axis=1)
    return w_ext[:, np.where(cols < 0, w.shape[1], cols)]


def _prep_weights(w_in, w_q_up, q_norm, k_norm):
    o_kr = Q_LORA + KV_LORA
    o_gq = o_kr + QK_ROPE
    o_gk = o_gq + GQA_HEADS * HEAD_DIM
    o_gv = o_gk + GQA_KV_HEADS * HEAD_DIM
    cols = [np.arange(0, o_kr), _rope_pad_cols(o_kr)]
    for hd in range(GQA_HEADS):
        cols.append(o_gq + hd * HEAD_DIM + _deinterleave(HEAD_DIM))
    for hd in range(GQA_KV_HEADS):
        cols.append(o_gk + hd * HEAD_DIM + _deinterleave(HEAD_DIM))
    cols.append(np.arange(o_gv, o_gv + GQA_KV_HEADS * HEAD_DIM))
    win = _take_cols(w_in, np.concatenate(cols)).astype(BF16)

    qcols = []
    for hd in range(MLA_HEADS):
        base = hd * (QK_NOPE + QK_ROPE)
        qcols += [base + np.arange(QK_NOPE), _rope_pad_cols(base + QK_NOPE)]
    wq = _take_cols(w_q_up, np.concatenate(qcols)).astype(BF16)
    perm = _deinterleave(HEAD_DIM)
    return win, wq, q_norm[perm][None, :], k_norm[perm][None, :]


def _rope_tables(s):
    rows = s // GRID_W
    row = jnp.repeat(jnp.arange(rows, dtype=F32), GRID_W)
    col = jnp.tile(jnp.arange(GRID_W, dtype=F32), rows)

    def cos_sin(rot_dim):
        n_pairs = rot_dim // 4
        freqs = ROPE_THETA ** (-jnp.arange(n_pairs, dtype=F32) * 2.0 / (rot_dim // 2))
        ang = jnp.concatenate([row[:, None] * freqs[None, :], col[:, None] * freqs[None, :]], axis=-1)
        return jnp.cos(ang), jnp.sin(ang)

    ca, sa = cos_sin(QK_ROPE)
    cb, sb = cos_sin(HEAD_DIM)
    return (jnp.concatenate([ca, ca, ca, ca], -1), jnp.concatenate([-sa, -sa, sa, sa], -1),
            jnp.concatenate([cb, cb], -1), jnp.concatenate([-sb, sb], -1))


def _pick(n, pref):
    return pref if n % pref == 0 else n


def _route_plan(route, tb):
    t = route.shape[0]
    a = t * TOP_K
    expert_id = route[:, 0:TOP_K].astype(jnp.int32).reshape(a)
    onehot = (expert_id[:, None] == jnp.arange(N_EXPERTS, dtype=jnp.int32)[None, :]).astype(jnp.int32)
    csum = jnp.cumsum(onehot, axis=0)
    counts = csum[-1]
    rank = jnp.sum(onehot * csum, axis=1) - 1
    padded = ((counts + tb - 1) // tb) * tb
    pad_ends = jnp.cumsum(padded)
    pad_off = pad_ends - padded
    dest = pad_off[expert_id] + rank
    nb = (a + N_EXPERTS * tb) // tb
    blk_start = jnp.arange(nb, dtype=jnp.int32) * tb
    blk_e = jnp.minimum(jnp.sum(blk_start[:, None] >= pad_ends[None, :], axis=1), N_EXPERTS - 1)
    n_used = (pad_ends[-1] // tb).astype(jnp.int32).reshape(1)
    order = jnp.sort(expert_id * a + jnp.arange(a, dtype=jnp.int32)) % a
    row_e = jnp.repeat(blk_e, tb)
    row_rank = jnp.arange(nb * tb, dtype=jnp.int32) - pad_off[row_e]
    offsets = jnp.cumsum(counts) - counts
    src = jnp.clip(offsets[row_e] + row_rank, 0, a - 1)
    valid = (row_rank < counts[row_e]) & (jnp.arange(nb * tb, dtype=jnp.int32) < pad_ends[-1])
    pad_tok = jnp.where(valid, order[src] // TOP_K, 0)
    last_e = blk_e[jnp.maximum(n_used[0] - 1, 0)]
    blk_e = jnp.where(jnp.arange(nb) < n_used[0], blk_e, last_e).astype(jnp.int32)
    return pad_tok.reshape(nb, 1, tb), dest.reshape(t, TOP_K), blk_e, n_used


def _trunk(x, p):
    b, s, d = x.shape
    t = b * s
    tm = _pick(s, 256)
    qa, ka, va, qb, kb, vb = _pre_call(x, p["gattn"], p["win"], p["gqa"], p["wq"], p["gkv"], p["wkv"],
                                       p["gqn"], p["gkn"], *p["tables"], tm)
    tk = 2 * tm if s % (4 * tm) == 0 else s // 2
    oa = _flash_call(qa, ka, va, _pick(s, 1024), tk, "mla_flash")
    ob = _flash_call(qb, kb, vb, _pick(s, 512), tk, "gqa_flash")
    x1, hp, route = _post_call(oa.reshape(t, -1), ob.reshape(t, -1), x.reshape(t, d), p["wout"], p["gffn"],
                               p["wr"], p["br"], tm)
    tb = 256
    pad_tok, dest, blk_e, n_used = _route_plan(route, tb)
    rows = _moe_call(blk_e, n_used, pad_tok, hp, p["wg"], p["wu"], p["wd"], tb)
    nt = t // tm
    dest_blk = dest.reshape(nt, tm, TOP_K).transpose(0, 2, 1).reshape(nt, 1, TOP_K * tm)
    y = _final_call(dest_blk, x1, route, rows, p["gfin"], tm)
    return y.reshape(b, s, d)


def kernel(x_prompt, x_sample, attn_norm, w_in, q_a_norm, w_q_up, kv_a_norm, w_kv_up, q_norm, k_norm, w_out,
           ffn_norm, w_group, b_group, w_expert, b_expert, w_gate, w_up, w_down, final_norm):
    assert attn_norm.shape[0] == 1, "single-layer trunk"
    win, wq, gqn, gkn = _prep_weights(w_in[0], w_q_up[0], q_norm[0], k_norm[0])
    d = w_in.shape[1]
    wr = jnp.zeros((d, LANES), F32).at[:, 0:N_GROUPS].set(w_group[0])
    wr = wr.at[:, N_GROUPS:N_GROUPS + N_EXPERTS].set(w_expert[0])
    br = jnp.zeros((1, LANES), F32).at[0, 0:N_GROUPS].set(b_group[0])
    br = br.at[0, N_GROUPS:N_GROUPS + N_EXPERTS].set(b_expert[0])
    wr_hi = wr.astype(BF16)
    wr = jnp.concatenate([wr_hi, (wr - wr_hi.astype(F32)).astype(BF16)], axis=1)
    p = dict(gattn=attn_norm, win=win, gqa=q_a_norm, wq=wq, gkv=kv_a_norm, wkv=w_kv_up[0].astype(BF16),
             gqn=gqn, gkn=gkn, wout=w_out[0].astype(BF16), gffn=ffn_norm, wr=wr, br=br,
             wg=w_gate[0].astype(BF16), wu=w_up[0].astype(BF16), wd=w_down[0].astype(BF16),
             gfin=final_norm[None, :])
    outs = []
    for x in (x_prompt, x_sample):
        p["tables"] = _rope_tables(x.shape[1])
        outs.append(_trunk(x, p))
    return tuple(outs)
```

```python
import functools
import math

import numpy as np
import jax
import jax.numpy as jnp
from jax import lax
from jax.experimental import pallas as pl
from jax.experimental.pallas import tpu as pltpu

F32 = jnp.float32
BF16 = jnp.bfloat16

EPS = 1e-6
ROPE_THETA = 10000.0
GRID_W = 64
LANES = 128
BF16_ROWS = 16
GATHER_UNROLL = 8

MLA_HEADS = 8
Q_LORA = 512
KV_LORA = 256
QK_NOPE = 128
QK_ROPE = 64
V_HEAD = 128
MLA_DK = 256
GQA_HEADS = 8
GQA_KV_HEADS = 2
GQA_REP = GQA_HEADS // GQA_KV_HEADS
HEAD_DIM = 128
N_GROUPS = 4
EXPERTS_PER_GROUP = 8
N_EXPERTS = N_GROUPS * EXPERTS_PER_GROUP
TOP_K = 2
LOG2E = math.log2(math.e)

VMEM_LIMIT = 56 * 1024 * 1024


def _cparams(sem):
    return pltpu.CompilerParams(dimension_semantics=sem, vmem_limit_bytes=VMEM_LIMIT)


def _rms(x, gain):
    return x * lax.rsqrt(jnp.mean(x * x, axis=-1, keepdims=True) + EPS) * gain


def _rope(x, cos_t, sin_t):
    return x * cos_t + pltpu.roll(x, LANES // 2, axis=1) * sin_t


def _pre_kernel(x_ref, gattn_ref, win_ref, gqa_ref, wq_ref, gkv_ref, wkv_ref, gqn_ref, gkn_ref,
                ca_ref, sa_ref, cb_ref, sb_ref,
                qa_ref, ka_ref, va_ref, qb_ref, kb_ref, vb_ref):
    x = x_ref[0]
    h = _rms(x, gattn_ref[...]).astype(BF16)
    proj = jnp.dot(h, win_ref[...], preferred_element_type=F32)
    o = 0
    q_lat = proj[:, o:o + Q_LORA]; o += Q_LORA
    kv_lat = proj[:, o:o + KV_LORA]; o += KV_LORA
    k_rope = proj[:, o:o + LANES]; o += LANES
    gq = proj[:, o:o + GQA_HEADS * HEAD_DIM]; o += GQA_HEADS * HEAD_DIM
    gk = proj[:, o:o + GQA_KV_HEADS * HEAD_DIM]; o += GQA_KV_HEADS * HEAD_DIM
    gv = proj[:, o:o + GQA_KV_HEADS * HEAD_DIM]

    ca, sa, cb, sb = ca_ref[...], sa_ref[...], cb_ref[...], sb_ref[...]

    q = jnp.dot(_rms(q_lat, gqa_ref[...]).astype(BF16), wq_ref[...], preferred_element_type=F32)
    kv = jnp.dot(_rms(kv_lat, gkv_ref[...]).astype(BF16), wkv_ref[...], preferred_element_type=F32)
    k_pe = _rope(k_rope, ca, sa).astype(BF16)
    sc_a = np.float32((QK_NOPE + QK_ROPE) ** -0.5 * LOG2E)
    for hd in range(MLA_HEADS):
        qh = q[:, hd * MLA_DK:(hd + 1) * MLA_DK]
        qa_ref[0, hd, 0, 0:LANES, :] = (qh[:, 0:LANES] * sc_a).T.astype(BF16)
        qa_ref[0, hd, 0, LANES:MLA_DK, :] = (_rope(qh[:, LANES:MLA_DK], ca, sa) * sc_a).T.astype(BF16)
        kvh = kv[:, hd * 2 * LANES:(hd + 1) * 2 * LANES]
        ka_ref[0, hd, :, 0:LANES] = kvh[:, 0:LANES].astype(BF16)
        ka_ref[0, hd, :, LANES:MLA_DK] = k_pe
        va_ref[0, hd, 0] = kvh[:, LANES:2 * LANES].T.astype(BF16)

    sc_b = np.float32(HEAD_DIM ** -0.5 * LOG2E)
    gqn, gkn = gqn_ref[...], gkn_ref[...]
    for hd in range(GQA_HEADS):
        qh = _rope(_rms(gq[:, hd * HEAD_DIM:(hd + 1) * HEAD_DIM], gqn), cb, sb)
        qb_ref[0, hd, 0] = (qh * sc_b).T.astype(BF16)
    for hd in range(GQA_KV_HEADS):
        kh = _rope(_rms(gk[:, hd * HEAD_DIM:(hd + 1) * HEAD_DIM], gkn), cb, sb)
        kb_ref[0, hd] = kh.astype(BF16)
        vb_ref[0, hd, 0] = gv[:, hd * HEAD_DIM:(hd + 1) * HEAD_DIM].T.astype(BF16)


def _pre_call(x, gattn, win, gqa, wq, gkv, wkv, gqn, gkn, ca, sa, cb, sb, tm):
    b, s, d = x.shape
    const = lambda shape: pl.BlockSpec(shape, lambda bi, si: (0,) * len(shape))
    tab = pl.BlockSpec((tm, LANES), lambda bi, si: (si, 0))
    head = lambda nh, w: pl.BlockSpec((1, nh, tm, w), lambda bi, si: (bi, 0, si, 0))
    head_t = lambda nh, w: pl.BlockSpec((1, nh, 1, w, tm), lambda bi, si: (bi, 0, si, 0, 0))
    return pl.pallas_call(
        _pre_kernel,
        grid=(b, s // tm),
        in_specs=[pl.BlockSpec((1, tm, d), lambda bi, si: (bi, si, 0)),
                  const(gattn.shape), const(win.shape), const(gqa.shape), const(wq.shape),
                  const(gkv.shape), const(wkv.shape), const(gqn.shape), const(gkn.shape),
                  tab, tab, tab, tab],
        out_specs=[head_t(MLA_HEADS, MLA_DK), head(MLA_HEADS, MLA_DK), head_t(MLA_HEADS, V_HEAD),
                   head_t(GQA_HEADS, HEAD_DIM), head(GQA_KV_HEADS, HEAD_DIM), head_t(GQA_KV_HEADS, HEAD_DIM)],
        out_shape=[jax.ShapeDtypeStruct((b, MLA_HEADS, s // tm, MLA_DK, tm), BF16),
                   jax.ShapeDtypeStruct((b, MLA_HEADS, s, MLA_DK), BF16),
                   jax.ShapeDtypeStruct((b, MLA_HEADS, s // tm, V_HEAD, tm), BF16),
                   jax.ShapeDtypeStruct((b, GQA_HEADS, s // tm, HEAD_DIM, tm), BF16),
                   jax.ShapeDtypeStruct((b, GQA_KV_HEADS, s, HEAD_DIM), BF16),
                   jax.ShapeDtypeStruct((b, GQA_KV_HEADS, s // tm, HEAD_DIM, tm), BF16)],
        compiler_params=_cparams(("parallel", "parallel")),
        name="pre_proj",
    )(x, gattn, win, gqa, wq, gkv, wkv, gqn, gkn, ca, sa, cb, sb)


def _flash_kernel(q_ref, k_ref, vt_ref, o_ref, acc_sc, s_sc, *, streams, tq, tk):
    s_len = k_ref.shape[2]
    dv, cw = vt_ref.shape[3], vt_ref.shape[4]
    qw = q_ref.shape[4]
    n_qc = tq // qw
    n_sub = tk // cw
    n_steps = s_len // tk
    unroll = 8 if n_steps % 8 == 0 else 2
    n_trips = n_steps // unroll
    ones = jnp.ones((BF16_ROWS, cw), BF16)

    def scores(r, c0, j, slot):
        start = j * tk if isinstance(j, int) else pl.multiple_of(j * tk, tk)
        k = k_ref[0, 0, pl.ds(start, tk), :]
        st = jnp.concatenate([jnp.dot(k, q_ref[0, r, c0 + c], preferred_element_type=F32)
                              for c in range(n_qc)], axis=1)
        s_sc[slot] = st
        return jnp.max(st, axis=0, keepdims=True)

    def key_step(j, slot, m_prev, mx, next_scores):
        mx_next = next_scores() if next_scores is not None else mx
        m_new = jnp.maximum(m_prev, mx)
        alpha = jnp.exp2(m_prev - m_new)
        pb = jnp.exp2(s_sc[slot] - m_new).astype(BF16)
        pv = None
        for c in range(n_sub):
            vt1 = jnp.concatenate([vt_ref[0, 0, j * n_sub + c], ones], axis=0)
            d = jnp.dot(vt1, pb[c * cw:(c + 1) * cw], preferred_element_type=F32)
            pv = d if pv is None else pv + d
        acc_sc[...] = alpha * acc_sc[...] + pv
        return m_new, mx_next

    mx = scores(streams[0][0], streams[0][1], 0, 0)
    for si, stream in enumerate(streams):
        r, c0, row, col = stream
        r_nx, c0_nx, j_nx = (streams[si + 1][0], streams[si + 1][1], 0) if si + 1 < len(streams) \
            else (r, c0, n_steps - 1)
        acc_sc[...] = jnp.zeros(acc_sc.shape, F32)

        def body(jj, carry, r=r, c0=c0, r_nx=r_nx, c0_nx=c0_nx, j_nx=j_nx):
            m_prev, mx = carry
            for u in range(unroll):
                j = jj * unroll + u
                if u + 1 < unroll:
                    nxt = functools.partial(scores, r, c0, j + 1, 1 - u % 2)
                else:
                    last = jj == n_trips - 1
                    nxt = functools.partial(scores, jnp.where(last, r_nx, r), jnp.where(last, c0_nx, c0),
                                            jnp.where(last, j_nx, j + 1), 1 - u % 2)
                m_prev, mx = key_step(j, u % 2, m_prev, mx, nxt)
            return m_prev, mx

        _, mx = lax.fori_loop(0, n_trips, body, (jnp.full((1, tq), -jnp.inf, F32), mx))
        out_t = acc_sc[0:dv, :] / acc_sc[dv:dv + 1, :]
        o_ref[0, row:row + tq, col:col + dv] = out_t.T.astype(o_ref.dtype)


def _flash_call(qt, k, vt, tq, n_tiles, tk, name):
    b, hq, _, dk, qw = qt.shape
    s = k.shape[2]
    hkv = k.shape[1]
    _, _, nchunk, dv, cw = vt.shape
    rep = hq // hkv
    tq_blk = tq * n_tiles
    assert tk % cw == 0 and (s // tk) % 2 == 0 and s % tq_blk == 0 and tq % qw == 0
    streams = tuple((r, ti * (tq // qw), ti * tq, r * dv) for r in range(rep) for ti in range(n_tiles))
    return pl.pallas_call(
        functools.partial(_flash_kernel, streams=streams, tq=tq, tk=tk),
        grid=(b, hkv, s // tq_blk),
        in_specs=[pl.BlockSpec((1, rep, tq_blk // qw, dk, qw), lambda bi, g, qi: (bi, g, qi, 0, 0)),
                  pl.BlockSpec((1, 1, s, dk), lambda bi, g, qi: (bi, g, 0, 0)),
                  pl.BlockSpec((1, 1, nchunk, dv, cw), lambda bi, g, qi: (bi, g, 0, 0, 0))],
        out_specs=pl.BlockSpec((1, tq_blk, rep * dv), lambda bi, g, qi: (bi, qi, g)),
        out_shape=jax.ShapeDtypeStruct((b, s, hq * dv), BF16),
        scratch_shapes=[pltpu.VMEM((dv + BF16_ROWS, tq), F32), pltpu.VMEM((2, tk, tq), F32)],
        compiler_params=_cparams(("parallel", "parallel", "arbitrary")),
        name=name,
    )(qt, k, vt)


def _post_kernel(oa_ref, ob_ref, x_ref, wout_ref, gffn_ref, wr_ref, br_ref, x1_ref, hp_ref, route_ref):
    half = oa_ref.shape[-1]
    mix = jnp.dot(oa_ref[...], wout_ref[0:half, :], preferred_element_type=F32)
    mix = mix + jnp.dot(ob_ref[...], wout_ref[half:2 * half, :], preferred_element_type=F32)
    x1 = x_ref[...] + mix
    x1_ref[...] = x1
    h = _rms(x1, gffn_ref[...])

    dh = h.shape[-1] // 2
    h_hi = h.astype(BF16)
    h_hi32 = h_hi.astype(F32)
    lo = pltpu.bitcast(h_hi32[:, 0:dh], jnp.uint32)
    hi = pltpu.bitcast(h_hi32[:, dh:2 * dh], jnp.uint32)
    hp_ref[...] = hi | (lo >> 16)

    h_lo = (h - h_hi32).astype(BF16)
    lg2 = (jnp.dot(h_hi, wr_ref[...], preferred_element_type=F32)
           + jnp.dot(h_lo, wr_ref[...], preferred_element_type=F32))
    lg = lg2[:, 0:LANES] + lg2[:, LANES:2 * LANES] + br_ref[...]
    lane = lax.broadcasted_iota(jnp.int32, lg.shape, 1)
    neg = jnp.float32(-jnp.inf)
    big = jnp.int32(4 * LANES)
    gl = jnp.where(lane < N_GROUPS, lg, neg)
    gmax = jnp.max(gl, axis=-1, keepdims=True)
    gsum = jnp.sum(jnp.exp(gl - gmax), axis=-1, keepdims=True)
    g_w = 1.0 / gsum
    g_idx = jnp.min(jnp.where(gl == gmax, lane, big), axis=-1, keepdims=True)
    lo_lane = N_GROUPS + EXPERTS_PER_GROUP * g_idx
    el = jnp.where((lane >= lo_lane) & (lane < lo_lane + EXPERTS_PER_GROUP), lg, neg)
    m1 = jnp.max(el, axis=-1, keepdims=True)
    i1 = jnp.min(jnp.where(el == m1, lane, big), axis=-1, keepdims=True)
    el2 = jnp.where(lane == i1, neg, el)
    m2 = jnp.max(el2, axis=-1, keepdims=True)
    i2 = jnp.min(jnp.where(el2 == m2, lane, big), axis=-1, keepdims=True)
    e2 = jnp.exp(m2 - m1)
    w1 = 1.0 / (1.0 + e2)
    w2 = e2 / (1.0 + e2)
    out = jnp.where(lane == 0, (i1 - N_GROUPS).astype(F32),
          jnp.where(lane == 1, (i2 - N_GROUPS).astype(F32),
          jnp.where(lane == 2, g_w * w1,
          jnp.where(lane == 3, g_w * w2, 0.0))))
    route_ref[...] = out


def _post_call(oa, ob, x, wout, gffn, wr, br, tm):
    t, d = x.shape
    half = oa.shape[-1]
    const = lambda shape: pl.BlockSpec(shape, lambda i: (0,) * len(shape))
    row = lambda w: pl.BlockSpec((tm, w), lambda i: (i, 0))
    return pl.pallas_call(
        _post_kernel,
        grid=(t // tm,),
        in_specs=[row(half), row(half), row(d), const(wout.shape), const(gffn.shape),
                  const(wr.shape), const(br.shape)],
        out_specs=[row(d), row(d // 2), row(LANES)],
        out_shape=[jax.ShapeDtypeStruct((t, d), F32),
                   jax.ShapeDtypeStruct((t, d // 2), jnp.uint32),
                   jax.ShapeDtypeStruct((t, LANES), F32)],
        compiler_params=_cparams(("parallel",)),
        name="post_proj_router",
    )(oa, ob, x, wout, gffn, wr, br)


def _row_gather(src_hbm, idx_ref, buf, sem, n_rows):
    def body(r, carry):
        pltpu.make_async_copy(src_hbm.at[pl.ds(idx_ref[0, 0, r], 1)], buf.at[pl.ds(r, 1)], sem).start()
        return carry
    lax.fori_loop(0, n_rows, body, 0, unroll=GATHER_UNROLL)


def _moe_kernel(blk_e_ref, n_used_ref, tok_cur_ref, tok_next_ref, hp_hbm, wg_ref, wu_ref, wd_ref,
                out_ref, buf, sem, wgb, wub, wdb):
    i = pl.program_id(0)
    slot = i % 2
    tb = buf.shape[1]
    n_used = n_used_ref[0]

    @pl.when((i == 0) | (blk_e_ref[i] != blk_e_ref[jnp.maximum(i - 1, 0)]))
    def _():
        wgb[...] = wg_ref[0].astype(BF16)
        wub[...] = wu_ref[0].astype(BF16)
        wdb[...] = wd_ref[0].astype(BF16)

    @pl.when(i == 0)
    def _():
        _row_gather(hp_hbm, tok_cur_ref, buf.at[0], sem.at[0], tb)

    @pl.when(i + 1 < n_used)
    def _():
        _row_gather(hp_hbm, tok_next_ref, buf.at[1 - slot], sem.at[1 - slot], tb)

    @pl.when(i < n_used)
    def _():
        pltpu.make_async_copy(hp_hbm.at[pl.ds(0, tb)], buf.at[slot], sem.at[slot]).wait()
        w = buf[slot]
        dh = w.shape[-1]
        x_lo = pltpu.bitcast(w << 16, F32).astype(BF16)
        x_hi = pltpu.bitcast(w & jnp.uint32(0xFFFF0000), F32).astype(BF16)
        g = jnp.dot(x_lo, wgb[0:dh, :], preferred_element_type=F32)
        g = g + jnp.dot(x_hi, wgb[dh:2 * dh, :], preferred_element_type=F32)
        u = jnp.dot(x_lo, wub[0:dh, :], preferred_element_type=F32)
        u = u + jnp.dot(x_hi, wub[dh:2 * dh, :], preferred_element_type=F32)
        hid = (g * jax.nn.sigmoid(g)) * u
        out_ref[...] = jnp.dot(hid.astype(BF16), wdb[...], preferred_element_type=F32)

    @pl.when(i >= n_used)
    def _():
        out_ref[...] = jnp.zeros(out_ref.shape, out_ref.dtype)


def _moe_call(blk_e, n_used, pad_tok, hp, wg, wu, wd, tb):
    nb = pad_tok.shape[0]
    dh = hp.shape[-1]
    d = 2 * dh
    de = wg.shape[-1]
    smem_blk = lambda f: pl.BlockSpec((1, 1, tb), f, memory_space=pltpu.SMEM)
    wspec = lambda shape: pl.BlockSpec(shape, lambda i, be, nu: (be[i], 0, 0))
    grid_spec = pltpu.PrefetchScalarGridSpec(
        num_scalar_prefetch=2,
        grid=(nb,),
        in_specs=[smem_blk(lambda i, be, nu: (i, 0, 0)),
                  smem_blk(lambda i, be, nu: (jnp.minimum(i + 1, nb - 1), 0, 0)),
                  pl.BlockSpec(memory_space=pl.ANY),
                  wspec((1, d, de)), wspec((1, d, de)), wspec((1, de, d))],
        out_specs=pl.BlockSpec((tb, d), lambda i, be, nu: (i, 0)),
        scratch_shapes=[pltpu.VMEM((2, tb, dh), jnp.uint32), pltpu.SemaphoreType.DMA((2,)),
                        pltpu.VMEM((d, de), BF16), pltpu.VMEM((d, de), BF16), pltpu.VMEM((de, d), BF16)],
    )
    return pl.pallas_call(
        _moe_kernel,
        grid_spec=grid_spec,
        out_shape=jax.ShapeDtypeStruct((nb * tb, d), F32),
        compiler_params=_cparams(("arbitrary",)),
        name="moe_experts",
    )(blk_e, n_used, pad_tok, pad_tok, hp, wg, wu, wd)


def _final_kernel(dest_cur_ref, dest_next_ref, x1_ref, route_ref, rows_hbm, gfin_ref, y_ref, buf, sem):
    i = pl.program_id(0)
    nsteps = pl.num_programs(0)
    slot = i % 2
    n_rows = buf.shape[1]
    tm = n_rows // TOP_K

    @pl.when(i == 0)
    def _():
        _row_gather(rows_hbm, dest_cur_ref, buf.at[0], sem.at[0], n_rows)

    @pl.when(i + 1 < nsteps)
    def _():
        _row_gather(rows_hbm, dest_next_ref, buf.at[1 - slot], sem.at[1 - slot], n_rows)

    pltpu.make_async_copy(rows_hbm.at[pl.ds(0, n_rows)], buf.at[slot], sem.at[slot]).wait()
    route = route_ref[...]
    r0 = buf[slot, 0:tm, :] * route[:, 2:3]
    r1 = buf[slot, tm:2 * tm, :] * route[:, 3:4]
    y_ref[...] = _rms(x1_ref[...] + (r0 + r1), gfin_ref[...])


def _final_call(dest, x1, route, rows, gfin, tm):
    t, d = x1.shape
    nt = t // tm
    smem_blk = lambda f: pl.BlockSpec((1, 1, TOP_K * tm), f, memory_space=pltpu.SMEM)
    return pl.pallas_call(
        _final_kernel,
        grid=(nt,),
        in_specs=[smem_blk(lambda i: (i, 0, 0)),
                  smem_blk(lambda i: (jnp.minimum(i + 1, nt - 1), 0, 0)),
                  pl.BlockSpec((tm, d), lambda i: (i, 0)),
                  pl.BlockSpec((tm, LANES), lambda i: (i, 0)),
                  pl.BlockSpec(memory_space=pl.ANY),
                  pl.BlockSpec(gfin.shape, lambda i: (0, 0))],
        out_specs=pl.BlockSpec((tm, d), lambda i: (i, 0)),
        out_shape=jax.ShapeDtypeStruct((t, d), F32),
        scratch_shapes=[pltpu.VMEM((2, TOP_K * tm, d), F32), pltpu.SemaphoreType.DMA((2,))],
        compiler_params=_cparams(("arbitrary",)),
        name="combine_final_norm",
    )(dest, dest, x1, route, rows, gfin)


def _deinterleave(n):
    return np.concatenate([np.arange(0, n, 2), np.arange(1, n, 2)])


def _rope_pad_cols(base):
    pad = -np.ones(LANES // 4, np.int64)
    return np.concatenate([base + np.arange(0, QK_ROPE, 2), pad, base + np.arange(1, QK_ROPE, 2), pad])


def _take_cols(w, cols):
    w_ext = jnp.concatenate([w, jnp.zeros((w.shape[0], 1), w.dtype)], axis=1)
    return w_ext[:, np.where(cols < 0, w.shape[1], cols)]


def _prep_weights(w_in, w_q_up, q_norm, k_norm):
    o_kr = Q_LORA + KV_LORA
    o_gq = o_kr + QK_ROPE
    o_gk = o_gq + GQA_HEADS * HEAD_DIM
    o_gv = o_gk + GQA_KV_HEADS * HEAD_DIM
    cols = [np.arange(0, o_kr), _rope_pad_cols(o_kr)]
    for hd in range(GQA_HEADS):
        cols.append(o_gq + hd * HEAD_DIM + _deinterleave(HEAD_DIM))
    for hd in range(GQA_KV_HEADS):
        cols.append(o_gk + hd * HEAD_DIM + _deinterleave(HEAD_DIM))
    cols.append(np.arange(o_gv, o_gv + GQA_KV_HEADS * HEAD_DIM))
    win = _take_cols(w_in, np.concatenate(cols)).astype(BF16)

    qcols = []
    for hd in range(MLA_HEADS):
        base = hd * (QK_NOPE + QK_ROPE)
        qcols += [base + np.arange(QK_NOPE), _rope_pad_cols(base + QK_NOPE)]
    wq = _take_cols(w_q_up, np.concatenate(qcols)).astype(BF16)
    perm = _deinterleave(HEAD_DIM)
    return win, wq, q_norm[perm][None, :], k_norm[perm][None, :]


def _rope_tables(s):
    rows = s // GRID_W
    row = jnp.repeat(jnp.arange(rows, dtype=F32), GRID_W)
    col = jnp.tile(jnp.arange(GRID_W, dtype=F32), rows)

    def cos_sin(rot_dim):
        n_pairs = rot_dim // 4
        freqs = ROPE_THETA ** (-jnp.arange(n_pairs, dtype=F32) * 2.0 / (rot_dim // 2))
        ang = jnp.concatenate([row[:, None] * freqs[None, :], col[:, None] * freqs[None, :]], axis=-1)
        return jnp.cos(ang), jnp.sin(ang)

    ca, sa = cos_sin(QK_ROPE)
    cb, sb = cos_sin(HEAD_DIM)
    return (jnp.concatenate([ca, ca, ca, ca], -1), jnp.concatenate([-sa, -sa, sa, sa], -1),
            jnp.concatenate([cb, cb], -1), jnp.concatenate([-sb, sb], -1))


def _pick(n, pref):
    return pref if n % pref == 0 else n


def _route_plan(route, tb):
    t = route.shape[0]
    a = t * TOP_K
    expert_id = route[:, 0:TOP_K].astype(jnp.int32).reshape(a)
    onehot = (expert_id[:, None] == jnp.arange(N_EXPERTS, dtype=jnp.int32)[None, :]).astype(jnp.int32)
    csum = jnp.cumsum(onehot, axis=0)
    counts = csum[-1]
    rank = jnp.sum(onehot * csum, axis=1) - 1
    padded = ((counts + tb - 1) // tb) * tb
    pad_ends = jnp.cumsum(padded)
    pad_off = pad_ends - padded
    dest = pad_off[expert_id] + rank
    nb = (a + N_EXPERTS * tb) // tb
    blk_start = jnp.arange(nb, dtype=jnp.int32) * tb
    blk_e = jnp.minimum(jnp.sum(blk_start[:, None] >= pad_ends[None, :], axis=1), N_EXPERTS - 1)
    n_used = (pad_ends[-1] // tb).astype(jnp.int32).reshape(1)
    order = jnp.sort(expert_id * a + jnp.arange(a, dtype=jnp.int32)) % a
    row_e = jnp.repeat(blk_e, tb)
    row_rank = jnp.arange(nb * tb, dtype=jnp.int32) - pad_off[row_e]
    offsets = jnp.cumsum(counts) - counts
    src = jnp.clip(offsets[row_e] + row_rank, 0, a - 1)
    valid = (row_rank < counts[row_e]) & (jnp.arange(nb * tb, dtype=jnp.int32) < pad_ends[-1])
    pad_tok = jnp.where(valid, order[src] // TOP_K, 0)
    last_e = blk_e[jnp.maximum(n_used[0] - 1, 0)]
    blk_e = jnp.where(jnp.arange(nb) < n_used[0], blk_e, last_e).astype(jnp.int32)
    return pad_tok.reshape(nb, 1, tb), dest.reshape(t, TOP_K), blk_e, n_used


def _trunk(x, p):
    b, s, d = x.shape
    t = b * s
    tm = _pick(s, 256)
    qa, ka, va, qb, kb, vb = _pre_call(x, p["gattn"], p["win"], p["gqa"], p["wq"], p["gkv"], p["wkv"],
                                       p["gqn"], p["gkn"], *p["tables"], tm)
    tk = 2 * tm if s % (4 * tm) == 0 else s // 2
    oa = _flash_call(qa, ka, va, _pick(s, 1024), 2 if s % 2048 == 0 else 1, tk, "mla_flash")
    ob = _flash_call(qb, kb, vb, _pick(s, 512), 1, tk, "gqa_flash")
    x1, hp, route = _post_call(oa.reshape(t, -1), ob.reshape(t, -1), x.reshape(t, d), p["wout"], p["gffn"],
                               p["wr"], p["br"], tm)
    tb = 256
    pad_tok, dest, blk_e, n_used = _route_plan(route, tb)
    rows = _moe_call(blk_e, n_used, pad_tok, hp, p["wg"], p["wu"], p["wd"], tb)
    nt = t // tm
    dest_blk = dest.reshape(nt, tm, TOP_K).transpose(0, 2, 1).reshape(nt, 1, TOP_K * tm)
    y = _final_call(dest_blk, x1, route, rows, p["gfin"], tm)
    return y.reshape(b, s, d)


def kernel(x_prompt, x_sample, attn_norm, w_in, q_a_norm, w_q_up, kv_a_norm, w_kv_up, q_norm, k_norm, w_out,
           ffn_norm, w_group, b_group, w_expert, b_expert, w_gate, w_up, w_down, final_norm):
    assert attn_norm.shape[0] == 1, "single-layer trunk"
    win, wq, gqn, gkn = _prep_weights(w_in[0], w_q_up[0], q_norm[0], k_norm[0])
    d = w_in.shape[1]
    wr = jnp.zeros((d, LANES), F32).at[:, 0:N_GROUPS].set(w_group[0])
    wr = wr.at[:, N_GROUPS:N_GROUPS + N_EXPERTS].set(w_expert[0])
    br = jnp.zeros((1, LANES), F32).at[0, 0:N_GROUPS].set(b_group[0])
    br = br.at[0, N_GROUPS:N_GROUPS + N_EXPERTS].set(b_expert[0])
    wr_hi = wr.astype(BF16)
    wr = jnp.concatenate([wr_hi, (wr - wr_hi.astype(F32)).astype(BF16)], axis=1)
    p = dict(gattn=attn_norm, win=win, gqa=q_a_norm, wq=wq, gkv=kv_a_norm, wkv=w_kv_up[0].astype(BF16),
             gqn=gqn, gkn=gkn, wout=w_out[0].astype(BF16), gffn=ffn_norm, wr=wr, br=br,
             wg=w_gate[0], wu=w_up[0], wd=w_down[0],
             gfin=final_norm[None, :])
    outs = []
    for x in (x_prompt, x_sample):
        p["tables"] = _rope_tables(x.shape[1])
        outs.append(_trunk(x, p))
    return tuple(outs)
```

```python
import functools
import math

import numpy as np
import jax
import jax.numpy as jnp
from jax import lax
from jax.experimental import pallas as pl
from jax.experimental.pallas import tpu as pltpu

F32 = jnp.float32
BF16 = jnp.bfloat16

EPS = 1e-6
ROPE_THETA = 10000.0
GRID_W = 64
LANES = 128
BF16_ROWS = 16
GATHER_UNROLL = 8

MLA_HEADS = 8
Q_LORA = 512
KV_LORA = 256
QK_NOPE = 128
QK_ROPE = 64
V_HEAD = 128
MLA_DK = 256
GQA_HEADS = 8
GQA_KV_HEADS = 2
GQA_REP = GQA_HEADS // GQA_KV_HEADS
HEAD_DIM = 128
N_GROUPS = 4
EXPERTS_PER_GROUP = 8
N_EXPERTS = N_GROUPS * EXPERTS_PER_GROUP
TOP_K = 2
LOG2E = math.log2(math.e)

VMEM_LIMIT = 56 * 1024 * 1024


def _cparams(sem):
    return pltpu.CompilerParams(dimension_semantics=sem, vmem_limit_bytes=VMEM_LIMIT)


def _rms(x, gain):
    return x * lax.rsqrt(jnp.mean(x * x, axis=-1, keepdims=True) + EPS) * gain


def _rope(x, cos_t, sin_t):
    return x * cos_t + pltpu.roll(x, LANES // 2, axis=1) * sin_t


def _pre_kernel(x_ref, gattn_ref, win_ref, gqa_ref, wq_ref, gkv_ref, wkv_ref, gqn_ref, gkn_ref,
                ca_ref, sa_ref, cb_ref, sb_ref,
                qa_ref, ka_ref, va_ref, qb_ref, kb_ref, vb_ref):
    x = x_ref[0]
    h = _rms(x, gattn_ref[...]).astype(BF16)
    proj = jnp.dot(h, win_ref[...], preferred_element_type=F32)
    o = 0
    q_lat = proj[:, o:o + Q_LORA]; o += Q_LORA
    kv_lat = proj[:, o:o + KV_LORA]; o += KV_LORA
    k_rope = proj[:, o:o + LANES]; o += LANES
    gq = proj[:, o:o + GQA_HEADS * HEAD_DIM]; o += GQA_HEADS * HEAD_DIM
    gk = proj[:, o:o + GQA_KV_HEADS * HEAD_DIM]; o += GQA_KV_HEADS * HEAD_DIM
    gv = proj[:, o:o + GQA_KV_HEADS * HEAD_DIM]

    ca, sa, cb, sb = ca_ref[...], sa_ref[...], cb_ref[...], sb_ref[...]

    q = jnp.dot(_rms(q_lat, gqa_ref[...]).astype(BF16), wq_ref[...], preferred_element_type=F32)
    kv = jnp.dot(_rms(kv_lat, gkv_ref[...]).astype(BF16), wkv_ref[...], preferred_element_type=F32)
    k_pe = _rope(k_rope, ca, sa).astype(BF16)
    sc_a = np.float32((QK_NOPE + QK_ROPE) ** -0.5 * LOG2E)
    for hd in range(MLA_HEADS):
        qh = q[:, hd * MLA_DK:(hd + 1) * MLA_DK]
        qa_ref[0, hd, 0, 0:LANES, :] = (qh[:, 0:LANES] * sc_a).T.astype(BF16)
        qa_ref[0, hd, 0, LANES:MLA_DK, :] = (_rope(qh[:, LANES:MLA_DK], ca, sa) * sc_a).T.astype(BF16)
        kvh = kv[:, hd * 2 * LANES:(hd + 1) * 2 * LANES]
        ka_ref[0, hd, :, 0:LANES] = kvh[:, 0:LANES].astype(BF16)
        ka_ref[0, hd, :, LANES:MLA_DK] = k_pe
        va_ref[0, hd, 0] = kvh[:, LANES:2 * LANES].T.astype(BF16)

    sc_b = np.float32(HEAD_DIM ** -0.5 * LOG2E)
    gqn, gkn = gqn_ref[...], gkn_ref[...]
    for hd in range(GQA_HEADS):
        qh = _rope(_rms(gq[:, hd * HEAD_DIM:(hd + 1) * HEAD_DIM], gqn), cb, sb)
        qb_ref[0, hd, 0] = (qh * sc_b).T.astype(BF16)
    for hd in range(GQA_KV_HEADS):
        kh = _rope(_rms(gk[:, hd * HEAD_DIM:(hd + 1) * HEAD_DIM], gkn), cb, sb)
        kb_ref[0, hd] = kh.astype(BF16)
        vb_ref[0, hd, 0] = gv[:, hd * HEAD_DIM:(hd + 1) * HEAD_DIM].T.astype(BF16)


def _pre_call(x, gattn, win, gqa, wq, gkv, wkv, gqn, gkn, ca, sa, cb, sb, tm):
    b, s, d = x.shape
    const = lambda shape: pl.BlockSpec(shape, lambda bi, si: (0,) * len(shape))
    tab = pl.BlockSpec((tm, LANES), lambda bi, si: (si, 0))
    head = lambda nh, w: pl.BlockSpec((1, nh, tm, w), lambda bi, si: (bi, 0, si, 0))
    head_t = lambda nh, w: pl.BlockSpec((1, nh, 1, w, tm), lambda bi, si: (bi, 0, si, 0, 0))
    return pl.pallas_call(
        _pre_kernel,
        grid=(b, s // tm),
        in_specs=[pl.BlockSpec((1, tm, d), lambda bi, si: (bi, si, 0)),
                  const(gattn.shape), const(win.shape), const(gqa.shape), const(wq.shape),
                  const(gkv.shape), const(wkv.shape), const(gqn.shape), const(gkn.shape),
                  tab, tab, tab, tab],
        out_specs=[head_t(MLA_HEADS, MLA_DK), head(MLA_HEADS, MLA_DK), head_t(MLA_HEADS, V_HEAD),
                   head_t(GQA_HEADS, HEAD_DIM), head(GQA_KV_HEADS, HEAD_DIM), head_t(GQA_KV_HEADS, HEAD_DIM)],
        out_shape=[jax.ShapeDtypeStruct((b, MLA_HEADS, s // tm, MLA_DK, tm), BF16),
                   jax.ShapeDtypeStruct((b, MLA_HEADS, s, MLA_DK), BF16),
                   jax.ShapeDtypeStruct((b, MLA_HEADS, s // tm, V_HEAD, tm), BF16),
                   jax.ShapeDtypeStruct((b, GQA_HEADS, s // tm, HEAD_DIM, tm), BF16),
                   jax.ShapeDtypeStruct((b, GQA_KV_HEADS, s, HEAD_DIM), BF16),
                   jax.ShapeDtypeStruct((b, GQA_KV_HEADS, s // tm, HEAD_DIM, tm), BF16)],
        compiler_params=_cparams(("parallel", "parallel")),
        name="pre_proj",
    )(x, gattn, win, gqa, wq, gkv, wkv, gqn, gkn, ca, sa, cb, sb)


def _flash_kernel(q_ref, k_ref, vt_ref, o_ref, acc_sc, s_sc, *, streams, tq, tk):
    s_len = k_ref.shape[2]
    dv, cw = vt_ref.shape[3], vt_ref.shape[4]
    qw = q_ref.shape[4]
    n_qc = tq // qw
    n_sub = tk // cw
    n_steps = s_len // tk
    unroll = 8 if n_steps % 8 == 0 else 2
    n_trips = n_steps // unroll
    ones = jnp.ones((BF16_ROWS, cw), BF16)

    def scores(r, c0, j, slot):
        start = j * tk if isinstance(j, int) else pl.multiple_of(j * tk, tk)
        k = k_ref[0, 0, pl.ds(start, tk), :]
        st = jnp.concatenate([jnp.dot(k, q_ref[0, r, c0 + c], preferred_element_type=F32)
                              for c in range(n_qc)], axis=1)
        s_sc[slot] = st
        return jnp.max(st, axis=0, keepdims=True)

    def key_step(j, slot, m_prev, mx, next_scores):
        mx_next = next_scores() if next_scores is not None else mx
        m_new = jnp.maximum(m_prev, mx)
        alpha = jnp.exp2(m_prev - m_new)
        pb = jnp.exp2(s_sc[slot] - m_new).astype(BF16)
        pv = None
        for c in range(n_sub):
            vt1 = jnp.concatenate([vt_ref[0, 0, j * n_sub + c], ones], axis=0)
            d = jnp.dot(vt1, pb[c * cw:(c + 1) * cw], preferred_element_type=F32)
            pv = d if pv is None else pv + d
        acc_sc[...] = alpha * acc_sc[...] + pv
        return m_new, mx_next

    mx = scores(streams[0][0], streams[0][1], 0, 0)
    for si, stream in enumerate(streams):
        r, c0, row, col = stream
        r_nx, c0_nx, j_nx = (streams[si + 1][0], streams[si + 1][1], 0) if si + 1 < len(streams) \
            else (r, c0, n_steps - 1)
        acc_sc[...] = jnp.zeros(acc_sc.shape, F32)

        def body(jj, carry, r=r, c0=c0, r_nx=r_nx, c0_nx=c0_nx, j_nx=j_nx):
            m_prev, mx = carry
            for u in range(unroll):
                j = jj * unroll + u
                if u + 1 < unroll:
                    nxt = functools.partial(scores, r, c0, j + 1, 1 - u % 2)
                else:
                    last = jj == n_trips - 1
                    nxt = functools.partial(scores, jnp.where(last, r_nx, r), jnp.where(last, c0_nx, c0),
                                            jnp.where(last, j_nx, j + 1), 1 - u % 2)
                m_prev, mx = key_step(j, u % 2, m_prev, mx, nxt)
            return m_prev, mx

        _, mx = lax.fori_loop(0, n_trips, body, (jnp.full((1, tq), -jnp.inf, F32), mx))
        out_t = acc_sc[0:dv, :] / acc_sc[dv:dv + 1, :]
        o_ref[0, row:row + tq, col:col + dv] = out_t.T.astype(o_ref.dtype)


def _flash_call(qt, k, vt, tq, n_tiles, tk, name):
    b, hq, _, dk, qw = qt.shape
    s = k.shape[2]
    hkv = k.shape[1]
    _, _, nchunk, dv, cw = vt.shape
    rep = hq // hkv
    tq_blk = tq * n_tiles
    assert tk % cw == 0 and (s // tk) % 2 == 0 and s % tq_blk == 0 and tq % qw == 0
    streams = tuple((r, ti * (tq // qw), ti * tq, r * dv) for r in range(rep) for ti in range(n_tiles))
    return pl.pallas_call(
        functools.partial(_flash_kernel, streams=streams, tq=tq, tk=tk),
        grid=(b, hkv, s // tq_blk),
        in_specs=[pl.BlockSpec((1, rep, tq_blk // qw, dk, qw), lambda bi, g, qi: (bi, g, qi, 0, 0)),
                  pl.BlockSpec((1, 1, s, dk), lambda bi, g, qi: (bi, g, 0, 0)),
                  pl.BlockSpec((1, 1, nchunk, dv, cw), lambda bi, g, qi: (bi, g, 0, 0, 0))],
        out_specs=pl.BlockSpec((1, tq_blk, rep * dv), lambda bi, g, qi: (bi, qi, g)),
        out_shape=jax.ShapeDtypeStruct((b, s, hq * dv), BF16),
        scratch_shapes=[pltpu.VMEM((dv + BF16_ROWS, tq), F32), pltpu.VMEM((2, tk, tq), F32)],
        compiler_params=_cparams(("parallel", "parallel", "arbitrary")),
        name=name,
    )(qt, k, vt)


def _post_kernel(oa_ref, ob_ref, x_ref, wout_ref, gffn_ref, wr_ref, br_ref, x1_ref, hp_ref, route_ref):
    half = oa_ref.shape[-1]
    mix = jnp.dot(oa_ref[...], wout_ref[0:half, :], preferred_element_type=F32)
    mix = mix + jnp.dot(ob_ref[...], wout_ref[half:2 * half, :], preferred_element_type=F32)
    x1 = x_ref[...] + mix
    x1_ref[...] = x1
    h = _rms(x1, gffn_ref[...])

    dh = h.shape[-1] // 2
    h_hi = h.astype(BF16)
    h_hi32 = h_hi.astype(F32)
    lo = pltpu.bitcast(h_hi32[:, 0:dh], jnp.uint32)
    hi = pltpu.bitcast(h_hi32[:, dh:2 * dh], jnp.uint32)
    hp_ref[...] = hi | (lo >> 16)

    h_lo = (h - h_hi32).astype(BF16)
    lg2 = (jnp.dot(h_hi, wr_ref[...], preferred_element_type=F32)
           + jnp.dot(h_lo, wr_ref[...], preferred_element_type=F32))
    lg = lg2[:, 0:LANES] + lg2[:, LANES:2 * LANES] + br_ref[...]
    lane = lax.broadcasted_iota(jnp.int32, lg.shape, 1)
    neg = jnp.float32(-jnp.inf)
    big = jnp.int32(4 * LANES)
    gl = jnp.where(lane < N_GROUPS, lg, neg)
    gmax = jnp.max(gl, axis=-1, keepdims=True)
    gsum = jnp.sum(jnp.exp(gl - gmax), axis=-1, keepdims=True)
    g_w = 1.0 / gsum
    g_idx = jnp.min(jnp.where(gl == gmax, lane, big), axis=-1, keepdims=True)
    lo_lane = N_GROUPS + EXPERTS_PER_GROUP * g_idx
    el = jnp.where((lane >= lo_lane) & (lane < lo_lane + EXPERTS_PER_GROUP), lg, neg)
    m1 = jnp.max(el, axis=-1, keepdims=True)
    i1 = jnp.min(jnp.where(el == m1, lane, big), axis=-1, keepdims=True)
    el2 = jnp.where(lane == i1, neg, el)
    m2 = jnp.max(el2, axis=-1, keepdims=True)
    i2 = jnp.min(jnp.where(el2 == m2, lane, big), axis=-1, keepdims=True)
    e2 = jnp.exp(m2 - m1)
    w1 = 1.0 / (1.0 + e2)
    w2 = e2 / (1.0 + e2)
    out = jnp.where(lane == 0, (i1 - N_GROUPS).astype(F32),
          jnp.where(lane == 1, (i2 - N_GROUPS).astype(F32),
          jnp.where(lane == 2, g_w * w1,
          jnp.where(lane == 3, g_w * w2, 0.0))))
    route_ref[...] = out


def _post_call(oa, ob, x, wout, gffn, wr, br, tm):
    t, d = x.shape
    half = oa.shape[-1]
    const = lambda shape: pl.BlockSpec(shape, lambda i: (0,) * len(shape))
    row = lambda w: pl.BlockSpec((tm, w), lambda i: (i, 0))
    return pl.pallas_call(
        _post_kernel,
        grid=(t // tm,),
        in_specs=[row(half), row(half), row(d), const(wout.shape), const(gffn.shape),
                  const(wr.shape), const(br.shape)],
        out_specs=[row(d), row(d // 2), row(LANES)],
        out_shape=[jax.ShapeDtypeStruct((t, d), F32),
                   jax.ShapeDtypeStruct((t, d // 2), jnp.uint32),
                   jax.ShapeDtypeStruct((t, LANES), F32)],
        compiler_params=_cparams(("parallel",)),
        name="post_proj_router",
    )(oa, ob, x, wout, gffn, wr, br)


def _row_gather(src_hbm, idx_ref, buf, sem, n_rows):
    def body(r, carry):
        pltpu.make_async_copy(src_hbm.at[pl.ds(idx_ref[0, 0, r], 1)], buf.at[pl.ds(r, 1)], sem).start()
        return carry
    lax.fori_loop(0, n_rows, body, 0, unroll=GATHER_UNROLL)


def _moe_kernel(blk_e_ref, n_used_ref, tok_cur_ref, tok_next_ref, hp_hbm, wg_ref, wu_ref, wd_ref,
                out_ref, buf0, buf1, sem, wgb, wub, wdb):
    i = pl.program_id(0)
    tb, dh = buf0.shape
    n_used = n_used_ref[0]

    @pl.when((i == 0) | (blk_e_ref[i] != blk_e_ref[jnp.maximum(i - 1, 0)]))
    def _():
        wgb[...] = wg_ref[0].astype(BF16)
        wub[...] = wu_ref[0].astype(BF16)
        wdb[...] = wd_ref[0].astype(BF16)

    @pl.when(i == 0)
    def _():
        _row_gather(hp_hbm, tok_cur_ref, buf0, sem.at[0], tb)

    def wait_rows(buf, s_):
        pltpu.make_async_copy(hp_hbm.at[pl.ds(0, tb)], buf, sem.at[s_]).wait()

    def block(cur, cur_s, nxt, nxt_s):
        wait_rows(cur, cur_s)
        for r in range(tb):
            pltpu.make_async_copy(hp_hbm.at[pl.ds(tok_next_ref[0, 0, r], 1)], nxt.at[pl.ds(r, 1)],
                                  sem.at[nxt_s]).start()
        w = cur[...]
        x_lo = pltpu.bitcast(w << 16, F32).astype(BF16)
        x_hi = pltpu.bitcast(w & jnp.uint32(0xFFFF0000), F32).astype(BF16)
        g = jnp.dot(x_lo, wgb[0:dh, :], preferred_element_type=F32)
        g = g + jnp.dot(x_hi, wgb[dh:2 * dh, :], preferred_element_type=F32)
        u = jnp.dot(x_lo, wub[0:dh, :], preferred_element_type=F32)
        u = u + jnp.dot(x_hi, wub[dh:2 * dh, :], preferred_element_type=F32)
        hid = (g * jax.nn.sigmoid(g)) * u
        out_ref[...] = jnp.dot(hid.astype(BF16), wdb[...], preferred_element_type=F32)

    used = i < n_used
    even = i % 2 == 0

    @pl.when(used & even)
    def _():
        block(buf0, 0, buf1, 1)

    @pl.when(used & jnp.logical_not(even))
    def _():
        block(buf1, 1, buf0, 0)

    @pl.when((i == n_used) & even)
    def _():
        wait_rows(buf0, 0)

    @pl.when((i == n_used) & jnp.logical_not(even))
    def _():
        wait_rows(buf1, 1)

    @pl.when(i >= n_used)
    def _():
        out_ref[...] = jnp.zeros(out_ref.shape, out_ref.dtype)


def _moe_call(blk_e, n_used, pad_tok, hp, wg, wu, wd, tb):
    nb = pad_tok.shape[0]
    dh = hp.shape[-1]
    d = 2 * dh
    de = wg.shape[-1]
    smem_blk = lambda f: pl.BlockSpec((1, 1, tb), f, memory_space=pltpu.SMEM)
    wspec = lambda shape: pl.BlockSpec(shape, lambda i, be, nu: (be[i], 0, 0))
    grid_spec = pltpu.PrefetchScalarGridSpec(
        num_scalar_prefetch=2,
        grid=(nb,),
        in_specs=[smem_blk(lambda i, be, nu: (i, 0, 0)),
                  smem_blk(lambda i, be, nu: (jnp.minimum(i + 1, nb - 1), 0, 0)),
                  pl.BlockSpec(memory_space=pl.ANY),
                  wspec((1, d, de)), wspec((1, d, de)), wspec((1, de, d))],
        out_specs=pl.BlockSpec((tb, d), lambda i, be, nu: (i, 0)),
        scratch_shapes=[pltpu.VMEM((tb, dh), jnp.uint32), pltpu.VMEM((tb, dh), jnp.uint32),
                        pltpu.SemaphoreType.DMA((2,)),
                        pltpu.VMEM((d, de), BF16), pltpu.VMEM((d, de), BF16), pltpu.VMEM((de, d), BF16)],
    )
    return pl.pallas_call(
        _moe_kernel,
        grid_spec=grid_spec,
        out_shape=jax.ShapeDtypeStruct((nb * tb, d), F32),
        compiler_params=_cparams(("arbitrary",)),
        name="moe_experts",
    )(blk_e, n_used, pad_tok, pad_tok, hp, wg, wu, wd)


def _final_kernel(dest_cur_ref, dest_next_ref, x1_ref, route_ref, rows_hbm, gfin_ref, y_ref, buf0, buf1, sem):
    i = pl.program_id(0)
    last = i == pl.num_programs(0) - 1
    n_rows = buf0.shape[0]
    tm = n_rows // TOP_K

    @pl.when(i == 0)
    def _():
        _row_gather(rows_hbm, dest_cur_ref, buf0, sem.at[0], n_rows)

    def wait_rows(buf, s_):
        pltpu.make_async_copy(rows_hbm.at[pl.ds(0, n_rows)], buf, sem.at[s_]).wait()

    def block(cur, cur_s, nxt, nxt_s):
        wait_rows(cur, cur_s)
        for r in range(n_rows):
            pltpu.make_async_copy(rows_hbm.at[pl.ds(dest_next_ref[0, 0, r], 1)], nxt.at[pl.ds(r, 1)],
                                  sem.at[nxt_s]).start()
        route = route_ref[...]
        r0 = cur[0:tm, :] * route[:, 2:3]
        r1 = cur[tm:2 * tm, :] * route[:, 3:4]
        y_ref[...] = _rms(x1_ref[...] + (r0 + r1), gfin_ref[...])

    even = i % 2 == 0

    @pl.when(even)
    def _():
        block(buf0, 0, buf1, 1)

    @pl.when(jnp.logical_not(even))
    def _():
        block(buf1, 1, buf0, 0)

    @pl.when(last & even)
    def _():
        wait_rows(buf1, 1)

    @pl.when(last & jnp.logical_not(even))
    def _():
        wait_rows(buf0, 0)


def _final_call(dest, x1, route, rows, gfin, tm):
    t, d = x1.shape
    nt = t // tm
    smem_blk = lambda f: pl.BlockSpec((1, 1, TOP_K * tm), f, memory_space=pltpu.SMEM)
    return pl.pallas_call(
        _final_kernel,
        grid=(nt,),
        in_specs=[smem_blk(lambda i: (i, 0, 0)),
                  smem_blk(lambda i: (jnp.minimum(i + 1, nt - 1), 0, 0)),
                  pl.BlockSpec((tm, d), lambda i: (i, 0)),
                  pl.BlockSpec((tm, LANES), lambda i: (i, 0)),
                  pl.BlockSpec(memory_space=pl.ANY),
                  pl.BlockSpec(gfin.shape, lambda i: (0, 0))],
        out_specs=pl.BlockSpec((tm, d), lambda i: (i, 0)),
        out_shape=jax.ShapeDtypeStruct((t, d), F32),
        scratch_shapes=[pltpu.VMEM((TOP_K * tm, d), F32), pltpu.VMEM((TOP_K * tm, d), F32),
                        pltpu.SemaphoreType.DMA((2,))],
        compiler_params=_cparams(("arbitrary",)),
        name="combine_final_norm",
    )(dest, dest, x1, route, rows, gfin)


def _deinterleave(n):
    return np.concatenate([np.arange(0, n, 2), np.arange(1, n, 2)])


def _rope_pad_cols(base):
    pad = -np.ones(LANES // 4, np.int64)
    return np.concatenate([base + np.arange(0, QK_ROPE, 2), pad, base + np.arange(1, QK_ROPE, 2), pad])


def _take_cols(w, cols):
    w_ext = jnp.concatenate([w, jnp.zeros((w.shape[0], 1), w.dtype)], axis=1)
    return w_ext[:, np.where(cols < 0, w.shape[1], cols)]


def _prep_weights(w_in, w_q_up, q_norm, k_norm):
    o_kr = Q_LORA + KV_LORA
    o_gq = o_kr + QK_ROPE
    o_gk = o_gq + GQA_HEADS * HEAD_DIM
    o_gv = o_gk + GQA_KV_HEADS * HEAD_DIM
    cols = [np.arange(0, o_kr), _rope_pad_cols(o_kr)]
    for hd in range(GQA_HEADS):
        cols.append(o_gq + hd * HEAD_DIM + _deinterleave(HEAD_DIM))
    for hd in range(GQA_KV_HEADS):
        cols.append(o_gk + hd * HEAD_DIM + _deinterleave(HEAD_DIM))
    cols.append(np.arange(o_gv, o_gv + GQA_KV_HEADS * HEAD_DIM))
    win = _take_cols(w_in, np.concatenate(cols)).astype(BF16)

    qcols = []
    for hd in range(MLA_HEADS):
        base = hd * (QK_NOPE + QK_ROPE)
        qcols += [base + np.arange(QK_NOPE), _rope_pad_cols(base + QK_NOPE)]
    wq = _take_cols(w_q_up, np.concatenate(qcols)).astype(BF16)
    perm = _deinterleave(HEAD_DIM)
    return win, wq, q_norm[perm][None, :], k_norm[perm][None, :]


def _rope_tables(s):
    rows = s // GRID_W
    row = jnp.repeat(jnp.arange(rows, dtype=F32), GRID_W)
    col = jnp.tile(jnp.arange(GRID_W, dtype=F32), rows)

    def cos_sin(rot_dim):
        n_pairs = rot_dim // 4
        freqs = ROPE_THETA ** (-jnp.arange(n_pairs, dtype=F32) * 2.0 / (rot_dim // 2))
        ang = jnp.concatenate([row[:, None] * freqs[None, :], col[:, None] * freqs[None, :]], axis=-1)
        return jnp.cos(ang), jnp.sin(ang)

    ca, sa = cos_sin(QK_ROPE)
    cb, sb = cos_sin(HEAD_DIM)
    return (jnp.concatenate([ca, ca, ca, ca], -1), jnp.concatenate([-sa, -sa, sa, sa], -1),
            jnp.concatenate([cb, cb], -1), jnp.concatenate([-sb, sb], -1))


def _pick(n, pref):
    return pref if n % pref == 0 else n


def _route_plan(route, tb):
    t = route.shape[0]
    a = t * TOP_K
    expert_id = route[:, 0:TOP_K].astype(jnp.int32).reshape(a)
    onehot = (expert_id[:, None] == jnp.arange(N_EXPERTS, dtype=jnp.int32)[None, :]).astype(jnp.int32)
    csum = jnp.cumsum(onehot, axis=0)
    counts = csum[-1]
    rank = jnp.sum(onehot * csum, axis=1) - 1
    padded = ((counts + tb - 1) // tb) * tb
    pad_ends = jnp.cumsum(padded)
    pad_off = pad_ends - padded
    dest = pad_off[expert_id] + rank
    nb = (a + N_EXPERTS * tb) // tb
    blk_start = jnp.arange(nb, dtype=jnp.int32) * tb
    blk_e = jnp.minimum(jnp.sum(blk_start[:, None] >= pad_ends[None, :], axis=1), N_EXPERTS - 1)
    n_used = (pad_ends[-1] // tb).astype(jnp.int32).reshape(1)
    order = jnp.sort(expert_id * a + jnp.arange(a, dtype=jnp.int32)) % a
    row_e = jnp.repeat(blk_e, tb)
    row_rank = jnp.arange(nb * tb, dtype=jnp.int32) - pad_off[row_e]
    offsets = jnp.cumsum(counts) - counts
    src = jnp.clip(offsets[row_e] + row_rank, 0, a - 1)
    valid = (row_rank < counts[row_e]) & (jnp.arange(nb * tb, dtype=jnp.int32) < pad_ends[-1])
    pad_tok = jnp.where(valid, order[src] // TOP_K, 0)
    last_e = blk_e[jnp.maximum(n_used[0] - 1, 0)]
    blk_e = jnp.where(jnp.arange(nb) < n_used[0], blk_e, last_e).astype(jnp.int32)
    return pad_tok.reshape(nb, 1, tb), dest.reshape(t, TOP_K), blk_e, n_used


def _trunk(x, p):
    b, s, d = x.shape
    t = b * s
    tm = _pick(s, 256)
    qa, ka, va, qb, kb, vb = _pre_call(x, p["gattn"], p["win"], p["gqa"], p["wq"], p["gkv"], p["wkv"],
                                       p["gqn"], p["gkn"], *p["tables"], tm)
    tk = 2 * tm if s % (4 * tm) == 0 else s // 2
    oa = _flash_call(qa, ka, va, _pick(s, 1024), 2 if s % 2048 == 0 else 1, tk, "mla_flash")
    ob = _flash_call(qb, kb, vb, _pick(s, 512), 1, tk, "gqa_flash")
    x1, hp, route = _post_call(oa.reshape(t, -1), ob.reshape(t, -1), x.reshape(t, d), p["wout"], p["gffn"],
                               p["wr"], p["br"], tm)
    tb = 256
    pad_tok, dest, blk_e, n_used = _route_plan(route, tb)
    rows = _moe_call(blk_e, n_used, pad_tok, hp, p["wg"], p["wu"], p["wd"], tb)
    nt = t // tm
    dest_blk = dest.reshape(nt, tm, TOP_K).transpose(0, 2, 1).reshape(nt, 1, TOP_K * tm)
    y = _final_call(dest_blk, x1, route, rows, p["gfin"], tm)
    return y.reshape(b, s, d)


def kernel(x_prompt, x_sample, attn_norm, w_in, q_a_norm, w_q_up, kv_a_norm, w_kv_up, q_norm, k_norm, w_out,
           ffn_norm, w_group, b_group, w_expert, b_expert, w_gate, w_up, w_down, final_norm):
    assert attn_norm.shape[0] == 1, "single-layer trunk"
    win, wq, gqn, gkn = _prep_weights(w_in[0], w_q_up[0], q_norm[0], k_norm[0])
    d = w_in.shape[1]
    wr = jnp.zeros((d, LANES), F32).at[:, 0:N_GROUPS].set(w_group[0])
    wr = wr.at[:, N_GROUPS:N_GROUPS + N_EXPERTS].set(w_expert[0])
    br = jnp.zeros((1, LANES), F32).at[0, 0:N_GROUPS].set(b_group[0])
    br = br.at[0, N_GROUPS:N_GROUPS + N_EXPERTS].set(b_expert[0])
    wr_hi = wr.astype(BF16)
    wr = jnp.concatenate([wr_hi, (wr - wr_hi.astype(F32)).astype(BF16)], axis=1)
    p = dict(gattn=attn_norm, win=win, gqa=q_a_norm, wq=wq, gkv=kv_a_norm, wkv=w_kv_up[0].astype(BF16),
             gqn=gqn, gkn=gkn, wout=w_out[0].astype(BF16), gffn=ffn_norm, wr=wr, br=br,
             wg=w_gate[0], wu=w_up[0], wd=w_down[0],
             gfin=final_norm[None, :])
    outs = []
    for x in (x_prompt, x_sample):
        p["tables"] = _rope_tables(x.shape[1])
        outs.append(_trunk(x, p))
    return tuple(outs)
```

```python
import functools
import math

import numpy as np
import jax
import jax.numpy as jnp
from jax import lax
from jax.experimental import pallas as pl
from jax.experimental.pallas import tpu as pltpu

F32 = jnp.float32
BF16 = jnp.bfloat16

EPS = 1e-6
ROPE_THETA = 10000.0
GRID_W = 64
LANES = 128
BF16_ROWS = 16
GATHER_UNROLL = 8

MLA_HEADS = 8
Q_LORA = 512
KV_LORA = 256
QK_NOPE = 128
QK_ROPE = 64
V_HEAD = 128
MLA_DK = 256
GQA_HEADS = 8
GQA_KV_HEADS = 2
GQA_REP = GQA_HEADS // GQA_KV_HEADS
HEAD_DIM = 128
N_GROUPS = 4
EXPERTS_PER_GROUP = 8
N_EXPERTS = N_GROUPS * EXPERTS_PER_GROUP
TOP_K = 2
LOG2E = math.log2(math.e)

VMEM_LIMIT = 56 * 1024 * 1024


def _cparams(sem):
    return pltpu.CompilerParams(dimension_semantics=sem, vmem_limit_bytes=VMEM_LIMIT)


def _rms(x, gain):
    return x * lax.rsqrt(jnp.mean(x * x, axis=-1, keepdims=True) + EPS) * gain


def _rope(x, cos_t, sin_t):
    return x * cos_t + pltpu.roll(x, LANES // 2, axis=1) * sin_t


def _pack_bf16_pairs(v):
    n = v.shape[-1] // 2
    w = pltpu.bitcast(v.astype(BF16).astype(F32), jnp.uint32)
    return w[:, n:2 * n] | (w[:, 0:n] >> 16)


def _unpack_bf16_pairs(w):
    return pltpu.bitcast(w << 16, F32), pltpu.bitcast(w & jnp.uint32(0xFFFF0000), F32)


def _pre_kernel(x_ref, gattn_ref, win_ref, gqa_ref, wq_ref, gkv_ref, wkv_ref, gqn_ref, gkn_ref,
                ca_ref, sa_ref, cb_ref, sb_ref,
                qa_ref, ka_ref, va_ref, qb_ref, kb_ref, vb_ref):
    x = x_ref[0]
    h = _rms(x, gattn_ref[...]).astype(BF16)
    proj = jnp.dot(h, win_ref[...], preferred_element_type=F32)
    o = 0
    q_lat = proj[:, o:o + Q_LORA]; o += Q_LORA
    kv_lat = proj[:, o:o + KV_LORA]; o += KV_LORA
    k_rope = proj[:, o:o + LANES]; o += LANES
    gq = proj[:, o:o + GQA_HEADS * HEAD_DIM]; o += GQA_HEADS * HEAD_DIM
    gk = proj[:, o:o + GQA_KV_HEADS * HEAD_DIM]; o += GQA_KV_HEADS * HEAD_DIM
    gv = proj[:, o:o + GQA_KV_HEADS * HEAD_DIM]

    ca, sa, cb, sb = ca_ref[...], sa_ref[...], cb_ref[...], sb_ref[...]

    q = jnp.dot(_rms(q_lat, gqa_ref[...]).astype(BF16), wq_ref[...], preferred_element_type=F32)
    kv = jnp.dot(_rms(kv_lat, gkv_ref[...]).astype(BF16), wkv_ref[...], preferred_element_type=F32)
    k_pe = _rope(k_rope, ca, sa).astype(BF16)
    sc_a = np.float32((QK_NOPE + QK_ROPE) ** -0.5 * LOG2E)
    for hd in range(MLA_HEADS):
        qh = q[:, hd * MLA_DK:(hd + 1) * MLA_DK]
        qa_ref[0, hd, 0, 0:LANES, :] = (qh[:, 0:LANES] * sc_a).T.astype(BF16)
        qa_ref[0, hd, 0, LANES:MLA_DK, :] = (_rope(qh[:, LANES:MLA_DK], ca, sa) * sc_a).T.astype(BF16)
        kvh = kv[:, hd * 2 * LANES:(hd + 1) * 2 * LANES]
        ka_ref[0, hd, :, 0:LANES] = kvh[:, 0:LANES].astype(BF16)
        ka_ref[0, hd, :, LANES:MLA_DK] = k_pe
        va_ref[0, hd, 0] = kvh[:, LANES:2 * LANES].T.astype(BF16)

    sc_b = np.float32(HEAD_DIM ** -0.5 * LOG2E)
    gqn, gkn = gqn_ref[...], gkn_ref[...]
    for hd in range(GQA_HEADS):
        qh = _rope(_rms(gq[:, hd * HEAD_DIM:(hd + 1) * HEAD_DIM], gqn), cb, sb)
        qb_ref[0, hd, 0] = (qh * sc_b).T.astype(BF16)
    for hd in range(GQA_KV_HEADS):
        kh = _rope(_rms(gk[:, hd * HEAD_DIM:(hd + 1) * HEAD_DIM], gkn), cb, sb)
        kb_ref[0, hd] = kh.astype(BF16)
        vb_ref[0, hd, 0] = gv[:, hd * HEAD_DIM:(hd + 1) * HEAD_DIM].T.astype(BF16)


def _pre_call(x, gattn, win, gqa, wq, gkv, wkv, gqn, gkn, ca, sa, cb, sb, tm):
    b, s, d = x.shape
    const = lambda shape: pl.BlockSpec(shape, lambda bi, si: (0,) * len(shape))
    tab = pl.BlockSpec((tm, LANES), lambda bi, si: (si, 0))
    head = lambda nh, w: pl.BlockSpec((1, nh, tm, w), lambda bi, si: (bi, 0, si, 0))
    head_t = lambda nh, w: pl.BlockSpec((1, nh, 1, w, tm), lambda bi, si: (bi, 0, si, 0, 0))
    return pl.pallas_call(
        _pre_kernel,
        grid=(b, s // tm),
        in_specs=[pl.BlockSpec((1, tm, d), lambda bi, si: (bi, si, 0)),
                  const(gattn.shape), const(win.shape), const(gqa.shape), const(wq.shape),
                  const(gkv.shape), const(wkv.shape), const(gqn.shape), const(gkn.shape),
                  tab, tab, tab, tab],
        out_specs=[head_t(MLA_HEADS, MLA_DK), head(MLA_HEADS, MLA_DK), head_t(MLA_HEADS, V_HEAD),
                   head_t(GQA_HEADS, HEAD_DIM), head(GQA_KV_HEADS, HEAD_DIM), head_t(GQA_KV_HEADS, HEAD_DIM)],
        out_shape=[jax.ShapeDtypeStruct((b, MLA_HEADS, s // tm, MLA_DK, tm), BF16),
                   jax.ShapeDtypeStruct((b, MLA_HEADS, s, MLA_DK), BF16),
                   jax.ShapeDtypeStruct((b, MLA_HEADS, s // tm, V_HEAD, tm), BF16),
                   jax.ShapeDtypeStruct((b, GQA_HEADS, s // tm, HEAD_DIM, tm), BF16),
                   jax.ShapeDtypeStruct((b, GQA_KV_HEADS, s, HEAD_DIM), BF16),
                   jax.ShapeDtypeStruct((b, GQA_KV_HEADS, s // tm, HEAD_DIM, tm), BF16)],
        compiler_params=_cparams(("parallel", "parallel")),
        name="pre_proj",
    )(x, gattn, win, gqa, wq, gkv, wkv, gqn, gkn, ca, sa, cb, sb)


def _flash_kernel(q_ref, k_ref, vt_ref, o_ref, acc_sc, s_sc, *, streams, tq, tk):
    s_len = k_ref.shape[2]
    dv, cw = vt_ref.shape[3], vt_ref.shape[4]
    qw = q_ref.shape[4]
    n_qc = tq // qw
    n_sub = tk // cw
    n_steps = s_len // tk
    unroll = 8 if n_steps % 8 == 0 else 2
    n_trips = n_steps // unroll
    ones = jnp.ones((BF16_ROWS, cw), BF16)

    def scores(r, c0, j, slot):
        start = j * tk if isinstance(j, int) else pl.multiple_of(j * tk, tk)
        k = k_ref[0, 0, pl.ds(start, tk), :]
        st = jnp.concatenate([jnp.dot(k, q_ref[0, r, c0 + c], preferred_element_type=F32)
                              for c in range(n_qc)], axis=1)
        s_sc[slot] = st
        return jnp.max(st, axis=0, keepdims=True)

    def key_step(j, slot, m_prev, mx, next_scores):
        mx_next = next_scores() if next_scores is not None else mx
        m_new = jnp.maximum(m_prev, mx)
        alpha = jnp.exp2(m_prev - m_new)
        pb = jnp.exp2(s_sc[slot] - m_new).astype(BF16)
        pv = None
        for c in range(n_sub):
            vt1 = jnp.concatenate([vt_ref[0, 0, j * n_sub + c], ones], axis=0)
            d = jnp.dot(vt1, pb[c * cw:(c + 1) * cw], preferred_element_type=F32)
            pv = d if pv is None else pv + d
        acc_sc[...] = alpha * acc_sc[...] + pv
        return m_new, mx_next

    mx = scores(streams[0][0], streams[0][1], 0, 0)
    for si, stream in enumerate(streams):
        r, c0, row, col = stream
        r_nx, c0_nx, j_nx = (streams[si + 1][0], streams[si + 1][1], 0) if si + 1 < len(streams) \
            else (r, c0, n_steps - 1)
        acc_sc[...] = jnp.zeros(acc_sc.shape, F32)

        def body(jj, carry, r=r, c0=c0, r_nx=r_nx, c0_nx=c0_nx, j_nx=j_nx):
            m_prev, mx = carry
            for u in range(unroll):
                j = jj * unroll + u
                if u + 1 < unroll:
                    nxt = functools.partial(scores, r, c0, j + 1, 1 - u % 2)
                else:
                    last = jj == n_trips - 1
                    nxt = functools.partial(scores, jnp.where(last, r_nx, r), jnp.where(last, c0_nx, c0),
                                            jnp.where(last, j_nx, j + 1), 1 - u % 2)
                m_prev, mx = key_step(j, u % 2, m_prev, mx, nxt)
            return m_prev, mx

        _, mx = lax.fori_loop(0, n_trips, body, (jnp.full((1, tq), -jnp.inf, F32), mx))
        out_t = acc_sc[0:dv, :] / acc_sc[dv:dv + 1, :]
        o_ref[0, row:row + tq, col:col + dv] = out_t.T.astype(o_ref.dtype)


def _flash_call(qt, k, vt, tq, n_tiles, tk, name):
    b, hq, _, dk, qw = qt.shape
    s = k.shape[2]
    hkv = k.shape[1]
    _, _, nchunk, dv, cw = vt.shape
    rep = hq // hkv
    tq_blk = tq * n_tiles
    assert tk % cw == 0 and (s // tk) % 2 == 0 and s % tq_blk == 0 and tq % qw == 0
    streams = tuple((r, ti * (tq // qw), ti * tq, r * dv) for r in range(rep) for ti in range(n_tiles))
    return pl.pallas_call(
        functools.partial(_flash_kernel, streams=streams, tq=tq, tk=tk),
        grid=(b, hkv, s // tq_blk),
        in_specs=[pl.BlockSpec((1, rep, tq_blk // qw, dk, qw), lambda bi, g, qi: (bi, g, qi, 0, 0)),
                  pl.BlockSpec((1, 1, s, dk), lambda bi, g, qi: (bi, g, 0, 0)),
                  pl.BlockSpec((1, 1, nchunk, dv, cw), lambda bi, g, qi: (bi, g, 0, 0, 0))],
        out_specs=pl.BlockSpec((1, tq_blk, rep * dv), lambda bi, g, qi: (bi, qi, g)),
        out_shape=jax.ShapeDtypeStruct((b, s, hq * dv), BF16),
        scratch_shapes=[pltpu.VMEM((dv + BF16_ROWS, tq), F32), pltpu.VMEM((2, tk, tq), F32)],
        compiler_params=_cparams(("parallel", "parallel", "arbitrary")),
        name=name,
    )(qt, k, vt)


def _post_kernel(*refs, bounds):
    n_set = len(bounds) - 1
    ins, (wout_ref, gffn_ref, wr_ref, br_ref, x1_ref, hp_ref, route_ref) = refs[:3 * n_set], refs[3 * n_set:]
    i = pl.program_id(0)
    for si in range(n_set):
        @pl.when((i >= bounds[si]) & (i < bounds[si + 1]))
        def _(si=si):
            _post_tile(*ins[3 * si:3 * si + 3], wout_ref, gffn_ref, wr_ref, br_ref, x1_ref, hp_ref, route_ref)


def _post_tile(oa_ref, ob_ref, x_ref, wout_ref, gffn_ref, wr_ref, br_ref, x1_ref, hp_ref, route_ref):
    half = oa_ref.shape[-1]
    mix = jnp.dot(oa_ref[...], wout_ref[0:half, :], preferred_element_type=F32)
    mix = mix + jnp.dot(ob_ref[...], wout_ref[half:2 * half, :], preferred_element_type=F32)
    x1 = x_ref[...] + mix
    x1_ref[...] = x1
    h = _rms(x1, gffn_ref[...])

    h_hi = h.astype(BF16)
    h_hi32 = h_hi.astype(F32)
    hp_ref[...] = _pack_bf16_pairs(h)

    h_lo = (h - h_hi32).astype(BF16)
    lg2 = (jnp.dot(h_hi, wr_ref[...], preferred_element_type=F32)
           + jnp.dot(h_lo, wr_ref[...], preferred_element_type=F32))
    lg = lg2[:, 0:LANES] + lg2[:, LANES:2 * LANES] + br_ref[...]
    lane = lax.broadcasted_iota(jnp.int32, lg.shape, 1)
    neg = jnp.float32(-jnp.inf)
    big = jnp.int32(4 * LANES)
    gl = jnp.where(lane < N_GROUPS, lg, neg)
    gmax = jnp.max(gl, axis=-1, keepdims=True)
    gsum = jnp.sum(jnp.exp(gl - gmax), axis=-1, keepdims=True)
    g_w = 1.0 / gsum
    g_idx = jnp.min(jnp.where(gl == gmax, lane, big), axis=-1, keepdims=True)
    lo_lane = N_GROUPS + EXPERTS_PER_GROUP * g_idx
    el = jnp.where((lane >= lo_lane) & (lane < lo_lane + EXPERTS_PER_GROUP), lg, neg)
    m1 = jnp.max(el, axis=-1, keepdims=True)
    i1 = jnp.min(jnp.where(el == m1, lane, big), axis=-1, keepdims=True)
    el2 = jnp.where(lane == i1, neg, el)
    m2 = jnp.max(el2, axis=-1, keepdims=True)
    i2 = jnp.min(jnp.where(el2 == m2, lane, big), axis=-1, keepdims=True)
    e2 = jnp.exp(m2 - m1)
    w1 = 1.0 / (1.0 + e2)
    w2 = e2 / (1.0 + e2)
    out = jnp.where(lane == 0, (i1 - N_GROUPS).astype(F32),
          jnp.where(lane == 1, (i2 - N_GROUPS).astype(F32),
          jnp.where(lane == 2, g_w * w1,
          jnp.where(lane == 3, g_w * w2, 0.0))))
    route_ref[...] = out


def _post_call(sets, wout, gffn, wr, br, tm):
    d = sets[0][2].shape[1]
    half = sets[0][0].shape[1]
    n_tiles = [x.shape[0] // tm for _, _, x in sets]
    bounds = tuple(int(v) for v in np.concatenate([[0], np.cumsum(n_tiles)]))
    t = bounds[-1] * tm
    const = lambda shape: pl.BlockSpec(shape, lambda i: (0,) * len(shape))
    row = lambda w: pl.BlockSpec((tm, w), lambda i: (i, 0))
    in_specs, operands = [], []
    for si, (oa, ob, x) in enumerate(sets):
        pin = lambda w, lo=bounds[si], n=n_tiles[si]: pl.BlockSpec((tm, w), lambda i: (jnp.clip(i - lo, 0, n - 1), 0))
        in_specs += [pin(half), pin(half), pin(d)]
        operands += [oa, ob, x]
    return pl.pallas_call(
        functools.partial(_post_kernel, bounds=bounds),
        grid=(bounds[-1],),
        in_specs=in_specs + [const(wout.shape), const(gffn.shape), const(wr.shape), const(br.shape)],
        out_specs=[row(d), row(d // 2), row(LANES)],
        out_shape=[jax.ShapeDtypeStruct((t, d), F32),
                   jax.ShapeDtypeStruct((t, d // 2), jnp.uint32),
                   jax.ShapeDtypeStruct((t, LANES), F32)],
        compiler_params=_cparams(("parallel",)),
        name="post_proj_router",
    )(*operands, wout, gffn, wr, br)


def _row_gather(src_hbm, idx_ref, buf, sem, n_rows):
    def body(r, carry):
        pltpu.make_async_copy(src_hbm.at[pl.ds(idx_ref[0, 0, r], 1)], buf.at[pl.ds(r, 1)], sem).start()
        return carry
    lax.fori_loop(0, n_rows, body, 0, unroll=GATHER_UNROLL)


def _moe_kernel(blk_e_ref, n_used_ref, tok_cur_ref, tok_next_ref, hp_hbm, wg_ref, wu_ref, wd_ref,
                out_ref, buf0, buf1, sem, wgb, wub, wdb):
    i = pl.program_id(0)
    tb, dh = buf0.shape
    n_used = n_used_ref[0]

    @pl.when((i == 0) | (blk_e_ref[i] != blk_e_ref[jnp.maximum(i - 1, 0)]))
    def _():
        wgb[...] = wg_ref[0].astype(BF16)
        wub[...] = wu_ref[0].astype(BF16)
        wdb[...] = wd_ref[0].astype(BF16)

    @pl.when(i == 0)
    def _():
        _row_gather(hp_hbm, tok_cur_ref, buf0, sem.at[0], tb)

    def wait_rows(buf, s_):
        pltpu.make_async_copy(hp_hbm.at[pl.ds(0, tb)], buf, sem.at[s_]).wait()

    def block(cur, cur_s, nxt, nxt_s):
        wait_rows(cur, cur_s)
        for r in range(tb):
            pltpu.make_async_copy(hp_hbm.at[pl.ds(tok_next_ref[0, 0, r], 1)], nxt.at[pl.ds(r, 1)],
                                  sem.at[nxt_s]).start()
        x_lo, x_hi = (v.astype(BF16) for v in _unpack_bf16_pairs(cur[...]))
        g = jnp.dot(x_lo, wgb[0:dh, :], preferred_element_type=F32)
        g = g + jnp.dot(x_hi, wgb[dh:2 * dh, :], preferred_element_type=F32)
        u = jnp.dot(x_lo, wub[0:dh, :], preferred_element_type=F32)
        u = u + jnp.dot(x_hi, wub[dh:2 * dh, :], preferred_element_type=F32)
        hid = (g * jax.nn.sigmoid(g)) * u
        out = jnp.dot(hid.astype(BF16), wdb[...], preferred_element_type=F32)
        out_ref[...] = _pack_bf16_pairs(out)

    used = i < n_used
    even = i % 2 == 0

    @pl.when(used & even)
    def _():
        block(buf0, 0, buf1, 1)

    @pl.when(used & jnp.logical_not(even))
    def _():
        block(buf1, 1, buf0, 0)

    @pl.when((i == n_used) & even)
    def _():
        wait_rows(buf0, 0)

    @pl.when((i == n_used) & jnp.logical_not(even))
    def _():
        wait_rows(buf1, 1)

    @pl.when(i >= n_used)
    def _():
        out_ref[...] = jnp.zeros(out_ref.shape, out_ref.dtype)


def _moe_call(blk_e, n_used, pad_tok, hp, wg, wu, wd, tb):
    nb = pad_tok.shape[0]
    dh = hp.shape[-1]
    d = 2 * dh
    de = wg.shape[-1]
    smem_blk = lambda f: pl.BlockSpec((1, 1, tb), f, memory_space=pltpu.SMEM)
    wspec = lambda shape: pl.BlockSpec(shape, lambda i, be, nu: (be[i], 0, 0))
    grid_spec = pltpu.PrefetchScalarGridSpec(
        num_scalar_prefetch=2,
        grid=(nb,),
        in_specs=[smem_blk(lambda i, be, nu: (i, 0, 0)),
                  smem_blk(lambda i, be, nu: (jnp.minimum(i + 1, nb - 1), 0, 0)),
                  pl.BlockSpec(memory_space=pl.ANY),
                  wspec((1, d, de)), wspec((1, d, de)), wspec((1, de, d))],
        out_specs=pl.BlockSpec((tb, dh), lambda i, be, nu: (i, 0)),
        scratch_shapes=[pltpu.VMEM((tb, dh), jnp.uint32), pltpu.VMEM((tb, dh), jnp.uint32),
                        pltpu.SemaphoreType.DMA((2,)),
                        pltpu.VMEM((d, de), BF16), pltpu.VMEM((d, de), BF16), pltpu.VMEM((de, d), BF16)],
    )
    return pl.pallas_call(
        _moe_kernel,
        grid_spec=grid_spec,
        out_shape=jax.ShapeDtypeStruct((nb * tb, dh), jnp.uint32),
        compiler_params=_cparams(("arbitrary",)),
        name="moe_experts",
    )(blk_e, n_used, pad_tok, pad_tok, hp, wg, wu, wd)


def _final_kernel(*refs, bounds):
    n_set = len(bounds) - 1
    dest_cur_ref, dest_next_ref, x1_ref, route_ref, rows_hbm, gfin_ref = refs[:6]
    y_refs = refs[6:6 + n_set]
    buf0, buf1, sem = refs[6 + n_set:]
    i = pl.program_id(0)
    last = i == pl.num_programs(0) - 1
    n_rows = buf0.shape[0]
    tm = n_rows // TOP_K

    @pl.when(i == 0)
    def _():
        _row_gather(rows_hbm, dest_cur_ref, buf0, sem.at[0], n_rows)

    def wait_rows(buf, s_):
        pltpu.make_async_copy(rows_hbm.at[pl.ds(0, n_rows)], buf, sem.at[s_]).wait()

    def block(cur, cur_s, nxt, nxt_s):
        wait_rows(cur, cur_s)
        for r in range(n_rows):
            pltpu.make_async_copy(rows_hbm.at[pl.ds(dest_next_ref[0, 0, r], 1)], nxt.at[pl.ds(r, 1)],
                                  sem.at[nxt_s]).start()
        route = route_ref[...]
        g0, g1 = route[:, 2:3], route[:, 3:4]
        a_lo, a_hi = _unpack_bf16_pairs(cur[0:tm, :])
        b_lo, b_hi = _unpack_bf16_pairs(cur[tm:2 * tm, :])
        moe = jnp.concatenate([a_lo * g0 + b_lo * g1, a_hi * g0 + b_hi * g1], axis=1)
        y = _rms(x1_ref[...] + moe, gfin_ref[...])
        for si in range(n_set):
            @pl.when((i >= bounds[si]) & (i < bounds[si + 1]))
            def _(si=si):
                y_refs[si][...] = y

    even = i % 2 == 0

    @pl.when(even)
    def _():
        block(buf0, 0, buf1, 1)

    @pl.when(jnp.logical_not(even))
    def _():
        block(buf1, 1, buf0, 0)

    @pl.when(last & even)
    def _():
        wait_rows(buf1, 1)

    @pl.when(last & jnp.logical_not(even))
    def _():
        wait_rows(buf0, 0)


def _final_call(dest, x1, route, rows, gfin, tm, n_tiles):
    t, d = x1.shape
    nt = t // tm
    bounds = tuple(int(v) for v in np.concatenate([[0], np.cumsum(n_tiles)]))
    assert bounds[-1] == nt
    smem_blk = lambda f: pl.BlockSpec((1, 1, TOP_K * tm), f, memory_space=pltpu.SMEM)
    out_specs = [pl.BlockSpec((tm, d), lambda i, lo=bounds[si], n=n_tiles[si]: (jnp.clip(i - lo, 0, n - 1), 0))
                 for si in range(len(n_tiles))]
    return pl.pallas_call(
        functools.partial(_final_kernel, bounds=bounds),
        grid=(nt,),
        in_specs=[smem_blk(lambda i: (i, 0, 0)),
                  smem_blk(lambda i: (jnp.minimum(i + 1, nt - 1), 0, 0)),
                  pl.BlockSpec((tm, d), lambda i: (i, 0)),
                  pl.BlockSpec((tm, LANES), lambda i: (i, 0)),
                  pl.BlockSpec(memory_space=pl.ANY),
                  pl.BlockSpec(gfin.shape, lambda i: (0, 0))],
        out_specs=out_specs,
        out_shape=[jax.ShapeDtypeStruct((n * tm, d), F32) for n in n_tiles],
        scratch_shapes=[pltpu.VMEM((TOP_K * tm, d // 2), jnp.uint32), pltpu.VMEM((TOP_K * tm, d // 2), jnp.uint32),
                        pltpu.SemaphoreType.DMA((2,))],
        compiler_params=_cparams(("arbitrary",)),
        name="combine_final_norm",
    )(dest, dest, x1, route, rows, gfin)


def _deinterleave(n):
    return np.concatenate([np.arange(0, n, 2), np.arange(1, n, 2)])


def _rope_pad_cols(base):
    pad = -np.ones(LANES // 4, np.int64)
    return np.concatenate([base + np.arange(0, QK_ROPE, 2), pad, base + np.arange(1, QK_ROPE, 2), pad])


def _take_cols(w, cols):
    w_ext = jnp.concatenate([w, jnp.zeros((w.shape[0], 1), w.dtype)], axis=1)
    return w_ext[:, np.where(cols < 0, w.shape[1], cols)]


def _prep_weights(w_in, w_q_up, q_norm, k_norm):
    o_kr = Q_LORA + KV_LORA
    o_gq = o_kr + QK_ROPE
    o_gk = o_gq + GQA_HEADS * HEAD_DIM
    o_gv = o_gk + GQA_KV_HEADS * HEAD_DIM
    cols = [np.arange(0, o_kr), _rope_pad_cols(o_kr)]
    for hd in range(GQA_HEADS):
        cols.append(o_gq + hd * HEAD_DIM + _deinterleave(HEAD_DIM))
    for hd in range(GQA_KV_HEADS):
        cols.append(o_gk + hd * HEAD_DIM + _deinterleave(HEAD_DIM))
    cols.append(np.arange(o_gv, o_gv + GQA_KV_HEADS * HEAD_DIM))
    win = _take_cols(w_in, np.concatenate(cols)).astype(BF16)

    qcols = []
    for hd in range(MLA_HEADS):
        base = hd * (QK_NOPE + QK_ROPE)
        qcols += [base + np.arange(QK_NOPE), _rope_pad_cols(base + QK_NOPE)]
    wq = _take_cols(w_q_up, np.concatenate(qcols)).astype(BF16)
    perm = _deinterleave(HEAD_DIM)
    return win, wq, q_norm[perm][None, :], k_norm[perm][None, :]


def _rope_tables(s):
    rows = s // GRID_W
    row = np.repeat(np.arange(rows, dtype=np.float64), GRID_W)
    col = np.tile(np.arange(GRID_W, dtype=np.float64), rows)

    def cos_sin(rot_dim):
        n_pairs = rot_dim // 4
        freqs = ROPE_THETA ** (-np.arange(n_pairs, dtype=np.float64) * 2.0 / (rot_dim // 2))
        ang = np.concatenate([row[:, None] * freqs[None, :], col[:, None] * freqs[None, :]], axis=-1)
        return np.cos(ang), np.sin(ang)

    ca, sa = cos_sin(QK_ROPE)
    cb, sb = cos_sin(HEAD_DIM)
    tabs = (np.concatenate([ca, ca, ca, ca], -1), np.concatenate([-sa, -sa, sa, sa], -1),
            np.concatenate([cb, cb], -1), np.concatenate([-sb, sb], -1))
    return tuple(jnp.asarray(t.astype(np.float32)) for t in tabs)


def _pick(n, pref):
    return pref if n % pref == 0 else n


def _route_plan(route, tb):
    t = route.shape[0]
    a = t * TOP_K
    expert_id = route[:, 0:TOP_K].astype(jnp.int32).reshape(a)
    onehot = (expert_id[:, None] == jnp.arange(N_EXPERTS, dtype=jnp.int32)[None, :]).astype(jnp.int32)
    csum = jnp.cumsum(onehot, axis=0)
    counts = csum[-1]
    rank = jnp.sum(onehot * csum, axis=1) - 1
    padded = ((counts + tb - 1) // tb) * tb
    pad_ends = jnp.cumsum(padded)
    pad_off = pad_ends - padded
    dest = pad_off[expert_id] + rank
    nb = (a + N_EXPERTS * tb) // tb
    blk_start = jnp.arange(nb, dtype=jnp.int32) * tb
    blk_e = jnp.minimum(jnp.sum(blk_start[:, None] >= pad_ends[None, :], axis=1), N_EXPERTS - 1)
    n_used = (pad_ends[-1] // tb).astype(jnp.int32).reshape(1)
    order = jnp.sort(expert_id * a + jnp.arange(a, dtype=jnp.int32)) % a
    row_e = jnp.repeat(blk_e, tb)
    row_rank = jnp.arange(nb * tb, dtype=jnp.int32) - pad_off[row_e]
    offsets = jnp.cumsum(counts) - counts
    src = jnp.clip(offsets[row_e] + row_rank, 0, a - 1)
    valid = (row_rank < counts[row_e]) & (jnp.arange(nb * tb, dtype=jnp.int32) < pad_ends[-1])
    pad_tok = jnp.where(valid, order[src] // TOP_K, 0)
    last_e = blk_e[jnp.maximum(n_used[0] - 1, 0)]
    blk_e = jnp.where(jnp.arange(nb) < n_used[0], blk_e, last_e).astype(jnp.int32)
    return pad_tok.reshape(nb, 1, tb), dest.reshape(t, TOP_K), blk_e, n_used


def _attention(x, p, tm):
    b, s, d = x.shape
    t = b * s
    qa, ka, va, qb, kb, vb = _pre_call(x, p["gattn"], p["win"], p["gqa"], p["wq"], p["gkv"], p["wkv"],
                                       p["gqn"], p["gkn"], *_rope_tables(s), tm)
    tk = 2 * tm if s % (4 * tm) == 0 else s // 2
    oa = _flash_call(qa, ka, va, _pick(s, 1024), 2 if s % 2048 == 0 else 1, tk, "mla_flash")
    ob = _flash_call(qb, kb, vb, _pick(s, 512), 1, tk, "gqa_flash")
    return oa.reshape(t, -1), ob.reshape(t, -1), x.reshape(t, d)


def _channel_mix(sets, p, tm):
    x1, hp, route = _post_call(sets, p["wout"], p["gffn"], p["wr"], p["br"], tm)
    t = x1.shape[0]
    tb = 256
    pad_tok, dest, blk_e, n_used = _route_plan(route, tb)
    rows = _moe_call(blk_e, n_used, pad_tok, hp, p["wg"], p["wu"], p["wd"], tb)
    nt = t // tm
    dest_blk = dest.reshape(nt, tm, TOP_K).transpose(0, 2, 1).reshape(nt, 1, TOP_K * tm)
    return _final_call(dest_blk, x1, route, rows, p["gfin"], tm, [x.shape[0] // tm for _, _, x in sets])


def kernel(x_prompt, x_sample, attn_norm, w_in, q_a_norm, w_q_up, kv_a_norm, w_kv_up, q_norm, k_norm, w_out,
           ffn_norm, w_group, b_group, w_expert, b_expert, w_gate, w_up, w_down, final_norm):
    assert attn_norm.shape[0] == 1, "single-layer trunk"
    win, wq, gqn, gkn = _prep_weights(w_in[0], w_q_up[0], q_norm[0], k_norm[0])
    d = w_in.shape[1]
    wr = jnp.zeros((d, LANES), F32).at[:, 0:N_GROUPS].set(w_group[0])
    wr = wr.at[:, N_GROUPS:N_GROUPS + N_EXPERTS].set(w_expert[0])
    br = jnp.zeros((1, LANES), F32).at[0, 0:N_GROUPS].set(b_group[0])
    br = br.at[0, N_GROUPS:N_GROUPS + N_EXPERTS].set(b_expert[0])
    wr_hi = wr.astype(BF16)
    wr = jnp.concatenate([wr_hi, (wr - wr_hi.astype(F32)).astype(BF16)], axis=1)
    p = dict(gattn=attn_norm, win=win, gqa=q_a_norm, wq=wq, gkv=kv_a_norm, wkv=w_kv_up[0].astype(BF16),
             gqn=gqn, gkn=gkn, wout=w_out[0].astype(BF16), gffn=ffn_norm, wr=wr, br=br,
             wg=w_gate[0], wu=w_up[0], wd=w_down[0],
             gfin=final_norm[None, :])
    xs = (x_prompt, x_sample)
    tm = 256
    assert all(x.shape[1] % tm == 0 for x in xs)
    ys = _channel_mix([_attention(x, p, tm) for x in xs], p, tm)
    return tuple(y.reshape(x.shape) for y, x in zip(ys, xs))
```

```python
import functools
import math

import numpy as np
import jax
import jax.numpy as jnp
from jax import lax
from jax.experimental import pallas as pl
from jax.experimental.pallas import tpu as pltpu

F32 = jnp.float32
BF16 = jnp.bfloat16

EPS = 1e-6
ROPE_THETA = 10000.0
GRID_W = 64
LANES = 128
BF16_ROWS = 16
GATHER_UNROLL = 8
POST_SUB = 256

MLA_HEADS = 8
Q_LORA = 512
KV_LORA = 256
QK_NOPE = 128
QK_ROPE = 64
V_HEAD = 128
MLA_DK = 256
GQA_HEADS = 8
GQA_KV_HEADS = 2
GQA_REP = GQA_HEADS // GQA_KV_HEADS
HEAD_DIM = 128
N_GROUPS = 4
EXPERTS_PER_GROUP = 8
N_EXPERTS = N_GROUPS * EXPERTS_PER_GROUP
TOP_K = 2
LOG2E = math.log2(math.e)

VMEM_LIMIT = 56 * 1024 * 1024


def _cparams(sem):
    return pltpu.CompilerParams(dimension_semantics=sem, vmem_limit_bytes=VMEM_LIMIT)


def _rms(x, gain):
    return x * lax.rsqrt(jnp.mean(x * x, axis=-1, keepdims=True) + EPS) * gain


def _rope(x, cos_t, sin_t):
    return x * cos_t + pltpu.roll(x, LANES // 2, axis=1) * sin_t


def _pack_bf16_pairs(v):
    n = v.shape[-1] // 2
    w = pltpu.bitcast(v.astype(BF16).astype(F32), jnp.uint32)
    return w[:, n:2 * n] | (w[:, 0:n] >> 16)


def _unpack_bf16_pairs(w):
    return pltpu.bitcast(w << 16, F32), pltpu.bitcast(w & jnp.uint32(0xFFFF0000), F32)


def _pre_kernel(x_ref, gattn_ref, win_ref, gqa_ref, wq_ref, gkv_ref, wkv_ref, gqn_ref, gkn_ref,
                ca_ref, sa_ref, cb_ref, sb_ref,
                qa_ref, ka_ref, va_ref, qb_ref, kb_ref, vb_ref):
    x = x_ref[0]
    h = _rms(x, gattn_ref[...]).astype(BF16)
    proj = jnp.dot(h, win_ref[...], preferred_element_type=F32)
    o = 0
    q_lat = proj[:, o:o + Q_LORA]; o += Q_LORA
    kv_lat = proj[:, o:o + KV_LORA]; o += KV_LORA
    k_rope = proj[:, o:o + LANES]; o += LANES
    gq = proj[:, o:o + GQA_HEADS * HEAD_DIM]; o += GQA_HEADS * HEAD_DIM
    gk = proj[:, o:o + GQA_KV_HEADS * HEAD_DIM]; o += GQA_KV_HEADS * HEAD_DIM
    gv = proj[:, o:o + GQA_KV_HEADS * HEAD_DIM]

    ca, sa, cb, sb = ca_ref[...], sa_ref[...], cb_ref[...], sb_ref[...]

    q = jnp.dot(_rms(q_lat, gqa_ref[...]).astype(BF16), wq_ref[...], preferred_element_type=F32)
    kv = jnp.dot(_rms(kv_lat, gkv_ref[...]).astype(BF16), wkv_ref[...], preferred_element_type=F32)
    k_pe = _rope(k_rope, ca, sa).astype(BF16)
    sc_a = np.float32((QK_NOPE + QK_ROPE) ** -0.5 * LOG2E)
    for hd in range(MLA_HEADS):
        qh = q[:, hd * MLA_DK:(hd + 1) * MLA_DK]
        qa_ref[0, hd, 0, 0:LANES, :] = (qh[:, 0:LANES] * sc_a).T.astype(BF16)
        qa_ref[0, hd, 0, LANES:MLA_DK, :] = (_rope(qh[:, LANES:MLA_DK], ca, sa) * sc_a).T.astype(BF16)
        kvh = kv[:, hd * 2 * LANES:(hd + 1) * 2 * LANES]
        ka_ref[0, hd, :, 0:LANES] = kvh[:, 0:LANES].astype(BF16)
        ka_ref[0, hd, :, LANES:MLA_DK] = k_pe
        va_ref[0, hd, 0] = kvh[:, LANES:2 * LANES].T.astype(BF16)

    sc_b = np.float32(HEAD_DIM ** -0.5 * LOG2E)
    gqn, gkn = gqn_ref[...], gkn_ref[...]
    for hd in range(GQA_HEADS):
        qh = _rope(_rms(gq[:, hd * HEAD_DIM:(hd + 1) * HEAD_DIM], gqn), cb, sb)
        qb_ref[0, hd, 0] = (qh * sc_b).T.astype(BF16)
    for hd in range(GQA_KV_HEADS):
        kh = _rope(_rms(gk[:, hd * HEAD_DIM:(hd + 1) * HEAD_DIM], gkn), cb, sb)
        kb_ref[0, hd] = kh.astype(BF16)
        vb_ref[0, hd, 0] = gv[:, hd * HEAD_DIM:(hd + 1) * HEAD_DIM].T.astype(BF16)


def _pre_call(x, gattn, win, gqa, wq, gkv, wkv, gqn, gkn, ca, sa, cb, sb, tm):
    b, s, d = x.shape
    const = lambda shape: pl.BlockSpec(shape, lambda bi, si: (0,) * len(shape))
    tab = pl.BlockSpec((tm, LANES), lambda bi, si: (si, 0))
    head = lambda nh, w: pl.BlockSpec((1, nh, tm, w), lambda bi, si: (bi, 0, si, 0))
    head_t = lambda nh, w: pl.BlockSpec((1, nh, 1, w, tm), lambda bi, si: (bi, 0, si, 0, 0))
    return pl.pallas_call(
        _pre_kernel,
        grid=(b, s // tm),
        in_specs=[pl.BlockSpec((1, tm, d), lambda bi, si: (bi, si, 0)),
                  const(gattn.shape), const(win.shape), const(gqa.shape), const(wq.shape),
                  const(gkv.shape), const(wkv.shape), const(gqn.shape), const(gkn.shape),
                  tab, tab, tab, tab],
        out_specs=[head_t(MLA_HEADS, MLA_DK), head(MLA_HEADS, MLA_DK), head_t(MLA_HEADS, V_HEAD),
                   head_t(GQA_HEADS, HEAD_DIM), head(GQA_KV_HEADS, HEAD_DIM), head_t(GQA_KV_HEADS, HEAD_DIM)],
        out_shape=[jax.ShapeDtypeStruct((b, MLA_HEADS, s // tm, MLA_DK, tm), BF16),
                   jax.ShapeDtypeStruct((b, MLA_HEADS, s, MLA_DK), BF16),
                   jax.ShapeDtypeStruct((b, MLA_HEADS, s // tm, V_HEAD, tm), BF16),
                   jax.ShapeDtypeStruct((b, GQA_HEADS, s // tm, HEAD_DIM, tm), BF16),
                   jax.ShapeDtypeStruct((b, GQA_KV_HEADS, s, HEAD_DIM), BF16),
                   jax.ShapeDtypeStruct((b, GQA_KV_HEADS, s // tm, HEAD_DIM, tm), BF16)],
        compiler_params=_cparams(("parallel", "parallel")),
        name="pre_proj",
    )(x, gattn, win, gqa, wq, gkv, wkv, gqn, gkn, ca, sa, cb, sb)


def _flash_kernel(q_ref, k_ref, vt_ref, o_ref, acc_sc, s_sc, *, streams, tq, tk):
    s_len = k_ref.shape[2]
    dv, cw = vt_ref.shape[3], vt_ref.shape[4]
    qw = q_ref.shape[4]
    n_qc = tq // qw
    n_sub = tk // cw
    n_steps = s_len // tk
    unroll = 8 if n_steps % 8 == 0 else 2
    n_trips = n_steps // unroll

    def scores(r, c0, j, slot):
        start = j * tk if isinstance(j, int) else pl.multiple_of(j * tk, tk)
        k = k_ref[0, 0, pl.ds(start, tk), :]
        st = jnp.concatenate([jnp.dot(k, q_ref[0, r, c0 + c], preferred_element_type=F32)
                              for c in range(n_qc)], axis=1)
        s_sc[slot] = st
        return jnp.max(st, axis=0, keepdims=True)

    ones = jnp.ones((BF16_ROWS, cw), BF16)

    def key_step(j, slot, m_prev, mx, next_scores):
        mx_next = next_scores()
        m_new = jnp.maximum(m_prev, mx)
        alpha = jnp.exp2(m_prev - m_new)
        pb = jnp.exp2(s_sc[slot] - m_new).astype(BF16)
        pv = None
        for c in range(n_sub):
            vt1 = jnp.concatenate([vt_ref[0, 0, j * n_sub + c], ones], axis=0)
            d = jnp.dot(vt1, pb[c * cw:(c + 1) * cw], preferred_element_type=F32)
            pv = d if pv is None else pv + d
        acc_sc[...] = alpha * acc_sc[...] + pv
        return m_new, mx_next

    mx = scores(streams[0][0], streams[0][1], 0, 0)
    for si, stream in enumerate(streams):
        r, c0, row, col = stream
        r_nx, c0_nx, j_nx = (streams[si + 1][0], streams[si + 1][1], 0) if si + 1 < len(streams) \
            else (r, c0, n_steps - 1)
        acc_sc[...] = jnp.zeros(acc_sc.shape, F32)

        def body(jj, carry, r=r, c0=c0, r_nx=r_nx, c0_nx=c0_nx, j_nx=j_nx):
            m_prev, mx = carry
            for u in range(unroll):
                j = jj * unroll + u
                if u + 1 < unroll:
                    nxt = functools.partial(scores, r, c0, j + 1, 1 - u % 2)
                else:
                    last = jj == n_trips - 1
                    nxt = functools.partial(scores, jnp.where(last, r_nx, r), jnp.where(last, c0_nx, c0),
                                            jnp.where(last, j_nx, j + 1), 1 - u % 2)
                m_prev, mx = key_step(j, u % 2, m_prev, mx, nxt)
            return m_prev, mx

        _, mx = lax.fori_loop(0, n_trips, body, (jnp.full((1, tq), -jnp.inf, F32), mx))
        out_t = acc_sc[0:dv, :] / acc_sc[dv:dv + 1, :]
        o_ref[0, row:row + tq, col:col + dv] = out_t.T.astype(o_ref.dtype)


def _flash_call(qt, k, vt, tq, n_tiles, tk, name):
    b, hq, _, dk, qw = qt.shape
    s = k.shape[2]
    hkv = k.shape[1]
    _, _, nchunk, dv, cw = vt.shape
    rep = hq // hkv
    tq_blk = tq * n_tiles
    assert tk % cw == 0 and (s // tk) % 2 == 0 and s % tq_blk == 0 and tq % qw == 0
    streams = tuple((r, ti * (tq // qw), ti * tq, r * dv) for r in range(rep) for ti in range(n_tiles))
    return pl.pallas_call(
        functools.partial(_flash_kernel, streams=streams, tq=tq, tk=tk),
        grid=(b, hkv, s // tq_blk),
        in_specs=[pl.BlockSpec((1, rep, tq_blk // qw, dk, qw), lambda bi, g, qi: (bi, g, qi, 0, 0)),
                  pl.BlockSpec((1, 1, s, dk), lambda bi, g, qi: (bi, g, 0, 0)),
                  pl.BlockSpec((1, 1, nchunk, dv, cw), lambda bi, g, qi: (bi, g, 0, 0, 0))],
        out_specs=pl.BlockSpec((1, tq_blk, rep * dv), lambda bi, g, qi: (bi, qi, g)),
        out_shape=jax.ShapeDtypeStruct((b, s, hq * dv), BF16),
        scratch_shapes=[pltpu.VMEM((dv + BF16_ROWS, tq), F32), pltpu.VMEM((2, tk, tq), F32)],
        compiler_params=_cparams(("parallel", "parallel", "arbitrary")),
        name=name,
    )(qt, k, vt)


def _post_kernel(*refs, bounds):
    n_set = len(bounds) - 1
    ins, (wout_ref, gffn_ref, wr_ref, br_ref, x1_ref, hp_ref, route_ref) = refs[:3 * n_set], refs[3 * n_set:]
    i = pl.program_id(0)
    for si in range(n_set):
        @pl.when((i >= bounds[si]) & (i < bounds[si + 1]))
        def _(si=si):
            _post_tile(*ins[3 * si:3 * si + 3], wout_ref, gffn_ref, wr_ref, br_ref, x1_ref, hp_ref, route_ref)


def _post_tile(oa_ref, ob_ref, x_ref, wout_ref, gffn_ref, wr_ref, br_ref, x1_ref, hp_ref, route_ref):
    for r0 in range(0, x_ref.shape[0], POST_SUB):
        rows = pl.ds(r0, POST_SUB)
        _post_rows(oa_ref.at[rows], ob_ref.at[rows], x_ref.at[rows], wout_ref, gffn_ref, wr_ref, br_ref,
                   x1_ref.at[rows], hp_ref.at[rows], route_ref.at[rows])


def _post_rows(oa_ref, ob_ref, x_ref, wout_ref, gffn_ref, wr_ref, br_ref, x1_ref, hp_ref, route_ref):
    half = oa_ref.shape[-1]
    mix = jnp.dot(oa_ref[...], wout_ref[0:half, :], preferred_element_type=F32)
    mix = mix + jnp.dot(ob_ref[...], wout_ref[half:2 * half, :], preferred_element_type=F32)
    x1 = x_ref[...] + mix
    x1_ref[...] = x1
    h = _rms(x1, gffn_ref[...])

    h_hi = h.astype(BF16)
    h_hi32 = h_hi.astype(F32)
    hp_ref[...] = _pack_bf16_pairs(h)

    h_lo = (h - h_hi32).astype(BF16)
    lg2 = (jnp.dot(h_hi, wr_ref[...], preferred_element_type=F32)
           + jnp.dot(h_lo, wr_ref[...], preferred_element_type=F32))
    lg = lg2[:, 0:LANES] + lg2[:, LANES:2 * LANES] + br_ref[...]
    lane = lax.broadcasted_iota(jnp.int32, lg.shape, 1)
    neg = jnp.float32(-jnp.inf)
    big = jnp.int32(4 * LANES)
    gl = jnp.where(lane < N_GROUPS, lg, neg)
    gmax = jnp.max(gl, axis=-1, keepdims=True)
    gsum = jnp.sum(jnp.exp(gl - gmax), axis=-1, keepdims=True)
    g_w = 1.0 / gsum
    g_idx = jnp.min(jnp.where(gl == gmax, lane, big), axis=-1, keepdims=True)
    lo_lane = N_GROUPS + EXPERTS_PER_GROUP * g_idx
    el = jnp.where((lane >= lo_lane) & (lane < lo_lane + EXPERTS_PER_GROUP), lg, neg)
    m1 = jnp.max(el, axis=-1, keepdims=True)
    i1 = jnp.min(jnp.where(el == m1, lane, big), axis=-1, keepdims=True)
    el2 = jnp.where(lane == i1, neg, el)
    m2 = jnp.max(el2, axis=-1, keepdims=True)
    i2 = jnp.min(jnp.where(el2 == m2, lane, big), axis=-1, keepdims=True)
    e2 = jnp.exp(m2 - m1)
    w1 = 1.0 / (1.0 + e2)
    w2 = e2 / (1.0 + e2)
    out = jnp.where(lane == 0, (i1 - N_GROUPS).astype(F32),
          jnp.where(lane == 1, (i2 - N_GROUPS).astype(F32),
          jnp.where(lane == 2, g_w * w1,
          jnp.where(lane == 3, g_w * w2, 0.0))))
    route_ref[...] = out


def _post_call(sets, wout, gffn, wr, br, tm):
    d = sets[0][2].shape[1]
    half = sets[0][0].shape[1]
    n_tiles = [x.shape[0] // tm for _, _, x in sets]
    bounds = tuple(int(v) for v in np.concatenate([[0], np.cumsum(n_tiles)]))
    t = bounds[-1] * tm
    const = lambda shape: pl.BlockSpec(shape, lambda i: (0,) * len(shape))
    row = lambda w: pl.BlockSpec((tm, w), lambda i: (i, 0))
    in_specs, operands = [], []
    for si, (oa, ob, x) in enumerate(sets):
        pin = lambda w, lo=bounds[si], n=n_tiles[si]: pl.BlockSpec((tm, w), lambda i: (jnp.clip(i - lo, 0, n - 1), 0))
        in_specs += [pin(half), pin(half), pin(d)]
        operands += [oa, ob, x]
    return pl.pallas_call(
        functools.partial(_post_kernel, bounds=bounds),
        grid=(bounds[-1],),
        in_specs=in_specs + [const(wout.shape), const(gffn.shape), const(wr.shape), const(br.shape)],
        out_specs=[row(d), row(d // 2), row(LANES)],
        out_shape=[jax.ShapeDtypeStruct((t, d), F32),
                   jax.ShapeDtypeStruct((t, d // 2), jnp.uint32),
                   jax.ShapeDtypeStruct((t, LANES), F32)],
        compiler_params=_cparams(("parallel",)),
        name="post_proj_router",
    )(*operands, wout, gffn, wr, br)


def _row_gather(src_hbm, idx_ref, buf, sem, n_rows):
    def body(r, carry):
        pltpu.make_async_copy(src_hbm.at[pl.ds(idx_ref[0, 0, r], 1)], buf.at[pl.ds(r, 1)], sem).start()
        return carry
    lax.fori_loop(0, n_rows, body, 0, unroll=GATHER_UNROLL)


def _moe_kernel(blk_e_ref, n_used_ref, tok_cur_ref, tok_next_ref, hp_hbm, wg_ref, wu_ref, wd_ref,
                out_ref, buf0, buf1, sem, wgb, wub, wdb):
    i = pl.program_id(0)
    tb, dh = buf0.shape
    n_used = n_used_ref[0]

    @pl.when((i == 0) | (blk_e_ref[i] != blk_e_ref[jnp.maximum(i - 1, 0)]))
    def _():
        wgb[...] = wg_ref[0].astype(BF16)
        wub[...] = wu_ref[0].astype(BF16)
        wdb[...] = wd_ref[0].astype(BF16)

    @pl.when(i == 0)
    def _():
        _row_gather(hp_hbm, tok_cur_ref, buf0, sem.at[0], tb)

    def wait_rows(buf, s_):
        pltpu.make_async_copy(hp_hbm.at[pl.ds(0, tb)], buf, sem.at[s_]).wait()

    def block(cur, cur_s, nxt, nxt_s):
        wait_rows(cur, cur_s)
        for r in range(tb):
            pltpu.make_async_copy(hp_hbm.at[pl.ds(tok_next_ref[0, 0, r], 1)], nxt.at[pl.ds(r, 1)],
                                  sem.at[nxt_s]).start()
        x_lo, x_hi = (v.astype(BF16) for v in _unpack_bf16_pairs(cur[...]))
        g = jnp.dot(x_lo, wgb[0:dh, :], preferred_element_type=F32)
        g = g + jnp.dot(x_hi, wgb[dh:2 * dh, :], preferred_element_type=F32)
        u = jnp.dot(x_lo, wub[0:dh, :], preferred_element_type=F32)
        u = u + jnp.dot(x_hi, wub[dh:2 * dh, :], preferred_element_type=F32)
        hid = (g * jax.nn.sigmoid(g)) * u
        out = jnp.dot(hid.astype(BF16), wdb[...], preferred_element_type=F32)
        out_ref[...] = _pack_bf16_pairs(out)

    used = i < n_used
    even = i % 2 == 0

    @pl.when(used & even)
    def _():
        block(buf0, 0, buf1, 1)

    @pl.when(used & jnp.logical_not(even))
    def _():
        block(buf1, 1, buf0, 0)

    @pl.when((i == n_used) & even)
    def _():
        wait_rows(buf0, 0)

    @pl.when((i == n_used) & jnp.logical_not(even))
    def _():
        wait_rows(buf1, 1)

    @pl.when(i >= n_used)
    def _():
        out_ref[...] = jnp.zeros(out_ref.shape, out_ref.dtype)


def _moe_call(blk_e, n_used, pad_tok, hp, wg, wu, wd, tb):
    nb = pad_tok.shape[0]
    dh = hp.shape[-1]
    d = 2 * dh
    de = wg.shape[-1]
    smem_blk = lambda f: pl.BlockSpec((1, 1, tb), f, memory_space=pltpu.SMEM)
    wspec = lambda shape: pl.BlockSpec(shape, lambda i, be, nu: (be[i], 0, 0))
    grid_spec = pltpu.PrefetchScalarGridSpec(
        num_scalar_prefetch=2,
        grid=(nb,),
        in_specs=[smem_blk(lambda i, be, nu: (i, 0, 0)),
                  smem_blk(lambda i, be, nu: (jnp.minimum(i + 1, nb - 1), 0, 0)),
                  pl.BlockSpec(memory_space=pl.ANY),
                  wspec((1, d, de)), wspec((1, d, de)), wspec((1, de, d))],
        out_specs=pl.BlockSpec((tb, dh), lambda i, be, nu: (i, 0)),
        scratch_shapes=[pltpu.VMEM((tb, dh), jnp.uint32), pltpu.VMEM((tb, dh), jnp.uint32),
                        pltpu.SemaphoreType.DMA((2,)),
                        pltpu.VMEM((d, de), BF16), pltpu.VMEM((d, de), BF16), pltpu.VMEM((de, d), BF16)],
    )
    return pl.pallas_call(
        _moe_kernel,
        grid_spec=grid_spec,
        out_shape=jax.ShapeDtypeStruct((nb * tb, dh), jnp.uint32),
        compiler_params=_cparams(("arbitrary",)),
        name="moe_experts",
    )(blk_e, n_used, pad_tok, pad_tok, hp, wg, wu, wd)


def _final_kernel(*refs, bounds):
    n_set = len(bounds) - 1
    dest_cur_ref, dest_next_ref, x1_ref, route_ref, rows_hbm, gfin_ref = refs[:6]
    y_refs = refs[6:6 + n_set]
    buf0, buf1, sem = refs[6 + n_set:]
    i = pl.program_id(0)
    last = i == pl.num_programs(0) - 1
    n_rows = buf0.shape[0]
    tm = n_rows // TOP_K

    @pl.when(i == 0)
    def _():
        _row_gather(rows_hbm, dest_cur_ref, buf0, sem.at[0], n_rows)

    def wait_rows(buf, s_):
        pltpu.make_async_copy(rows_hbm.at[pl.ds(0, n_rows)], buf, sem.at[s_]).wait()

    def block(cur, cur_s, nxt, nxt_s):
        wait_rows(cur, cur_s)
        for r in range(n_rows):
            pltpu.make_async_copy(rows_hbm.at[pl.ds(dest_next_ref[0, 0, r], 1)], nxt.at[pl.ds(r, 1)],
                                  sem.at[nxt_s]).start()
        route = route_ref[...]
        g0, g1 = route[:, 2:3], route[:, 3:4]
        a_lo, a_hi = _unpack_bf16_pairs(cur[0:tm, :])
        b_lo, b_hi = _unpack_bf16_pairs(cur[tm:2 * tm, :])
        moe = jnp.concatenate([a_lo * g0 + b_lo * g1, a_hi * g0 + b_hi * g1], axis=1)
        y = _rms(x1_ref[...] + moe, gfin_ref[...])
        for si in range(n_set):
            @pl.when((i >= bounds[si]) & (i < bounds[si + 1]))
            def _(si=si):
                y_refs[si][...] = y

    even = i % 2 == 0

    @pl.when(even)
    def _():
        block(buf0, 0, buf1, 1)

    @pl.when(jnp.logical_not(even))
    def _():
        block(buf1, 1, buf0, 0)

    @pl.when(last & even)
    def _():
        wait_rows(buf1, 1)

    @pl.when(last & jnp.logical_not(even))
    def _():
        wait_rows(buf0, 0)


def _final_call(dest, x1, route, rows, gfin, tm, n_tiles):
    t, d = x1.shape
    nt = t // tm
    bounds = tuple(int(v) for v in np.concatenate([[0], np.cumsum(n_tiles)]))
    assert bounds[-1] == nt
    smem_blk = lambda f: pl.BlockSpec((1, 1, TOP_K * tm), f, memory_space=pltpu.SMEM)
    out_specs = [pl.BlockSpec((tm, d), lambda i, lo=bounds[si], n=n_tiles[si]: (jnp.clip(i - lo, 0, n - 1), 0))
                 for si in range(len(n_tiles))]
    return pl.pallas_call(
        functools.partial(_final_kernel, bounds=bounds),
        grid=(nt,),
        in_specs=[smem_blk(lambda i: (i, 0, 0)),
                  smem_blk(lambda i: (jnp.minimum(i + 1, nt - 1), 0, 0)),
                  pl.BlockSpec((tm, d), lambda i: (i, 0)),
                  pl.BlockSpec((tm, LANES), lambda i: (i, 0)),
                  pl.BlockSpec(memory_space=pl.ANY),
                  pl.BlockSpec(gfin.shape, lambda i: (0, 0))],
        out_specs=out_specs,
        out_shape=[jax.ShapeDtypeStruct((n * tm, d), F32) for n in n_tiles],
        scratch_shapes=[pltpu.VMEM((TOP_K * tm, d // 2), jnp.uint32), pltpu.VMEM((TOP_K * tm, d // 2), jnp.uint32),
                        pltpu.SemaphoreType.DMA((2,))],
        compiler_params=_cparams(("arbitrary",)),
        name="combine_final_norm",
    )(dest, dest, x1, route, rows, gfin)


def _deinterleave(n):
    return np.concatenate([np.arange(0, n, 2), np.arange(1, n, 2)])


def _rope_pad_cols(base):
    pad = -np.ones(LANES // 4, np.int64)
    return np.concatenate([base + np.arange(0, QK_ROPE, 2), pad, base + np.arange(1, QK_ROPE, 2), pad])


def _take_cols(w, cols):
    w_ext = jnp.concatenate([w, jnp.zeros((w.shape[0], 1), w.dtype)], axis=1)
    return w_ext[:, np.where(cols < 0, w.shape[1], cols)]


def _prep_weights(w_in, w_q_up, q_norm, k_norm):
    o_kr = Q_LORA + KV_LORA
    o_gq = o_kr + QK_ROPE
    o_gk = o_gq + GQA_HEADS * HEAD_DIM
    o_gv = o_gk + GQA_KV_HEADS * HEAD_DIM
    cols = [np.arange(0, o_kr), _rope_pad_cols(o_kr)]
    for hd in range(GQA_HEADS):
        cols.append(o_gq + hd * HEAD_DIM + _deinterleave(HEAD_DIM))
    for hd in range(GQA_KV_HEADS):
        cols.append(o_gk + hd * HEAD_DIM + _deinterleave(HEAD_DIM))
    cols.append(np.arange(o_gv, o_gv + GQA_KV_HEADS * HEAD_DIM))
    win = _take_cols(w_in, np.concatenate(cols)).astype(BF16)

    qcols = []
    for hd in range(MLA_HEADS):
        base = hd * (QK_NOPE + QK_ROPE)
        qcols += [base + np.arange(QK_NOPE), _rope_pad_cols(base + QK_NOPE)]
    wq = _take_cols(w_q_up, np.concatenate(qcols)).astype(BF16)
    perm = _deinterleave(HEAD_DIM)
    return win, wq, q_norm[perm][None, :], k_norm[perm][None, :]


def _rope_tables(s):
    rows = s // GRID_W
    row = np.repeat(np.arange(rows, dtype=np.float64), GRID_W)
    col = np.tile(np.arange(GRID_W, dtype=np.float64), rows)

    def cos_sin(rot_dim):
        n_pairs = rot_dim // 4
        freqs = ROPE_THETA ** (-np.arange(n_pairs, dtype=np.float64) * 2.0 / (rot_dim // 2))
        ang = np.concatenate([row[:, None] * freqs[None, :], col[:, None] * freqs[None, :]], axis=-1)
        return np.cos(ang), np.sin(ang)

    ca, sa = cos_sin(QK_ROPE)
    cb, sb = cos_sin(HEAD_DIM)
    tabs = (np.concatenate([ca, ca, ca, ca], -1), np.concatenate([-sa, -sa, sa, sa], -1),
            np.concatenate([cb, cb], -1), np.concatenate([-sb, sb], -1))
    return tuple(jnp.asarray(t.astype(np.float32)) for t in tabs)


def _pick(n, pref):
    return pref if n % pref == 0 else n


def _route_plan(route, tb):
    t = route.shape[0]
    a = t * TOP_K
    expert_id = route[:, 0:TOP_K].astype(jnp.int32).reshape(a)
    onehot = (expert_id[:, None] == jnp.arange(N_EXPERTS, dtype=jnp.int32)[None, :]).astype(jnp.int32)
    csum = jnp.cumsum(onehot, axis=0)
    counts = csum[-1]
    rank = jnp.sum(onehot * csum, axis=1) - 1
    padded = ((counts + tb - 1) // tb) * tb
    pad_ends = jnp.cumsum(padded)
    pad_off = pad_ends - padded
    dest = pad_off[expert_id] + rank
    nb = (a + N_EXPERTS * tb) // tb
    blk_start = jnp.arange(nb, dtype=jnp.int32) * tb
    blk_e = jnp.minimum(jnp.sum(blk_start[:, None] >= pad_ends[None, :], axis=1), N_EXPERTS - 1)
    n_used = (pad_ends[-1] // tb).astype(jnp.int32).reshape(1)
    order = jnp.sort(expert_id * a + jnp.arange(a, dtype=jnp.int32)) % a
    row_e = jnp.repeat(blk_e, tb)
    row_rank = jnp.arange(nb * tb, dtype=jnp.int32) - pad_off[row_e]
    offsets = jnp.cumsum(counts) - counts
    src = jnp.clip(offsets[row_e] + row_rank, 0, a - 1)
    valid = (row_rank < counts[row_e]) & (jnp.arange(nb * tb, dtype=jnp.int32) < pad_ends[-1])
    pad_tok = jnp.where(valid, order[src] // TOP_K, 0)
    last_e = blk_e[jnp.maximum(n_used[0] - 1, 0)]
    blk_e = jnp.where(jnp.arange(nb) < n_used[0], blk_e, last_e).astype(jnp.int32)
    return pad_tok.reshape(nb, 1, tb), dest.reshape(t, TOP_K), blk_e, n_used


def _attention(x, p, tm):
    b, s, d = x.shape
    t = b * s
    qa, ka, va, qb, kb, vb = _pre_call(x, p["gattn"], p["win"], p["gqa"], p["wq"], p["gkv"], p["wkv"],
                                       p["gqn"], p["gkn"], *_rope_tables(s), tm)
    tk = 2 * tm if s % (4 * tm) == 0 else s // 2
    oa = _flash_call(qa, ka, va, _pick(s, 1024), 2 if s % 2048 == 0 else 1, tk, "mla_flash")
    ob = _flash_call(qb, kb, vb, _pick(s, 512), 1, tk, "gqa_flash")
    return oa.reshape(t, -1), ob.reshape(t, -1), x.reshape(t, d)


def _channel_mix(sets, p, tm):
    x1, hp, route = _post_call(sets, p["wout"], p["gffn"], p["wr"], p["br"], 2 * POST_SUB)
    t = x1.shape[0]
    tb = 256
    pad_tok, dest, blk_e, n_used = _route_plan(route, tb)
    rows = _moe_call(blk_e, n_used, pad_tok, hp, p["wg"], p["wu"], p["wd"], tb)
    nt = t // tm
    dest_blk = dest.reshape(nt, tm, TOP_K).transpose(0, 2, 1).reshape(nt, 1, TOP_K * tm)
    return _final_call(dest_blk, x1, route, rows, p["gfin"], tm, [x.shape[0] // tm for _, _, x in sets])


def kernel(x_prompt, x_sample, attn_norm, w_in, q_a_norm, w_q_up, kv_a_norm, w_kv_up, q_norm, k_norm, w_out,
           ffn_norm, w_group, b_group, w_expert, b_expert, w_gate, w_up, w_down, final_norm):
    assert attn_norm.shape[0] == 1, "single-layer trunk"
    win, wq, gqn, gkn = _prep_weights(w_in[0], w_q_up[0], q_norm[0], k_norm[0])
    d = w_in.shape[1]
    wr = jnp.zeros((d, LANES), F32).at[:, 0:N_GROUPS].set(w_group[0])
    wr = wr.at[:, N_GROUPS:N_GROUPS + N_EXPERTS].set(w_expert[0])
    br = jnp.zeros((1, LANES), F32).at[0, 0:N_GROUPS].set(b_group[0])
    br = br.at[0, N_GROUPS:N_GROUPS + N_EXPERTS].set(b_expert[0])
    wr_hi = wr.astype(BF16)
    wr = jnp.concatenate([wr_hi, (wr - wr_hi.astype(F32)).astype(BF16)], axis=1)
    p = dict(gattn=attn_norm, win=win, gqa=q_a_norm, wq=wq, gkv=kv_a_norm, wkv=w_kv_up[0].astype(BF16),
             gqn=gqn, gkn=gkn, wout=w_out[0].astype(BF16), gffn=ffn_norm, wr=wr, br=br,
             wg=w_gate[0], wu=w_up[0], wd=w_down[0],
             gfin=final_norm[None, :])
    xs = (x_prompt, x_sample)
    tm = 256
    assert all(x.shape[1] % (2 * POST_SUB) == 0 for x in xs)
    ys = _channel_mix([_attention(x, p, tm) for x in xs], p, tm)
    return tuple(y.reshape(x.shape) for y, x in zip(ys, xs))
```

```python
import functools
import math

import numpy as np
import jax
import jax.numpy as jnp
from jax import lax
from jax.experimental import pallas as pl
from jax.experimental.pallas import tpu as pltpu

F32 = jnp.float32
BF16 = jnp.bfloat16

EPS = 1e-6
ROPE_THETA = 10000.0
GRID_W = 64
LANES = 128
BF16_ROWS = 16
GATHER_UNROLL = 8
POST_SUB = 256
WORD_TILES = 8

MLA_HEADS = 8
Q_LORA = 512
KV_LORA = 256
QK_NOPE = 128
QK_ROPE = 64
V_HEAD = 128
MLA_DK = 256
GQA_HEADS = 8
GQA_KV_HEADS = 2
GQA_REP = GQA_HEADS // GQA_KV_HEADS
HEAD_DIM = 128
N_GROUPS = 4
EXPERTS_PER_GROUP = 8
N_EXPERTS = N_GROUPS * EXPERTS_PER_GROUP
TOP_K = 2
LOG2E = math.log2(math.e)

VMEM_LIMIT = 56 * 1024 * 1024


def _cparams(sem):
    return pltpu.CompilerParams(dimension_semantics=sem, vmem_limit_bytes=VMEM_LIMIT)


def _rms(x, gain):
    return x * lax.rsqrt(jnp.mean(x * x, axis=-1, keepdims=True) + EPS) * gain


def _rope(x, cos_t, sin_t):
    return x * cos_t + pltpu.roll(x, LANES // 2, axis=1) * sin_t


def _pack_bf16_pairs(v):
    n = v.shape[-1] // 2
    w = pltpu.bitcast(v.astype(BF16).astype(F32), jnp.uint32)
    return w[:, n:2 * n] | (w[:, 0:n] >> 16)


def _store_token_tiles(ref, w):
    rows, n = w.shape
    k = n // LANES
    for c in range(k):
        ref[pl.ds(c, rows, stride=k), :] = w[:, c * LANES:(c + 1) * LANES]


def _load_token_tiles(ref, rows, k, r0=0):
    return jnp.concatenate([ref[pl.ds(r0 * k + c, rows, stride=k), :] for c in range(k)], axis=1)


def _unpack_bf16_pairs(w):
    return pltpu.bitcast(w << 16, F32), pltpu.bitcast(w & jnp.uint32(0xFFFF0000), F32)


def _pre_kernel(x_ref, gattn_ref, win_ref, gqa_ref, wq_ref, gkv_ref, wkv_ref, gqn_ref, gkn_ref,
                ca_ref, sa_ref, cb_ref, sb_ref,
                qa_ref, ka_ref, va_ref, qb_ref, kb_ref, vb_ref):
    x = x_ref[0]
    h = _rms(x, gattn_ref[...]).astype(BF16)
    proj = jnp.dot(h, win_ref[...], preferred_element_type=F32)
    o = 0
    q_lat = proj[:, o:o + Q_LORA]; o += Q_LORA
    kv_lat = proj[:, o:o + KV_LORA]; o += KV_LORA
    k_rope = proj[:, o:o + LANES]; o += LANES
    gq = proj[:, o:o + GQA_HEADS * HEAD_DIM]; o += GQA_HEADS * HEAD_DIM
    gk = proj[:, o:o + GQA_KV_HEADS * HEAD_DIM]; o += GQA_KV_HEADS * HEAD_DIM
    gv = proj[:, o:o + GQA_KV_HEADS * HEAD_DIM]

    ca, sa, cb, sb = ca_ref[...], sa_ref[...], cb_ref[...], sb_ref[...]

    q = jnp.dot(_rms(q_lat, gqa_ref[...]).astype(BF16), wq_ref[...], preferred_element_type=F32)
    kv = jnp.dot(_rms(kv_lat, gkv_ref[...]).astype(BF16), wkv_ref[...], preferred_element_type=F32)
    k_pe = _rope(k_rope, ca, sa).astype(BF16)
    sc_a = np.float32((QK_NOPE + QK_ROPE) ** -0.5 * LOG2E)
    for hd in range(MLA_HEADS):
        qh = q[:, hd * MLA_DK:(hd + 1) * MLA_DK]
        qa_ref[0, hd, 0, 0:LANES, :] = (qh[:, 0:LANES] * sc_a).T.astype(BF16)
        qa_ref[0, hd, 0, LANES:MLA_DK, :] = (_rope(qh[:, LANES:MLA_DK], ca, sa) * sc_a).T.astype(BF16)
        kvh = kv[:, hd * 2 * LANES:(hd + 1) * 2 * LANES]
        ka_ref[0, hd, :, 0:LANES] = kvh[:, 0:LANES].astype(BF16)
        ka_ref[0, hd, :, LANES:MLA_DK] = k_pe
        va_ref[0, hd, 0] = kvh[:, LANES:2 * LANES].T.astype(BF16)

    sc_b = np.float32(HEAD_DIM ** -0.5 * LOG2E)
    gqn, gkn = gqn_ref[...], gkn_ref[...]
    for hd in range(GQA_HEADS):
        qh = _rope(_rms(gq[:, hd * HEAD_DIM:(hd + 1) * HEAD_DIM], gqn), cb, sb)
        qb_ref[0, hd, 0] = (qh * sc_b).T.astype(BF16)
    for hd in range(GQA_KV_HEADS):
        kh = _rope(_rms(gk[:, hd * HEAD_DIM:(hd + 1) * HEAD_DIM], gkn), cb, sb)
        kb_ref[0, hd] = kh.astype(BF16)
        vb_ref[0, hd, 0] = gv[:, hd * HEAD_DIM:(hd + 1) * HEAD_DIM].T.astype(BF16)


def _pre_call(x, gattn, win, gqa, wq, gkv, wkv, gqn, gkn, ca, sa, cb, sb, tm):
    b, s, d = x.shape
    const = lambda shape: pl.BlockSpec(shape, lambda bi, si: (0,) * len(shape))
    tab = pl.BlockSpec((tm, LANES), lambda bi, si: (si, 0))
    head = lambda nh, w: pl.BlockSpec((1, nh, tm, w), lambda bi, si: (bi, 0, si, 0))
    head_t = lambda nh, w: pl.BlockSpec((1, nh, 1, w, tm), lambda bi, si: (bi, 0, si, 0, 0))
    return pl.pallas_call(
        _pre_kernel,
        grid=(b, s // tm),
        in_specs=[pl.BlockSpec((1, tm, d), lambda bi, si: (bi, si, 0)),
                  const(gattn.shape), const(win.shape), const(gqa.shape), const(wq.shape),
                  const(gkv.shape), const(wkv.shape), const(gqn.shape), const(gkn.shape),
                  tab, tab, tab, tab],
        out_specs=[head_t(MLA_HEADS, MLA_DK), head(MLA_HEADS, MLA_DK), head_t(MLA_HEADS, V_HEAD),
                   head_t(GQA_HEADS, HEAD_DIM), head(GQA_KV_HEADS, HEAD_DIM), head_t(GQA_KV_HEADS, HEAD_DIM)],
        out_shape=[jax.ShapeDtypeStruct((b, MLA_HEADS, s // tm, MLA_DK, tm), BF16),
                   jax.ShapeDtypeStruct((b, MLA_HEADS, s, MLA_DK), BF16),
                   jax.ShapeDtypeStruct((b, MLA_HEADS, s // tm, V_HEAD, tm), BF16),
                   jax.ShapeDtypeStruct((b, GQA_HEADS, s // tm, HEAD_DIM, tm), BF16),
                   jax.ShapeDtypeStruct((b, GQA_KV_HEADS, s, HEAD_DIM), BF16),
                   jax.ShapeDtypeStruct((b, GQA_KV_HEADS, s // tm, HEAD_DIM, tm), BF16)],
        compiler_params=_cparams(("parallel", "parallel")),
        name="pre_proj",
    )(x, gattn, win, gqa, wq, gkv, wkv, gqn, gkn, ca, sa, cb, sb)


def _flash_kernel(q_ref, k_ref, vt_ref, o_ref, acc_sc, s_sc, *, streams, tq, tk):
    s_len = k_ref.shape[2]
    dv, cw = vt_ref.shape[3], vt_ref.shape[4]
    qw = q_ref.shape[4]
    n_qc = tq // qw
    n_sub = tk // cw
    n_steps = s_len // tk
    unroll = 8 if n_steps % 16 == 0 else 4 if n_steps % 8 == 0 else 2
    n_trips = n_steps // unroll

    def scores(r, c0, j, slot):
        start = j * tk if isinstance(j, int) else pl.multiple_of(j * tk, tk)
        k = k_ref[0, 0, pl.ds(start, tk), :]
        st = jnp.concatenate([jnp.dot(k, q_ref[0, r, c0 + c], preferred_element_type=F32)
                              for c in range(n_qc)], axis=1)
        s_sc[slot] = st
        return jnp.max(st, axis=0, keepdims=True)

    ones = jnp.ones((BF16_ROWS, cw), BF16)

    def key_step(j, slot, m_prev, mx, next_scores):
        mx_next = next_scores()
        m_new = jnp.maximum(m_prev, mx)
        alpha = jnp.exp2(m_prev - m_new)
        pb = jnp.exp2(s_sc[slot] - m_new).astype(BF16)
        pv = None
        for c in range(n_sub):
            vt1 = jnp.concatenate([vt_ref[0, 0, j * n_sub + c], ones], axis=0)
            d = jnp.dot(vt1, pb[c * cw:(c + 1) * cw], preferred_element_type=F32)
            pv = d if pv is None else pv + d
        acc_sc[...] = alpha * acc_sc[...] + pv
        return m_new, mx_next

    mx = scores(streams[0][0], streams[0][1], 0, 0)
    for si, stream in enumerate(streams):
        r, c0, row, col = stream
        r_nx, c0_nx, j_nx = (streams[si + 1][0], streams[si + 1][1], 0) if si + 1 < len(streams) \
            else (r, c0, n_steps - 1)
        acc_sc[...] = jnp.zeros(acc_sc.shape, F32)

        def body(jj, carry, r=r, c0=c0, r_nx=r_nx, c0_nx=c0_nx, j_nx=j_nx):
            m_prev, mx = carry
            for u in range(unroll):
                j = jj * unroll + u
                if u + 1 < unroll:
                    nxt = functools.partial(scores, r, c0, j + 1, 1 - u % 2)
                else:
                    last = jj == n_trips - 1
                    nxt = functools.partial(scores, jnp.where(last, r_nx, r), jnp.where(last, c0_nx, c0),
                                            jnp.where(last, j_nx, j + 1), 1 - u % 2)
                m_prev, mx = key_step(j, u % 2, m_prev, mx, nxt)
            return m_prev, mx

        _, mx = lax.fori_loop(0, n_trips, body, (jnp.full((1, tq), -jnp.inf, F32), mx))
        out_t = acc_sc[0:dv, :] / acc_sc[dv:dv + 1, :]
        o_ref[0, row:row + tq, col:col + dv] = out_t.T.astype(o_ref.dtype)


def _flash_call(qt, k, vt, tq, n_tiles, tk, name):
    b, hq, _, dk, qw = qt.shape
    s = k.shape[2]
    hkv = k.shape[1]
    _, _, nchunk, dv, cw = vt.shape
    rep = hq // hkv
    tq_blk = tq * n_tiles
    assert tk % cw == 0 and (s // tk) % 2 == 0 and s % tq_blk == 0 and tq % qw == 0
    streams = tuple((r, ti * (tq // qw), ti * tq, r * dv) for r in range(rep) for ti in range(n_tiles))
    return pl.pallas_call(
        functools.partial(_flash_kernel, streams=streams, tq=tq, tk=tk),
        grid=(b, hkv, s // tq_blk),
        in_specs=[pl.BlockSpec((1, rep, tq_blk // qw, dk, qw), lambda bi, g, qi: (bi, g, qi, 0, 0)),
                  pl.BlockSpec((1, 1, s, dk), lambda bi, g, qi: (bi, g, 0, 0)),
                  pl.BlockSpec((1, 1, nchunk, dv, cw), lambda bi, g, qi: (bi, g, 0, 0, 0))],
        out_specs=pl.BlockSpec((1, tq_blk, rep * dv), lambda bi, g, qi: (bi, qi, g)),
        out_shape=jax.ShapeDtypeStruct((b, s, hq * dv), BF16),
        scratch_shapes=[pltpu.VMEM((dv + BF16_ROWS, tq), F32), pltpu.VMEM((2, tk, tq), F32)],
        compiler_params=_cparams(("parallel", "parallel", "arbitrary")),
        name=name,
    )(qt, k, vt)


def _post_kernel(*refs, bounds):
    n_set = len(bounds) - 1
    ins, (wout_ref, gffn_ref, wr_ref, br_ref, x1_ref, hp_ref, route_ref) = refs[:3 * n_set], refs[3 * n_set:]
    i = pl.program_id(0)
    for si in range(n_set):
        @pl.when((i >= bounds[si]) & (i < bounds[si + 1]))
        def _(si=si):
            _post_tile(*ins[3 * si:3 * si + 3], wout_ref, gffn_ref, wr_ref, br_ref, x1_ref, hp_ref, route_ref)


def _post_tile(oa_ref, ob_ref, x_ref, wout_ref, gffn_ref, wr_ref, br_ref, x1_ref, hp_ref, route_ref):
    for r0 in range(0, x_ref.shape[0], POST_SUB):
        rows = pl.ds(r0, POST_SUB)
        k = hp_ref.shape[0] // x_ref.shape[0]
        _post_rows(oa_ref.at[rows], ob_ref.at[rows], x_ref.at[rows], wout_ref, gffn_ref, wr_ref, br_ref,
                   x1_ref.at[rows], hp_ref.at[pl.ds(r0 * k, POST_SUB * k)], route_ref.at[rows])


def _post_rows(oa_ref, ob_ref, x_ref, wout_ref, gffn_ref, wr_ref, br_ref, x1_ref, hp_ref, route_ref):
    half = oa_ref.shape[-1]
    mix = jnp.dot(oa_ref[...], wout_ref[0:half, :], preferred_element_type=F32)
    mix = mix + jnp.dot(ob_ref[...], wout_ref[half:2 * half, :], preferred_element_type=F32)
    x1 = x_ref[...] + mix
    x1_ref[...] = x1
    h = _rms(x1, gffn_ref[...])

    h_hi = h.astype(BF16)
    h_hi32 = h_hi.astype(F32)
    _store_token_tiles(hp_ref, _pack_bf16_pairs(h))

    h_lo = (h - h_hi32).astype(BF16)
    lg2 = (jnp.dot(h_hi, wr_ref[...], preferred_element_type=F32)
           + jnp.dot(h_lo, wr_ref[...], preferred_element_type=F32))
    lg = lg2[:, 0:LANES] + lg2[:, LANES:2 * LANES] + br_ref[...]
    lane = lax.broadcasted_iota(jnp.int32, lg.shape, 1)
    neg = jnp.float32(-jnp.inf)
    big = jnp.int32(4 * LANES)
    gl = jnp.where(lane < N_GROUPS, lg, neg)
    gmax = jnp.max(gl, axis=-1, keepdims=True)
    gsum = jnp.sum(jnp.exp(gl - gmax), axis=-1, keepdims=True)
    g_w = 1.0 / gsum
    g_idx = jnp.min(jnp.where(gl == gmax, lane, big), axis=-1, keepdims=True)
    lo_lane = N_GROUPS + EXPERTS_PER_GROUP * g_idx
    el = jnp.where((lane >= lo_lane) & (lane < lo_lane + EXPERTS_PER_GROUP), lg, neg)
    m1 = jnp.max(el, axis=-1, keepdims=True)
    i1 = jnp.min(jnp.where(el == m1, lane, big), axis=-1, keepdims=True)
    el2 = jnp.where(lane == i1, neg, el)
    m2 = jnp.max(el2, axis=-1, keepdims=True)
    i2 = jnp.min(jnp.where(el2 == m2, lane, big), axis=-1, keepdims=True)
    e2 = jnp.exp(m2 - m1)
    w1 = 1.0 / (1.0 + e2)
    w2 = e2 / (1.0 + e2)
    out = jnp.where(lane == 0, (i1 - N_GROUPS).astype(F32),
          jnp.where(lane == 1, (i2 - N_GROUPS).astype(F32),
          jnp.where(lane == 2, g_w * w1,
          jnp.where(lane == 3, g_w * w2, 0.0))))
    route_ref[...] = out


def _post_call(sets, wout, gffn, wr, br, tm):
    d = sets[0][2].shape[1]
    half = sets[0][0].shape[1]
    n_tiles = [x.shape[0] // tm for _, _, x in sets]
    bounds = tuple(int(v) for v in np.concatenate([[0], np.cumsum(n_tiles)]))
    t = bounds[-1] * tm
    const = lambda shape: pl.BlockSpec(shape, lambda i: (0,) * len(shape))
    row = lambda w: pl.BlockSpec((tm, w), lambda i: (i, 0))
    in_specs, operands = [], []
    for si, (oa, ob, x) in enumerate(sets):
        pin = lambda w, lo=bounds[si], n=n_tiles[si]: pl.BlockSpec((tm, w), lambda i: (jnp.clip(i - lo, 0, n - 1), 0))
        in_specs += [pin(half), pin(half), pin(d)]
        operands += [oa, ob, x]
    return pl.pallas_call(
        functools.partial(_post_kernel, bounds=bounds),
        grid=(bounds[-1],),
        in_specs=in_specs + [const(wout.shape), const(gffn.shape), const(wr.shape), const(br.shape)],
        out_specs=[row(d), pl.BlockSpec((tm * WORD_TILES, LANES), lambda i: (i, 0)), row(LANES)],
        out_shape=[jax.ShapeDtypeStruct((t, d), F32),
                   jax.ShapeDtypeStruct((t * WORD_TILES, LANES), jnp.uint32),
                   jax.ShapeDtypeStruct((t, LANES), F32)],
        compiler_params=_cparams(("parallel",)),
        name="post_proj_router",
    )(*operands, wout, gffn, wr, br)


def _tile_copy(src_hbm, idx_ref, r, buf, sem):
    k = WORD_TILES
    return pltpu.make_async_copy(src_hbm.at[pl.ds(pl.multiple_of(idx_ref[0, 0, r], k), k)],
                                 buf.at[pl.ds(pl.multiple_of(r * k, k), k)], sem)


def _row_gather(src_hbm, idx_ref, buf, sem, n_rows):
    def body(r, carry):
        _tile_copy(src_hbm, idx_ref, r, buf, sem).start()
        return carry
    lax.fori_loop(0, n_rows, body, 0, unroll=GATHER_UNROLL)


def _moe_kernel(blk_e_ref, n_used_ref, tok_cur_ref, tok_next_ref, hp_hbm, wg_ref, wu_ref, wd_ref,
                out_ref, buf0, buf1, sem, wgb, wub, wdb):
    i = pl.program_id(0)
    tb = buf0.shape[0] // WORD_TILES
    dh = WORD_TILES * LANES
    n_used = n_used_ref[0]

    @pl.when((i == 0) | (blk_e_ref[i] != blk_e_ref[jnp.maximum(i - 1, 0)]))
    def _():
        wgb[...] = wg_ref[0].astype(BF16)
        wub[...] = wu_ref[0].astype(BF16)
        wdb[...] = wd_ref[0].astype(BF16)

    @pl.when(i == 0)
    def _():
        _row_gather(hp_hbm, tok_cur_ref, buf0, sem.at[0], tb)

    def wait_rows(buf, s_):
        pltpu.make_async_copy(hp_hbm.at[pl.ds(0, tb * WORD_TILES)], buf, sem.at[s_]).wait()

    def block(cur, cur_s, nxt, nxt_s):
        wait_rows(cur, cur_s)
        for r in range(tb):
            _tile_copy(hp_hbm, tok_next_ref, r, nxt, sem.at[nxt_s]).start()
        x_lo, x_hi = (v.astype(BF16) for v in _unpack_bf16_pairs(_load_token_tiles(cur, tb, WORD_TILES)))
        g = jnp.dot(x_lo, wgb[0:dh, :], preferred_element_type=F32)
        g = g + jnp.dot(x_hi, wgb[dh:2 * dh, :], preferred_element_type=F32)
        u = jnp.dot(x_lo, wub[0:dh, :], preferred_element_type=F32)
        u = u + jnp.dot(x_hi, wub[dh:2 * dh, :], preferred_element_type=F32)
        hid = (g * jax.nn.sigmoid(g)) * u
        out = jnp.dot(hid.astype(BF16), wdb[...], preferred_element_type=F32)
        _store_token_tiles(out_ref, _pack_bf16_pairs(out))

    used = i < n_used
    even = i % 2 == 0

    @pl.when(used & even)
    def _():
        block(buf0, 0, buf1, 1)

    @pl.when(used & jnp.logical_not(even))
    def _():
        block(buf1, 1, buf0, 0)

    @pl.when((i == n_used) & even)
    def _():
        wait_rows(buf0, 0)

    @pl.when((i == n_used) & jnp.logical_not(even))
    def _():
        wait_rows(buf1, 1)

    @pl.when(i >= n_used)
    def _():
        out_ref[...] = jnp.zeros(out_ref.shape, out_ref.dtype)


def _moe_call(blk_e, n_used, pad_tok, hp, wg, wu, wd, tb):
    nb = pad_tok.shape[0]
    dh = WORD_TILES * LANES
    d = 2 * dh
    de = wg.shape[-1]
    smem_blk = lambda f: pl.BlockSpec((1, 1, tb), f, memory_space=pltpu.SMEM)
    wspec = lambda shape: pl.BlockSpec(shape, lambda i, be, nu: (be[i], 0, 0))
    grid_spec = pltpu.PrefetchScalarGridSpec(
        num_scalar_prefetch=2,
        grid=(nb,),
        in_specs=[smem_blk(lambda i, be, nu: (i, 0, 0)),
                  smem_blk(lambda i, be, nu: (jnp.minimum(i + 1, nb - 1), 0, 0)),
                  pl.BlockSpec(memory_space=pl.ANY),
                  wspec((1, d, de)), wspec((1, d, de)), wspec((1, de, d))],
        out_specs=pl.BlockSpec((tb * WORD_TILES, LANES), lambda i, be, nu: (i, 0)),
        scratch_shapes=[pltpu.VMEM((tb * WORD_TILES, LANES), jnp.uint32),
                        pltpu.VMEM((tb * WORD_TILES, LANES), jnp.uint32),
                        pltpu.SemaphoreType.DMA((2,)),
                        pltpu.VMEM((d, de), BF16), pltpu.VMEM((d, de), BF16), pltpu.VMEM((de, d), BF16)],
    )
    return pl.pallas_call(
        _moe_kernel,
        grid_spec=grid_spec,
        out_shape=jax.ShapeDtypeStruct((nb * tb * WORD_TILES, LANES), jnp.uint32),
        compiler_params=_cparams(("arbitrary",)),
        name="moe_experts",
    )(blk_e, n_used, pad_tok, pad_tok, hp, wg, wu, wd)


def _final_kernel(*refs, bounds):
    n_set = len(bounds) - 1
    dest_cur_ref, dest_next_ref, x1_ref, route_ref, rows_hbm, gfin_ref = refs[:6]
    y_refs = refs[6:6 + n_set]
    buf0, buf1, sem = refs[6 + n_set:]
    i = pl.program_id(0)
    last = i == pl.num_programs(0) - 1
    n_rows = buf0.shape[0] // WORD_TILES
    tm = n_rows // TOP_K

    @pl.when(i == 0)
    def _():
        _row_gather(rows_hbm, dest_cur_ref, buf0, sem.at[0], n_rows)

    def wait_rows(buf, s_):
        pltpu.make_async_copy(rows_hbm.at[pl.ds(0, n_rows * WORD_TILES)], buf, sem.at[s_]).wait()

    def block(cur, cur_s, nxt, nxt_s):
        wait_rows(cur, cur_s)
        for r in range(n_rows):
            _tile_copy(rows_hbm, dest_next_ref, r, nxt, sem.at[nxt_s]).start()
        route = route_ref[...]
        g0, g1 = route[:, 2:3], route[:, 3:4]
        a_lo, a_hi = _unpack_bf16_pairs(_load_token_tiles(cur, tm, WORD_TILES))
        b_lo, b_hi = _unpack_bf16_pairs(_load_token_tiles(cur, tm, WORD_TILES, r0=tm))
        moe = jnp.concatenate([a_lo * g0 + b_lo * g1, a_hi * g0 + b_hi * g1], axis=1)
        y = _rms(x1_ref[...] + moe, gfin_ref[...])
        for si in range(n_set):
            @pl.when((i >= bounds[si]) & (i < bounds[si + 1]))
            def _(si=si):
                y_refs[si][...] = y

    even = i % 2 == 0

    @pl.when(even)
    def _():
        block(buf0, 0, buf1, 1)

    @pl.when(jnp.logical_not(even))
    def _():
        block(buf1, 1, buf0, 0)

    @pl.when(last & even)
    def _():
        wait_rows(buf1, 1)

    @pl.when(last & jnp.logical_not(even))
    def _():
        wait_rows(buf0, 0)


def _final_call(dest, x1, route, rows, gfin, tm, n_tiles):
    t, d = x1.shape
    nt = t // tm
    bounds = tuple(int(v) for v in np.concatenate([[0], np.cumsum(n_tiles)]))
    assert bounds[-1] == nt
    smem_blk = lambda f: pl.BlockSpec((1, 1, TOP_K * tm), f, memory_space=pltpu.SMEM)
    out_specs = [pl.BlockSpec((tm, d), lambda i, lo=bounds[si], n=n_tiles[si]: (jnp.clip(i - lo, 0, n - 1), 0))
                 for si in range(len(n_tiles))]
    return pl.pallas_call(
        functools.partial(_final_kernel, bounds=bounds),
        grid=(nt,),
        in_specs=[smem_blk(lambda i: (i, 0, 0)),
                  smem_blk(lambda i: (jnp.minimum(i + 1, nt - 1), 0, 0)),
                  pl.BlockSpec((tm, d), lambda i: (i, 0)),
                  pl.BlockSpec((tm, LANES), lambda i: (i, 0)),
                  pl.BlockSpec(memory_space=pl.ANY),
                  pl.BlockSpec(gfin.shape, lambda i: (0, 0))],
        out_specs=out_specs,
        out_shape=[jax.ShapeDtypeStruct((n * tm, d), F32) for n in n_tiles],
        scratch_shapes=[pltpu.VMEM((TOP_K * tm * WORD_TILES, LANES), jnp.uint32),
                        pltpu.VMEM((TOP_K * tm * WORD_TILES, LANES), jnp.uint32),
                        pltpu.SemaphoreType.DMA((2,))],
        compiler_params=_cparams(("arbitrary",)),
        name="combine_final_norm",
    )(dest, dest, x1, route, rows, gfin)


def _deinterleave(n):
    return np.concatenate([np.arange(0, n, 2), np.arange(1, n, 2)])


def _rope_pad_cols(base):
    pad = -np.ones(LANES // 4, np.int64)
    return np.concatenate([base + np.arange(0, QK_ROPE, 2), pad, base + np.arange(1, QK_ROPE, 2), pad])


def _take_cols(w, cols):
    w_ext = jnp.concatenate([w, jnp.zeros((w.shape[0], 1), w.dtype)], axis=1)
    return w_ext[:, np.where(cols < 0, w.shape[1], cols)]


def _prep_weights(w_in, w_q_up, q_norm, k_norm):
    o_kr = Q_LORA + KV_LORA
    o_gq = o_kr + QK_ROPE
    o_gk = o_gq + GQA_HEADS * HEAD_DIM
    o_gv = o_gk + GQA_KV_HEADS * HEAD_DIM
    cols = [np.arange(0, o_kr), _rope_pad_cols(o_kr)]
    for hd in range(GQA_HEADS):
        cols.append(o_gq + hd * HEAD_DIM + _deinterleave(HEAD_DIM))
    for hd in range(GQA_KV_HEADS):
        cols.append(o_gk + hd * HEAD_DIM + _deinterleave(HEAD_DIM))
    cols.append(np.arange(o_gv, o_gv + GQA_KV_HEADS * HEAD_DIM))
    win = _take_cols(w_in, np.concatenate(cols)).astype(BF16)

    qcols = []
    for hd in range(MLA_HEADS):
        base = hd * (QK_NOPE + QK_ROPE)
        qcols += [base + np.arange(QK_NOPE), _rope_pad_cols(base + QK_NOPE)]
    wq = _take_cols(w_q_up, np.concatenate(qcols)).astype(BF16)
    perm = _deinterleave(HEAD_DIM)
    return win, wq, q_norm[perm][None, :], k_norm[perm][None, :]


def _rope_tables(s):
    rows = s // GRID_W
    row = np.repeat(np.arange(rows, dtype=np.float64), GRID_W)
    col = np.tile(np.arange(GRID_W, dtype=np.float64), rows)

    def cos_sin(rot_dim):
        n_pairs = rot_dim // 4
        freqs = ROPE_THETA ** (-np.arange(n_pairs, dtype=np.float64) * 2.0 / (rot_dim // 2))
        ang = np.concatenate([row[:, None] * freqs[None, :], col[:, None] * freqs[None, :]], axis=-1)
        return np.cos(ang), np.sin(ang)

    ca, sa = cos_sin(QK_ROPE)
    cb, sb = cos_sin(HEAD_DIM)
    tabs = (np.concatenate([ca, ca, ca, ca], -1), np.concatenate([-sa, -sa, sa, sa], -1),
            np.concatenate([cb, cb], -1), np.concatenate([-sb, sb], -1))
    return tuple(jnp.asarray(t.astype(np.float32)) for t in tabs)


def _pick(n, pref):
    return pref if n % pref == 0 else n


def _route_plan(route, tb):
    t = route.shape[0]
    a = t * TOP_K
    expert_id = route[:, 0:TOP_K].astype(jnp.int32).reshape(a)
    onehot = (expert_id[:, None] == jnp.arange(N_EXPERTS, dtype=jnp.int32)[None, :]).astype(jnp.int32)
    csum = jnp.cumsum(onehot, axis=0)
    counts = csum[-1]
    rank = jnp.sum(onehot * csum, axis=1) - 1
    padded = ((counts + tb - 1) // tb) * tb
    pad_ends = jnp.cumsum(padded)
    pad_off = pad_ends - padded
    dest = pad_off[expert_id] + rank
    nb = (a + N_EXPERTS * tb) // tb
    blk_start = jnp.arange(nb, dtype=jnp.int32) * tb
    blk_e = jnp.minimum(jnp.sum(blk_start[:, None] >= pad_ends[None, :], axis=1), N_EXPERTS - 1)
    n_used = (pad_ends[-1] // tb).astype(jnp.int32).reshape(1)
    order = jnp.sort(expert_id * a + jnp.arange(a, dtype=jnp.int32)) % a
    row_e = jnp.repeat(blk_e, tb)
    row_rank = jnp.arange(nb * tb, dtype=jnp.int32) - pad_off[row_e]
    offsets = jnp.cumsum(counts) - counts
    src = jnp.clip(offsets[row_e] + row_rank, 0, a - 1)
    valid = (row_rank < counts[row_e]) & (jnp.arange(nb * tb, dtype=jnp.int32) < pad_ends[-1])
    pad_tok = jnp.where(valid, order[src] // TOP_K, 0)
    last_e = blk_e[jnp.maximum(n_used[0] - 1, 0)]
    blk_e = jnp.where(jnp.arange(nb) < n_used[0], blk_e, last_e).astype(jnp.int32)
    return (pad_tok * WORD_TILES).reshape(nb, 1, tb), (dest * WORD_TILES).reshape(t, TOP_K), blk_e, n_used


def _attention(x, p, tm):
    b, s, d = x.shape
    t = b * s
    qa, ka, va, qb, kb, vb = _pre_call(x, p["gattn"], p["win"], p["gqa"], p["wq"], p["gkv"], p["wkv"],
                                       p["gqn"], p["gkn"], *_rope_tables(s), tm)
    tk = 2 * tm if s % (4 * tm) == 0 else s // 2
    oa = _flash_call(qa, ka, va, _pick(s, 1024), 2 if s % 2048 == 0 else 1, tk, "mla_flash")
    ob = _flash_call(qb, kb, vb, _pick(s, 512), 1, tk, "gqa_flash")
    return oa.reshape(t, -1), ob.reshape(t, -1), x.reshape(t, d)


def _channel_mix(sets, p, tm):
    x1, hp, route = _post_call(sets, p["wout"], p["gffn"], p["wr"], p["br"], 2 * POST_SUB)
    t = x1.shape[0]
    tb = 256
    pad_tok, dest, blk_e, n_used = _route_plan(route, tb)
    rows = _moe_call(blk_e, n_used, pad_tok, hp, p["wg"], p["wu"], p["wd"], tb)
    nt = t // tm
    dest_blk = dest.reshape(nt, tm, TOP_K).transpose(0, 2, 1).reshape(nt, 1, TOP_K * tm)
    return _final_call(dest_blk, x1, route, rows, p["gfin"], tm, [x.shape[0] // tm for _, _, x in sets])


def kernel(x_prompt, x_sample, attn_norm, w_in, q_a_norm, w_q_up, kv_a_norm, w_kv_up, q_norm, k_norm, w_out,
           ffn_norm, w_group, b_group, w_expert, b_expert, w_gate, w_up, w_down, final_norm):
    assert attn_norm.shape[0] == 1, "single-layer trunk"
    win, wq, gqn, gkn = _prep_weights(w_in[0], w_q_up[0], q_norm[0], k_norm[0])
    d = w_in.shape[1]
    wr = jnp.zeros((d, LANES), F32).at[:, 0:N_GROUPS].set(w_group[0])
    wr = wr.at[:, N_GROUPS:N_GROUPS + N_EXPERTS].set(w_expert[0])
    br = jnp.zeros((1, LANES), F32).at[0, 0:N_GROUPS].set(b_group[0])
    br = br.at[0, N_GROUPS:N_GROUPS + N_EXPERTS].set(b_expert[0])
    wr_hi = wr.astype(BF16)
    wr = jnp.concatenate([wr_hi, (wr - wr_hi.astype(F32)).astype(BF16)], axis=1)
    p = dict(gattn=attn_norm, win=win, gqa=q_a_norm, wq=wq, gkv=kv_a_norm, wkv=w_kv_up[0].astype(BF16),
             gqn=gqn, gkn=gkn, wout=w_out[0].astype(BF16), gffn=ffn_norm, wr=wr, br=br,
             wg=w_gate[0], wu=w_up[0], wd=w_down[0],
             gfin=final_norm[None, :])
    xs = (x_prompt, x_sample)
    tm = 256
    assert all(x.shape[1] % (2 * POST_SUB) == 0 for x in xs)
    ys = _channel_mix([_attention(x, p, tm) for x in xs], p, tm)
    return tuple(y.reshape(x.shape) for y, x in zip(ys, xs))
```

```python
import functools
import math

import numpy as np
import jax
import jax.numpy as jnp
from jax import lax
from jax.experimental import pallas as pl
from jax.experimental.pallas import tpu as pltpu

F32 = jnp.float32
BF16 = jnp.bfloat16

EPS = 1e-6
ROPE_THETA = 10000.0
GRID_W = 64
LANES = 128
BF16_ROWS = 16
GATHER_UNROLL = 8
POST_SUB = 256
DMA_QUEUES = 2
WORD_TILES = 8

MLA_HEADS = 8
Q_LORA = 512
KV_LORA = 256
QK_NOPE = 128
QK_ROPE = 64
V_HEAD = 128
MLA_DK = 256
GQA_HEADS = 8
GQA_KV_HEADS = 2
GQA_REP = GQA_HEADS // GQA_KV_HEADS
HEAD_DIM = 128
N_GROUPS = 4
EXPERTS_PER_GROUP = 8
N_EXPERTS = N_GROUPS * EXPERTS_PER_GROUP
TOP_K = 2
LOG2E = math.log2(math.e)

VMEM_LIMIT = 56 * 1024 * 1024


def _cparams(sem):
    return pltpu.CompilerParams(dimension_semantics=sem, vmem_limit_bytes=VMEM_LIMIT)


def _rms(x, gain):
    return x * lax.rsqrt(jnp.mean(x * x, axis=-1, keepdims=True) + EPS) * gain


def _rope(x, cos_t, sin_t):
    return x * cos_t + pltpu.roll(x, LANES // 2, axis=1) * sin_t


def _pack_bf16_pairs(v):
    n = v.shape[-1] // 2
    w = pltpu.bitcast(v.astype(BF16).astype(F32), jnp.uint32)
    return w[:, n:2 * n] | (w[:, 0:n] >> 16)


def _store_token_tiles(ref, w):
    rows, n = w.shape
    k = n // LANES
    for c in range(k):
        ref[pl.ds(c, rows, stride=k), :] = w[:, c * LANES:(c + 1) * LANES]


def _load_token_tiles(ref, rows, k, r0=0):
    return jnp.concatenate([ref[pl.ds(r0 * k + c, rows, stride=k), :] for c in range(k)], axis=1)


def _unpack_bf16_pairs(w):
    return pltpu.bitcast(w << 16, F32), pltpu.bitcast(w & jnp.uint32(0xFFFF0000), F32)


def _pre_kernel(x_ref, gattn_ref, win_ref, gqa_ref, wq_ref, gkv_ref, wkv_ref, gqn_ref, gkn_ref,
                ca_ref, sa_ref, cb_ref, sb_ref,
                qa_ref, ka_ref, va_ref, qb_ref, kb_ref, vb_ref):
    x = x_ref[0]
    h = _rms(x, gattn_ref[...]).astype(BF16)
    proj = jnp.dot(h, win_ref[...], preferred_element_type=F32)
    o = 0
    q_lat = proj[:, o:o + Q_LORA]; o += Q_LORA
    kv_lat = proj[:, o:o + KV_LORA]; o += KV_LORA
    k_rope = proj[:, o:o + LANES]; o += LANES
    gq = proj[:, o:o + GQA_HEADS * HEAD_DIM]; o += GQA_HEADS * HEAD_DIM
    gk = proj[:, o:o + GQA_KV_HEADS * HEAD_DIM]; o += GQA_KV_HEADS * HEAD_DIM
    gv = proj[:, o:o + GQA_KV_HEADS * HEAD_DIM]

    ca, sa, cb, sb = ca_ref[...], sa_ref[...], cb_ref[...], sb_ref[...]

    q = jnp.dot(_rms(q_lat, gqa_ref[...]).astype(BF16), wq_ref[...], preferred_element_type=F32)
    kv = jnp.dot(_rms(kv_lat, gkv_ref[...]).astype(BF16), wkv_ref[...], preferred_element_type=F32)
    k_pe = _rope(k_rope, ca, sa).astype(BF16)
    sc_a = np.float32((QK_NOPE + QK_ROPE) ** -0.5 * LOG2E)
    for hd in range(MLA_HEADS):
        qh = q[:, hd * MLA_DK:(hd + 1) * MLA_DK]
        qa_ref[0, hd, 0, 0:LANES, :] = (qh[:, 0:LANES] * sc_a).T.astype(BF16)
        qa_ref[0, hd, 0, LANES:MLA_DK, :] = (_rope(qh[:, LANES:MLA_DK], ca, sa) * sc_a).T.astype(BF16)
        kvh = kv[:, hd * 2 * LANES:(hd + 1) * 2 * LANES]
        ka_ref[0, hd, :, 0:LANES] = kvh[:, 0:LANES].astype(BF16)
        ka_ref[0, hd, :, LANES:MLA_DK] = k_pe
        va_ref[0, hd, 0] = kvh[:, LANES:2 * LANES].T.astype(BF16)

    sc_b = np.float32(HEAD_DIM ** -0.5 * LOG2E)
    gqn, gkn = gqn_ref[...], gkn_ref[...]
    for hd in range(GQA_HEADS):
        qh = _rope(_rms(gq[:, hd * HEAD_DIM:(hd + 1) * HEAD_DIM], gqn), cb, sb)
        qb_ref[0, hd, 0] = (qh * sc_b).T.astype(BF16)
    for hd in range(GQA_KV_HEADS):
        kh = _rope(_rms(gk[:, hd * HEAD_DIM:(hd + 1) * HEAD_DIM], gkn), cb, sb)
        kb_ref[0, hd] = kh.astype(BF16)
        vb_ref[0, hd, 0] = gv[:, hd * HEAD_DIM:(hd + 1) * HEAD_DIM].T.astype(BF16)


def _pre_call(x, gattn, win, gqa, wq, gkv, wkv, gqn, gkn, ca, sa, cb, sb, tm):
    b, s, d = x.shape
    const = lambda shape: pl.BlockSpec(shape, lambda bi, si: (0,) * len(shape))
    tab = pl.BlockSpec((tm, LANES), lambda bi, si: (si, 0))
    head = lambda nh, w: pl.BlockSpec((1, nh, tm, w), lambda bi, si: (bi, 0, si, 0))
    head_t = lambda nh, w: pl.BlockSpec((1, nh, 1, w, tm), lambda bi, si: (bi, 0, si, 0, 0))
    return pl.pallas_call(
        _pre_kernel,
        grid=(b, s // tm),
        in_specs=[pl.BlockSpec((1, tm, d), lambda bi, si: (bi, si, 0)),
                  const(gattn.shape), const(win.shape), const(gqa.shape), const(wq.shape),
                  const(gkv.shape), const(wkv.shape), const(gqn.shape), const(gkn.shape),
                  tab, tab, tab, tab],
        out_specs=[head_t(MLA_HEADS, MLA_DK), head(MLA_HEADS, MLA_DK), head_t(MLA_HEADS, V_HEAD),
                   head_t(GQA_HEADS, HEAD_DIM), head(GQA_KV_HEADS, HEAD_DIM), head_t(GQA_KV_HEADS, HEAD_DIM)],
        out_shape=[jax.ShapeDtypeStruct((b, MLA_HEADS, s // tm, MLA_DK, tm), BF16),
                   jax.ShapeDtypeStruct((b, MLA_HEADS, s, MLA_DK), BF16),
                   jax.ShapeDtypeStruct((b, MLA_HEADS, s // tm, V_HEAD, tm), BF16),
                   jax.ShapeDtypeStruct((b, GQA_HEADS, s // tm, HEAD_DIM, tm), BF16),
                   jax.ShapeDtypeStruct((b, GQA_KV_HEADS, s, HEAD_DIM), BF16),
                   jax.ShapeDtypeStruct((b, GQA_KV_HEADS, s // tm, HEAD_DIM, tm), BF16)],
        compiler_params=_cparams(("parallel", "parallel")),
        name="pre_proj",
    )(x, gattn, win, gqa, wq, gkv, wkv, gqn, gkn, ca, sa, cb, sb)


def _flash_kernel(q_ref, k_ref, vt_ref, o_ref, acc_sc, s_sc, *, streams, tq, tk):
    s_len = k_ref.shape[2]
    dv, cw = vt_ref.shape[3], vt_ref.shape[4]
    qw = q_ref.shape[4]
    n_qc = tq // qw
    n_sub = tk // cw
    n_steps = s_len // tk
    unroll = 8 if n_steps % 16 == 0 else 4 if n_steps % 8 == 0 else 2
    n_trips = n_steps // unroll

    def scores(r, c0, j, slot):
        start = j * tk if isinstance(j, int) else pl.multiple_of(j * tk, tk)
        k = k_ref[0, 0, pl.ds(start, tk), :]
        st = jnp.concatenate([jnp.dot(k, q_ref[0, r, c0 + c], preferred_element_type=F32)
                              for c in range(n_qc)], axis=1)
        s_sc[slot] = st
        return jnp.max(st, axis=0, keepdims=True)

    ones = jnp.ones((BF16_ROWS, cw), BF16)

    def key_step(j, slot, m_prev, mx, next_scores):
        mx_next = next_scores()
        m_new = jnp.maximum(m_prev, mx)
        alpha = jnp.exp2(m_prev - m_new)
        pb = jnp.exp2(s_sc[slot] - m_new).astype(BF16)
        pv = None
        for c in range(n_sub):
            vt1 = jnp.concatenate([vt_ref[0, 0, j * n_sub + c], ones], axis=0)
            d = jnp.dot(vt1, pb[c * cw:(c + 1) * cw], preferred_element_type=F32)
            pv = d if pv is None else pv + d
        acc_sc[...] = alpha * acc_sc[...] + pv
        return m_new, mx_next

    mx = scores(streams[0][0], streams[0][1], 0, 0)
    for si, stream in enumerate(streams):
        r, c0, row, col = stream
        r_nx, c0_nx, j_nx = (streams[si + 1][0], streams[si + 1][1], 0) if si + 1 < len(streams) \
            else (r, c0, n_steps - 1)
        acc_sc[...] = jnp.zeros(acc_sc.shape, F32)

        def body(jj, carry, r=r, c0=c0, r_nx=r_nx, c0_nx=c0_nx, j_nx=j_nx):
            m_prev, mx = carry
            for u in range(unroll):
                j = jj * unroll + u
                if u + 1 < unroll:
                    nxt = functools.partial(scores, r, c0, j + 1, 1 - u % 2)
                else:
                    last = jj == n_trips - 1
                    nxt = functools.partial(scores, jnp.where(last, r_nx, r), jnp.where(last, c0_nx, c0),
                                            jnp.where(last, j_nx, j + 1), 1 - u % 2)
                m_prev, mx = key_step(j, u % 2, m_prev, mx, nxt)
            return m_prev, mx

        _, mx = lax.fori_loop(0, n_trips, body, (jnp.full((1, tq), -jnp.inf, F32), mx))
        out_t = acc_sc[0:dv, :] / acc_sc[dv:dv + 1, :]
        o_ref[0, row:row + tq, col:col + dv] = out_t.T.astype(o_ref.dtype)


def _flash_call(qt, k, vt, tq, n_tiles, tk, name):
    b, hq, _, dk, qw = qt.shape
    s = k.shape[2]
    hkv = k.shape[1]
    _, _, nchunk, dv, cw = vt.shape
    rep = hq // hkv
    tq_blk = tq * n_tiles
    assert tk % cw == 0 and (s // tk) % 2 == 0 and s % tq_blk == 0 and tq % qw == 0
    streams = tuple((r, ti * (tq // qw), ti * tq, r * dv) for r in range(rep) for ti in range(n_tiles))
    return pl.pallas_call(
        functools.partial(_flash_kernel, streams=streams, tq=tq, tk=tk),
        grid=(b, hkv, s // tq_blk),
        in_specs=[pl.BlockSpec((1, rep, tq_blk // qw, dk, qw), lambda bi, g, qi: (bi, g, qi, 0, 0)),
                  pl.BlockSpec((1, 1, s, dk), lambda bi, g, qi: (bi, g, 0, 0)),
                  pl.BlockSpec((1, 1, nchunk, dv, cw), lambda bi, g, qi: (bi, g, 0, 0, 0))],
        out_specs=pl.BlockSpec((1, tq_blk, rep * dv), lambda bi, g, qi: (bi, qi, g)),
        out_shape=jax.ShapeDtypeStruct((b, s, hq * dv), BF16),
        scratch_shapes=[pltpu.VMEM((dv + BF16_ROWS, tq), F32), pltpu.VMEM((2, tk, tq), F32)],
        compiler_params=_cparams(("parallel", "parallel", "arbitrary")),
        name=name,
    )(qt, k, vt)


def _post_kernel(*refs, bounds):
    n_set = len(bounds) - 1
    ins, (wout_ref, gffn_ref, wr_ref, br_ref, x1_ref, hp_ref, route_ref) = refs[:3 * n_set], refs[3 * n_set:]
    i = pl.program_id(0)
    for si in range(n_set):
        @pl.when((i >= bounds[si]) & (i < bounds[si + 1]))
        def _(si=si):
            _post_tile(*ins[3 * si:3 * si + 3], wout_ref, gffn_ref, wr_ref, br_ref, x1_ref, hp_ref, route_ref)


def _post_tile(oa_ref, ob_ref, x_ref, wout_ref, gffn_ref, wr_ref, br_ref, x1_ref, hp_ref, route_ref):
    for r0 in range(0, x_ref.shape[0], POST_SUB):
        rows = pl.ds(r0, POST_SUB)
        k = hp_ref.shape[0] // x_ref.shape[0]
        _post_rows(oa_ref.at[rows], ob_ref.at[rows], x_ref.at[rows], wout_ref, gffn_ref, wr_ref, br_ref,
                   x1_ref.at[rows], hp_ref.at[pl.ds(r0 * k, POST_SUB * k)], route_ref.at[rows])


def _post_rows(oa_ref, ob_ref, x_ref, wout_ref, gffn_ref, wr_ref, br_ref, x1_ref, hp_ref, route_ref):
    half = oa_ref.shape[-1]
    mix = jnp.dot(oa_ref[...], wout_ref[0:half, :], preferred_element_type=F32)
    mix = mix + jnp.dot(ob_ref[...], wout_ref[half:2 * half, :], preferred_element_type=F32)
    x1 = x_ref[...] + mix
    x1_ref[...] = x1
    h = _rms(x1, gffn_ref[...])

    h_hi = h.astype(BF16)
    h_hi32 = h_hi.astype(F32)
    _store_token_tiles(hp_ref, _pack_bf16_pairs(h))

    h_lo = (h - h_hi32).astype(BF16)
    lg2 = (jnp.dot(h_hi, wr_ref[...], preferred_element_type=F32)
           + jnp.dot(h_lo, wr_ref[...], preferred_element_type=F32))
    lg = lg2[:, 0:LANES] + lg2[:, LANES:2 * LANES] + br_ref[...]
    lane = lax.broadcasted_iota(jnp.int32, lg.shape, 1)
    neg = jnp.float32(-jnp.inf)
    big = jnp.int32(4 * LANES)
    gl = jnp.where(lane < N_GROUPS, lg, neg)
    gmax = jnp.max(gl, axis=-1, keepdims=True)
    gsum = jnp.sum(jnp.exp(gl - gmax), axis=-1, keepdims=True)
    g_w = 1.0 / gsum
    g_idx = jnp.min(jnp.where(gl == gmax, lane, big), axis=-1, keepdims=True)
    lo_lane = N_GROUPS + EXPERTS_PER_GROUP * g_idx
    el = jnp.where((lane >= lo_lane) & (lane < lo_lane + EXPERTS_PER_GROUP), lg, neg)
    m1 = jnp.max(el, axis=-1, keepdims=True)
    i1 = jnp.min(jnp.where(el == m1, lane, big), axis=-1, keepdims=True)
    el2 = jnp.where(lane == i1, neg, el)
    m2 = jnp.max(el2, axis=-1, keepdims=True)
    i2 = jnp.min(jnp.where(el2 == m2, lane, big), axis=-1, keepdims=True)
    e2 = jnp.exp(m2 - m1)
    w1 = 1.0 / (1.0 + e2)
    w2 = e2 / (1.0 + e2)
    out = jnp.where(lane == 0, (i1 - N_GROUPS).astype(F32),
          jnp.where(lane == 1, (i2 - N_GROUPS).astype(F32),
          jnp.where(lane == 2, g_w * w1,
          jnp.where(lane == 3, g_w * w2, 0.0))))
    route_ref[...] = out


def _post_call(sets, wout, gffn, wr, br, tm):
    d = sets[0][2].shape[1]
    half = sets[0][0].shape[1]
    n_tiles = [x.shape[0] // tm for _, _, x in sets]
    bounds = tuple(int(v) for v in np.concatenate([[0], np.cumsum(n_tiles)]))
    t = bounds[-1] * tm
    const = lambda shape: pl.BlockSpec(shape, lambda i: (0,) * len(shape))
    row = lambda w: pl.BlockSpec((tm, w), lambda i: (i, 0))
    in_specs, operands = [], []
    for si, (oa, ob, x) in enumerate(sets):
        pin = lambda w, lo=bounds[si], n=n_tiles[si]: pl.BlockSpec((tm, w), lambda i: (jnp.clip(i - lo, 0, n - 1), 0))
        in_specs += [pin(half), pin(half), pin(d)]
        operands += [oa, ob, x]
    return pl.pallas_call(
        functools.partial(_post_kernel, bounds=bounds),
        grid=(bounds[-1],),
        in_specs=in_specs + [const(wout.shape), const(gffn.shape), const(wr.shape), const(br.shape)],
        out_specs=[row(d), pl.BlockSpec((tm * WORD_TILES, LANES), lambda i: (i, 0)), row(LANES)],
        out_shape=[jax.ShapeDtypeStruct((t, d), F32),
                   jax.ShapeDtypeStruct((t * WORD_TILES, LANES), jnp.uint32),
                   jax.ShapeDtypeStruct((t, LANES), F32)],
        compiler_params=_cparams(("parallel",)),
        name="post_proj_router",
    )(*operands, wout, gffn, wr, br)


def _tile_copy(src_hbm, idx_ref, r, buf, sem):
    k = WORD_TILES
    return pltpu.make_async_copy(src_hbm.at[pl.ds(pl.multiple_of(idx_ref[0, 0, r], k), k)],
                                 buf.at[pl.ds(pl.multiple_of(r * k, k), k)], sem)


def _row_gather(src_hbm, idx_ref, buf, sem, n_rows):
    def body(r, carry):
        _tile_copy(src_hbm, idx_ref, r, buf, sem).start()
        return carry
    lax.fori_loop(0, n_rows, body, 0, unroll=GATHER_UNROLL)


def _moe_kernel(blk_e_ref, n_used_ref, tok_cur_ref, tok_next_ref, hp_hbm, wg_ref, wu_ref, wd_ref,
                out_ref, buf0, buf1, sem, wgb, wub, wdb):
    i = pl.program_id(0)
    tb = buf0.shape[0] // WORD_TILES
    dh = WORD_TILES * LANES
    n_used = n_used_ref[0]

    @pl.when((i == 0) | (blk_e_ref[i] != blk_e_ref[jnp.maximum(i - 1, 0)]))
    def _():
        wgb[...] = wg_ref[0].astype(BF16)
        wub[...] = wu_ref[0].astype(BF16)
        wdb[...] = wd_ref[0].astype(BF16)

    @pl.when(i == 0)
    def _():
        _row_gather(hp_hbm, tok_cur_ref, buf0, sem.at[0], tb)

    def wait_rows(buf, s_):
        pltpu.make_async_copy(hp_hbm.at[pl.ds(0, tb * WORD_TILES)], buf, sem.at[s_]).wait()

    def block(cur, cur_s, nxt, nxt_s):
        wait_rows(cur, cur_s)
        for r in range(tb):
            _tile_copy(hp_hbm, tok_next_ref, r, nxt, sem.at[nxt_s]).start(priority=r % DMA_QUEUES)
        x_lo, x_hi = (v.astype(BF16) for v in _unpack_bf16_pairs(_load_token_tiles(cur, tb, WORD_TILES)))
        g = jnp.dot(x_lo, wgb[0:dh, :], preferred_element_type=F32)
        g = g + jnp.dot(x_hi, wgb[dh:2 * dh, :], preferred_element_type=F32)
        u = jnp.dot(x_lo, wub[0:dh, :], preferred_element_type=F32)
        u = u + jnp.dot(x_hi, wub[dh:2 * dh, :], preferred_element_type=F32)
        hid = (g * jax.nn.sigmoid(g)) * u
        out = jnp.dot(hid.astype(BF16), wdb[...], preferred_element_type=F32)
        _store_token_tiles(out_ref, _pack_bf16_pairs(out))

    used = i < n_used
    even = i % 2 == 0

    @pl.when(used & even)
    def _():
        block(buf0, 0, buf1, 1)

    @pl.when(used & jnp.logical_not(even))
    def _():
        block(buf1, 1, buf0, 0)

    @pl.when((i == n_used) & even)
    def _():
        wait_rows(buf0, 0)

    @pl.when((i == n_used) & jnp.logical_not(even))
    def _():
        wait_rows(buf1, 1)

    @pl.when(i >= n_used)
    def _():
        out_ref[...] = jnp.zeros(out_ref.shape, out_ref.dtype)


def _moe_call(blk_e, n_used, pad_tok, hp, wg, wu, wd, tb):
    nb = pad_tok.shape[0]
    dh = WORD_TILES * LANES
    d = 2 * dh
    de = wg.shape[-1]
    smem_blk = lambda f: pl.BlockSpec((1, 1, tb), f, memory_space=pltpu.SMEM)
    wspec = lambda shape: pl.BlockSpec(shape, lambda i, be, nu: (be[i], 0, 0))
    grid_spec = pltpu.PrefetchScalarGridSpec(
        num_scalar_prefetch=2,
        grid=(nb,),
        in_specs=[smem_blk(lambda i, be, nu: (i, 0, 0)),
                  smem_blk(lambda i, be, nu: (jnp.minimum(i + 1, nb - 1), 0, 0)),
                  pl.BlockSpec(memory_space=pl.ANY),
                  wspec((1, d, de)), wspec((1, d, de)), wspec((1, de, d))],
        out_specs=pl.BlockSpec((tb * WORD_TILES, LANES), lambda i, be, nu: (i, 0)),
        scratch_shapes=[pltpu.VMEM((tb * WORD_TILES, LANES), jnp.uint32),
                        pltpu.VMEM((tb * WORD_TILES, LANES), jnp.uint32),
                        pltpu.SemaphoreType.DMA((2,)),
                        pltpu.VMEM((d, de), BF16), pltpu.VMEM((d, de), BF16), pltpu.VMEM((de, d), BF16)],
    )
    return pl.pallas_call(
        _moe_kernel,
        grid_spec=grid_spec,
        out_shape=jax.ShapeDtypeStruct((nb * tb * WORD_TILES, LANES), jnp.uint32),
        compiler_params=_cparams(("arbitrary",)),
        name="moe_experts",
    )(blk_e, n_used, pad_tok, pad_tok, hp, wg, wu, wd)


def _final_kernel(*refs, bounds):
    n_set = len(bounds) - 1
    dest_cur_ref, dest_next_ref, x1_ref, route_ref, rows_hbm, gfin_ref = refs[:6]
    y_refs = refs[6:6 + n_set]
    buf0, buf1, sem = refs[6 + n_set:]
    i = pl.program_id(0)
    last = i == pl.num_programs(0) - 1
    n_rows = buf0.shape[0] // WORD_TILES
    tm = n_rows // TOP_K

    @pl.when(i == 0)
    def _():
        _row_gather(rows_hbm, dest_cur_ref, buf0, sem.at[0], n_rows)

    def wait_rows(buf, s_):
        pltpu.make_async_copy(rows_hbm.at[pl.ds(0, n_rows * WORD_TILES)], buf, sem.at[s_]).wait()

    def block(cur, cur_s, nxt, nxt_s):
        wait_rows(cur, cur_s)
        for r in range(n_rows):
            _tile_copy(rows_hbm, dest_next_ref, r, nxt, sem.at[nxt_s]).start(priority=r % DMA_QUEUES)
        route = route_ref[...]
        g0, g1 = route[:, 2:3], route[:, 3:4]
        a_lo, a_hi = _unpack_bf16_pairs(_load_token_tiles(cur, tm, WORD_TILES))
        b_lo, b_hi = _unpack_bf16_pairs(_load_token_tiles(cur, tm, WORD_TILES, r0=tm))
        moe = jnp.concatenate([a_lo * g0 + b_lo * g1, a_hi * g0 + b_hi * g1], axis=1)
        y = _rms(x1_ref[...] + moe, gfin_ref[...])
        for si in range(n_set):
            @pl.when((i >= bounds[si]) & (i < bounds[si + 1]))
            def _(si=si):
                y_refs[si][...] = y

    even = i % 2 == 0

    @pl.when(even)
    def _():
        block(buf0, 0, buf1, 1)

    @pl.when(jnp.logical_not(even))
    def _():
        block(buf1, 1, buf0, 0)

    @pl.when(last & even)
    def _():
        wait_rows(buf1, 1)

    @pl.when(last & jnp.logical_not(even))
    def _():
        wait_rows(buf0, 0)


def _final_call(dest, x1, route, rows, gfin, tm, n_tiles):
    t, d = x1.shape
    nt = t // tm
    bounds = tuple(int(v) for v in np.concatenate([[0], np.cumsum(n_tiles)]))
    assert bounds[-1] == nt
    smem_blk = lambda f: pl.BlockSpec((1, 1, TOP_K * tm), f, memory_space=pltpu.SMEM)
    out_specs = [pl.BlockSpec((tm, d), lambda i, lo=bounds[si], n=n_tiles[si]: (jnp.clip(i - lo, 0, n - 1), 0))
                 for si in range(len(n_tiles))]
    return pl.pallas_call(
        functools.partial(_final_kernel, bounds=bounds),
        grid=(nt,),
        in_specs=[smem_blk(lambda i: (i, 0, 0)),
                  smem_blk(lambda i: (jnp.minimum(i + 1, nt - 1), 0, 0)),
                  pl.BlockSpec((tm, d), lambda i: (i, 0)),
                  pl.BlockSpec((tm, LANES), lambda i: (i, 0)),
                  pl.BlockSpec(memory_space=pl.ANY),
                  pl.BlockSpec(gfin.shape, lambda i: (0, 0))],
        out_specs=out_specs,
        out_shape=[jax.ShapeDtypeStruct((n * tm, d), F32) for n in n_tiles],
        scratch_shapes=[pltpu.VMEM((TOP_K * tm * WORD_TILES, LANES), jnp.uint32),
                        pltpu.VMEM((TOP_K * tm * WORD_TILES, LANES), jnp.uint32),
                        pltpu.SemaphoreType.DMA((2,))],
        compiler_params=_cparams(("arbitrary",)),
        name="combine_final_norm",
    )(dest, dest, x1, route, rows, gfin)


def _deinterleave(n):
    return np.concatenate([np.arange(0, n, 2), np.arange(1, n, 2)])


def _rope_pad_cols(base):
    pad = -np.ones(LANES // 4, np.int64)
    return np.concatenate([base + np.arange(0, QK_ROPE, 2), pad, base + np.arange(1, QK_ROPE, 2), pad])


def _take_cols(w, cols):
    w_ext = jnp.concatenate([w, jnp.zeros((w.shape[0], 1), w.dtype)], axis=1)
    return w_ext[:, np.where(cols < 0, w.shape[1], cols)]


def _prep_weights(w_in, w_q_up, q_norm, k_norm):
    o_kr = Q_LORA + KV_LORA
    o_gq = o_kr + QK_ROPE
    o_gk = o_gq + GQA_HEADS * HEAD_DIM
    o_gv = o_gk + GQA_KV_HEADS * HEAD_DIM
    cols = [np.arange(0, o_kr), _rope_pad_cols(o_kr)]
    for hd in range(GQA_HEADS):
        cols.append(o_gq + hd * HEAD_DIM + _deinterleave(HEAD_DIM))
    for hd in range(GQA_KV_HEADS):
        cols.append(o_gk + hd * HEAD_DIM + _deinterleave(HEAD_DIM))
    cols.append(np.arange(o_gv, o_gv + GQA_KV_HEADS * HEAD_DIM))
    win = _take_cols(w_in, np.concatenate(cols)).astype(BF16)

    qcols = []
    for hd in range(MLA_HEADS):
        base = hd * (QK_NOPE + QK_ROPE)
        qcols += [base + np.arange(QK_NOPE), _rope_pad_cols(base + QK_NOPE)]
    wq = _take_cols(w_q_up, np.concatenate(qcols)).astype(BF16)
    perm = _deinterleave(HEAD_DIM)
    return win, wq, q_norm[perm][None, :], k_norm[perm][None, :]


def _rope_tables(s):
    rows = s // GRID_W
    row = np.repeat(np.arange(rows, dtype=np.float64), GRID_W)
    col = np.tile(np.arange(GRID_W, dtype=np.float64), rows)

    def cos_sin(rot_dim):
        n_pairs = rot_dim // 4
        freqs = ROPE_THETA ** (-np.arange(n_pairs, dtype=np.float64) * 2.0 / (rot_dim // 2))
        ang = np.concatenate([row[:, None] * freqs[None, :], col[:, None] * freqs[None, :]], axis=-1)
        return np.cos(ang), np.sin(ang)

    ca, sa = cos_sin(QK_ROPE)
    cb, sb = cos_sin(HEAD_DIM)
    tabs = (np.concatenate([ca, ca, ca, ca], -1), np.concatenate([-sa, -sa, sa, sa], -1),
            np.concatenate([cb, cb], -1), np.concatenate([-sb, sb], -1))
    return tuple(jnp.asarray(t.astype(np.float32)) for t in tabs)


def _pick(n, pref):
    return pref if n % pref == 0 else n


def _route_plan(route, tb):
    t = route.shape[0]
    a = t * TOP_K
    expert_id = route[:, 0:TOP_K].astype(jnp.int32).reshape(a)
    onehot = (expert_id[:, None] == jnp.arange(N_EXPERTS, dtype=jnp.int32)[None, :]).astype(jnp.int32)
    csum = jnp.cumsum(onehot, axis=0)
    counts = csum[-1]
    rank = jnp.sum(onehot * csum, axis=1) - 1
    padded = ((counts + tb - 1) // tb) * tb
    pad_ends = jnp.cumsum(padded)
    pad_off = pad_ends - padded
    dest = pad_off[expert_id] + rank
    nb = (a + N_EXPERTS * tb) // tb
    blk_start = jnp.arange(nb, dtype=jnp.int32) * tb
    blk_e = jnp.minimum(jnp.sum(blk_start[:, None] >= pad_ends[None, :], axis=1), N_EXPERTS - 1)
    n_used = (pad_ends[-1] // tb).astype(jnp.int32).reshape(1)
    order = jnp.sort(expert_id * a + jnp.arange(a, dtype=jnp.int32)) % a
    row_e = jnp.repeat(blk_e, tb)
    row_rank = jnp.arange(nb * tb, dtype=jnp.int32) - pad_off[row_e]
    offsets = jnp.cumsum(counts) - counts
    src = jnp.clip(offsets[row_e] + row_rank, 0, a - 1)
    valid = (row_rank < counts[row_e]) & (jnp.arange(nb * tb, dtype=jnp.int32) < pad_ends[-1])
    pad_tok = jnp.where(valid, order[src] // TOP_K, 0)
    last_e = blk_e[jnp.maximum(n_used[0] - 1, 0)]
    blk_e = jnp.where(jnp.arange(nb) < n_used[0], blk_e, last_e).astype(jnp.int32)
    return (pad_tok * WORD_TILES).reshape(nb, 1, tb), (dest * WORD_TILES).reshape(t, TOP_K), blk_e, n_used


def _attention(x, p, tm):
    b, s, d = x.shape
    t = b * s
    qa, ka, va, qb, kb, vb = _pre_call(x, p["gattn"], p["win"], p["gqa"], p["wq"], p["gkv"], p["wkv"],
                                       p["gqn"], p["gkn"], *_rope_tables(s), tm)
    tk = 2 * tm if s % (4 * tm) == 0 else s // 2
    oa = _flash_call(qa, ka, va, _pick(s, 1024), 2 if s % 2048 == 0 else 1, tk, "mla_flash")
    ob = _flash_call(qb, kb, vb, _pick(s, 512), 1, tk, "gqa_flash")
    return oa.reshape(t, -1), ob.reshape(t, -1), x.reshape(t, d)


def _channel_mix(sets, p, tm):
    x1, hp, route = _post_call(sets, p["wout"], p["gffn"], p["wr"], p["br"], 2 * POST_SUB)
    t = x1.shape[0]
    tb = 256
    pad_tok, dest, blk_e, n_used = _route_plan(route, tb)
    rows = _moe_call(blk_e, n_used, pad_tok, hp, p["wg"], p["wu"], p["wd"], tb)
    nt = t // tm
    dest_blk = dest.reshape(nt, tm, TOP_K).transpose(0, 2, 1).reshape(nt, 1, TOP_K * tm)
    return _final_call(dest_blk, x1, route, rows, p["gfin"], tm, [x.shape[0] // tm for _, _, x in sets])


def kernel(x_prompt, x_sample, attn_norm, w_in, q_a_norm, w_q_up, kv_a_norm, w_kv_up, q_norm, k_norm, w_out,
           ffn_norm, w_group, b_group, w_expert, b_expert, w_gate, w_up, w_down, final_norm):
    assert attn_norm.shape[0] == 1, "single-layer trunk"
    win, wq, gqn, gkn = _prep_weights(w_in[0], w_q_up[0], q_norm[0], k_norm[0])
    d = w_in.shape[1]
    wr = jnp.zeros((d, LANES), F32).at[:, 0:N_GROUPS].set(w_group[0])
    wr = wr.at[:, N_GROUPS:N_GROUPS + N_EXPERTS].set(w_expert[0])
    br = jnp.zeros((1, LANES), F32).at[0, 0:N_GROUPS].set(b_group[0])
    br = br.at[0, N_GROUPS:N_GROUPS + N_EXPERTS].set(b_expert[0])
    wr_hi = wr.astype(BF16)
    wr = jnp.concatenate([wr_hi, (wr - wr_hi.astype(F32)).astype(BF16)], axis=1)
    p = dict(gattn=attn_norm, win=win, gqa=q_a_norm, wq=wq, gkv=kv_a_norm, wkv=w_kv_up[0].astype(BF16),
             gqn=gqn, gkn=gkn, wout=w_out[0].astype(BF16), gffn=ffn_norm, wr=wr, br=br,
             wg=w_gate[0], wu=w_up[0], wd=w_down[0],
             gfin=final_norm[None, :])
    xs = (x_prompt, x_sample)
    tm = 256
    assert all(x.shape[1] % (2 * POST_SUB) == 0 for x in xs)
    ys = _channel_mix([_attention(x, p, tm) for x in xs], p, tm)
    return tuple(y.reshape(x.shape) for y, x in zip(ys, xs))
```

```python
import functools
import math

import numpy as np
import jax
import jax.numpy as jnp
from jax import lax
from jax.experimental import pallas as pl
from jax.experimental.pallas import tpu as pltpu

F32 = jnp.float32
BF16 = jnp.bfloat16

EPS = 1e-6
ROPE_THETA = 10000.0
GRID_W = 64
LANES = 128
BF16_ROWS = 16
GATHER_UNROLL = 8
TOKEN_TILE = 256
POST_SUB = 256
EXPERT_ROWS = 256
MLA_TQ, MLA_Q_TILES = 1024, 2
GQA_TQ = 512
DMA_QUEUES = 2
WORD_TILES = 8

MLA_HEADS = 8
Q_LORA = 512
KV_LORA = 256
QK_NOPE = 128
QK_ROPE = 64
V_HEAD = 128
MLA_DK = 256
GQA_HEADS = 8
GQA_KV_HEADS = 2
GQA_REP = GQA_HEADS // GQA_KV_HEADS
HEAD_DIM = 128
N_GROUPS = 4
EXPERTS_PER_GROUP = 8
N_EXPERTS = N_GROUPS * EXPERTS_PER_GROUP
TOP_K = 2
LOG2E = math.log2(math.e)

VMEM_LIMIT = 56 * 1024 * 1024


def _cparams(sem):
    return pltpu.CompilerParams(dimension_semantics=sem, vmem_limit_bytes=VMEM_LIMIT)


def _rms(x, gain):
    return x * lax.rsqrt(jnp.mean(x * x, axis=-1, keepdims=True) + EPS) * gain


def _rope(x, cos_t, sin_t):
    return x * cos_t + pltpu.roll(x, LANES // 2, axis=1) * sin_t


def _pack_bf16_pairs(v):
    n = v.shape[-1] // 2
    w = pltpu.bitcast(v.astype(BF16).astype(F32), jnp.uint32)
    return w[:, n:2 * n] | (w[:, 0:n] >> 16)


def _store_token_tiles(ref, w):
    rows, n = w.shape
    k = n // LANES
    for c in range(k):
        ref[pl.ds(c, rows, stride=k), :] = w[:, c * LANES:(c + 1) * LANES]


def _load_token_tiles(ref, rows, k, r0=0):
    return jnp.concatenate([ref[pl.ds(r0 * k + c, rows, stride=k), :] for c in range(k)], axis=1)


def _unpack_bf16_pairs(w):
    return pltpu.bitcast(w << 16, F32), pltpu.bitcast(w & jnp.uint32(0xFFFF0000), F32)


def _pre_kernel(x_ref, gattn_ref, win_ref, gqa_ref, wq_ref, gkv_ref, wkv_ref, gqn_ref, gkn_ref,
                ca_ref, sa_ref, cb_ref, sb_ref,
                qa_ref, ka_ref, va_ref, qb_ref, kb_ref, vb_ref):
    x = x_ref[0]
    h = _rms(x, gattn_ref[...]).astype(BF16)
    proj = jnp.dot(h, win_ref[...], preferred_element_type=F32)
    o = 0
    q_lat = proj[:, o:o + Q_LORA]; o += Q_LORA
    kv_lat = proj[:, o:o + KV_LORA]; o += KV_LORA
    k_rope = proj[:, o:o + LANES]; o += LANES
    gq = proj[:, o:o + GQA_HEADS * HEAD_DIM]; o += GQA_HEADS * HEAD_DIM
    gk = proj[:, o:o + GQA_KV_HEADS * HEAD_DIM]; o += GQA_KV_HEADS * HEAD_DIM
    gv = proj[:, o:o + GQA_KV_HEADS * HEAD_DIM]

    ca, sa, cb, sb = ca_ref[...], sa_ref[...], cb_ref[...], sb_ref[...]

    q = jnp.dot(_rms(q_lat, gqa_ref[...]).astype(BF16), wq_ref[...], preferred_element_type=F32)
    kv = jnp.dot(_rms(kv_lat, gkv_ref[...]).astype(BF16), wkv_ref[...], preferred_element_type=F32)
    k_pe = _rope(k_rope, ca, sa).astype(BF16)
    sc_a = np.float32((QK_NOPE + QK_ROPE) ** -0.5 * LOG2E)
    for hd in range(MLA_HEADS):
        qh = q[:, hd * MLA_DK:(hd + 1) * MLA_DK]
        qa_ref[0, hd, 0, 0:LANES, :] = (qh[:, 0:LANES] * sc_a).T.astype(BF16)
        qa_ref[0, hd, 0, LANES:MLA_DK, :] = (_rope(qh[:, LANES:MLA_DK], ca, sa) * sc_a).T.astype(BF16)
        kvh = kv[:, hd * 2 * LANES:(hd + 1) * 2 * LANES]
        ka_ref[0, hd, :, 0:LANES] = kvh[:, 0:LANES].astype(BF16)
        ka_ref[0, hd, :, LANES:MLA_DK] = k_pe
        va_ref[0, hd, 0] = kvh[:, LANES:2 * LANES].T.astype(BF16)

    sc_b = np.float32(HEAD_DIM ** -0.5 * LOG2E)
    gqn, gkn = gqn_ref[...], gkn_ref[...]
    for hd in range(GQA_HEADS):
        qh = _rope(_rms(gq[:, hd * HEAD_DIM:(hd + 1) * HEAD_DIM], gqn), cb, sb)
        qb_ref[0, hd, 0] = (qh * sc_b).T.astype(BF16)
    for hd in range(GQA_KV_HEADS):
        kh = _rope(_rms(gk[:, hd * HEAD_DIM:(hd + 1) * HEAD_DIM], gkn), cb, sb)
        kb_ref[0, hd] = kh.astype(BF16)
        vb_ref[0, hd, 0] = gv[:, hd * HEAD_DIM:(hd + 1) * HEAD_DIM].T.astype(BF16)


def _pre_call(x, gattn, win, gqa, wq, gkv, wkv, gqn, gkn, ca, sa, cb, sb, tm):
    b, s, d = x.shape
    const = lambda shape: pl.BlockSpec(shape, lambda bi, si: (0,) * len(shape))
    tab = pl.BlockSpec((tm, LANES), lambda bi, si: (si, 0))
    head = lambda nh, w: pl.BlockSpec((1, nh, tm, w), lambda bi, si: (bi, 0, si, 0))
    head_t = lambda nh, w: pl.BlockSpec((1, nh, 1, w, tm), lambda bi, si: (bi, 0, si, 0, 0))
    return pl.pallas_call(
        _pre_kernel,
        grid=(b, s // tm),
        in_specs=[pl.BlockSpec((1, tm, d), lambda bi, si: (bi, si, 0)),
                  const(gattn.shape), const(win.shape), const(gqa.shape), const(wq.shape),
                  const(gkv.shape), const(wkv.shape), const(gqn.shape), const(gkn.shape),
                  tab, tab, tab, tab],
        out_specs=[head_t(MLA_HEADS, MLA_DK), head(MLA_HEADS, MLA_DK), head_t(MLA_HEADS, V_HEAD),
                   head_t(GQA_HEADS, HEAD_DIM), head(GQA_KV_HEADS, HEAD_DIM), head_t(GQA_KV_HEADS, HEAD_DIM)],
        out_shape=[jax.ShapeDtypeStruct((b, MLA_HEADS, s // tm, MLA_DK, tm), BF16),
                   jax.ShapeDtypeStruct((b, MLA_HEADS, s, MLA_DK), BF16),
                   jax.ShapeDtypeStruct((b, MLA_HEADS, s // tm, V_HEAD, tm), BF16),
                   jax.ShapeDtypeStruct((b, GQA_HEADS, s // tm, HEAD_DIM, tm), BF16),
                   jax.ShapeDtypeStruct((b, GQA_KV_HEADS, s, HEAD_DIM), BF16),
                   jax.ShapeDtypeStruct((b, GQA_KV_HEADS, s // tm, HEAD_DIM, tm), BF16)],
        compiler_params=_cparams(("parallel", "parallel")),
        name="pre_proj",
    )(x, gattn, win, gqa, wq, gkv, wkv, gqn, gkn, ca, sa, cb, sb)


def _flash_kernel(q_ref, k_ref, vt_ref, o_ref, acc_sc, s_sc, *, streams, tq, tk):
    s_len = k_ref.shape[2]
    dv, cw = vt_ref.shape[3], vt_ref.shape[4]
    qw = q_ref.shape[4]
    n_qc = tq // qw
    n_sub = tk // cw
    n_steps = s_len // tk
    unroll = 8 if n_steps % 16 == 0 else 4 if n_steps % 8 == 0 else 2
    n_trips = n_steps // unroll

    def scores(r, c0, j, slot):
        start = j * tk if isinstance(j, int) else pl.multiple_of(j * tk, tk)
        k = k_ref[0, 0, pl.ds(start, tk), :]
        st = jnp.concatenate([jnp.dot(k, q_ref[0, r, c0 + c], preferred_element_type=F32)
                              for c in range(n_qc)], axis=1)
        s_sc[slot] = st
        return jnp.max(st, axis=0, keepdims=True)

    ones = jnp.ones((BF16_ROWS, cw), BF16)

    def key_step(j, slot, m_prev, mx, next_scores):
        mx_next = next_scores()
        m_new = jnp.maximum(m_prev, mx)
        alpha = jnp.exp2(m_prev - m_new)
        pb = jnp.exp2(s_sc[slot] - m_new).astype(BF16)
        pv = None
        for c in range(n_sub):
            vt1 = jnp.concatenate([vt_ref[0, 0, j * n_sub + c], ones], axis=0)
            d = jnp.dot(vt1, pb[c * cw:(c + 1) * cw], preferred_element_type=F32)
            pv = d if pv is None else pv + d
        acc_sc[...] = alpha * acc_sc[...] + pv
        return m_new, mx_next

    mx = scores(streams[0][0], streams[0][1], 0, 0)
    for si, stream in enumerate(streams):
        r, c0, row, col = stream
        r_nx, c0_nx, j_nx = (streams[si + 1][0], streams[si + 1][1], 0) if si + 1 < len(streams) \
            else (r, c0, n_steps - 1)
        acc_sc[...] = jnp.zeros(acc_sc.shape, F32)

        def body(jj, carry, r=r, c0=c0, r_nx=r_nx, c0_nx=c0_nx, j_nx=j_nx):
            m_prev, mx = carry
            for u in range(unroll):
                j = jj * unroll + u
                if u + 1 < unroll:
                    nxt = functools.partial(scores, r, c0, j + 1, 1 - u % 2)
                else:
                    last = jj == n_trips - 1
                    nxt = functools.partial(scores, jnp.where(last, r_nx, r), jnp.where(last, c0_nx, c0),
                                            jnp.where(last, j_nx, j + 1), 1 - u % 2)
                m_prev, mx = key_step(j, u % 2, m_prev, mx, nxt)
            return m_prev, mx

        _, mx = lax.fori_loop(0, n_trips, body, (jnp.full((1, tq), -jnp.inf, F32), mx))
        out_t = acc_sc[0:dv, :] / acc_sc[dv:dv + 1, :]
        o_ref[0, row:row + tq, col:col + dv] = out_t.T.astype(o_ref.dtype)


def _flash_call(qt, k, vt, tq, n_tiles, tk, name):
    b, hq, _, dk, qw = qt.shape
    s = k.shape[2]
    hkv = k.shape[1]
    _, _, nchunk, dv, cw = vt.shape
    rep = hq // hkv
    tq_blk = tq * n_tiles
    assert tk % cw == 0 and (s // tk) % 2 == 0 and s % tq_blk == 0 and tq % qw == 0
    streams = tuple((r, ti * (tq // qw), ti * tq, r * dv) for r in range(rep) for ti in range(n_tiles))
    return pl.pallas_call(
        functools.partial(_flash_kernel, streams=streams, tq=tq, tk=tk),
        grid=(b, hkv, s // tq_blk),
        in_specs=[pl.BlockSpec((1, rep, tq_blk // qw, dk, qw), lambda bi, g, qi: (bi, g, qi, 0, 0)),
                  pl.BlockSpec((1, 1, s, dk), lambda bi, g, qi: (bi, g, 0, 0)),
                  pl.BlockSpec((1, 1, nchunk, dv, cw), lambda bi, g, qi: (bi, g, 0, 0, 0))],
        out_specs=pl.BlockSpec((1, tq_blk, rep * dv), lambda bi, g, qi: (bi, qi, g)),
        out_shape=jax.ShapeDtypeStruct((b, s, hq * dv), BF16),
        scratch_shapes=[pltpu.VMEM((dv + BF16_ROWS, tq), F32), pltpu.VMEM((2, tk, tq), F32)],
        compiler_params=_cparams(("parallel", "parallel", "arbitrary")),
        name=name,
    )(qt, k, vt)


def _post_kernel(*refs, bounds):
    n_set = len(bounds) - 1
    ins, (wout_ref, gffn_ref, wr_ref, br_ref, x1_ref, hp_ref, route_ref) = refs[:3 * n_set], refs[3 * n_set:]
    i = pl.program_id(0)
    for si in range(n_set):
        @pl.when((i >= bounds[si]) & (i < bounds[si + 1]))
        def _(si=si):
            _post_tile(*ins[3 * si:3 * si + 3], wout_ref, gffn_ref, wr_ref, br_ref, x1_ref, hp_ref, route_ref)


def _post_tile(oa_ref, ob_ref, x_ref, wout_ref, gffn_ref, wr_ref, br_ref, x1_ref, hp_ref, route_ref):
    for r0 in range(0, x_ref.shape[0], POST_SUB):
        rows = pl.ds(r0, POST_SUB)
        k = hp_ref.shape[0] // x_ref.shape[0]
        _post_rows(oa_ref.at[rows], ob_ref.at[rows], x_ref.at[rows], wout_ref, gffn_ref, wr_ref, br_ref,
                   x1_ref.at[rows], hp_ref.at[pl.ds(r0 * k, POST_SUB * k)], route_ref.at[rows])


def _post_rows(oa_ref, ob_ref, x_ref, wout_ref, gffn_ref, wr_ref, br_ref, x1_ref, hp_ref, route_ref):
    half = oa_ref.shape[-1]
    mix = jnp.dot(oa_ref[...], wout_ref[0:half, :], preferred_element_type=F32)
    mix = mix + jnp.dot(ob_ref[...], wout_ref[half:2 * half, :], preferred_element_type=F32)
    x1 = x_ref[...] + mix
    x1_ref[...] = x1
    h = _rms(x1, gffn_ref[...])

    h_hi = h.astype(BF16)
    h_hi32 = h_hi.astype(F32)
    _store_token_tiles(hp_ref, _pack_bf16_pairs(h))

    h_lo = (h - h_hi32).astype(BF16)
    lg2 = (jnp.dot(h_hi, wr_ref[...], preferred_element_type=F32)
           + jnp.dot(h_lo, wr_ref[...], preferred_element_type=F32))
    lg = lg2[:, 0:LANES] + lg2[:, LANES:2 * LANES] + br_ref[...]
    lane = lax.broadcasted_iota(jnp.int32, lg.shape, 1)
    neg = jnp.float32(-jnp.inf)
    big = jnp.int32(4 * LANES)
    gl = jnp.where(lane < N_GROUPS, lg, neg)
    gmax = jnp.max(gl, axis=-1, keepdims=True)
    gsum = jnp.sum(jnp.exp(gl - gmax), axis=-1, keepdims=True)
    g_w = 1.0 / gsum
    g_idx = jnp.min(jnp.where(gl == gmax, lane, big), axis=-1, keepdims=True)
    lo_lane = N_GROUPS + EXPERTS_PER_GROUP * g_idx
    el = jnp.where((lane >= lo_lane) & (lane < lo_lane + EXPERTS_PER_GROUP), lg, neg)
    m1 = jnp.max(el, axis=-1, keepdims=True)
    i1 = jnp.min(jnp.where(el == m1, lane, big), axis=-1, keepdims=True)
    el2 = jnp.where(lane == i1, neg, el)
    m2 = jnp.max(el2, axis=-1, keepdims=True)
    i2 = jnp.min(jnp.where(el2 == m2, lane, big), axis=-1, keepdims=True)
    e2 = jnp.exp(m2 - m1)
    w1 = 1.0 / (1.0 + e2)
    w2 = e2 / (1.0 + e2)
    out = jnp.where(lane == 0, (i1 - N_GROUPS).astype(F32),
          jnp.where(lane == 1, (i2 - N_GROUPS).astype(F32),
          jnp.where(lane == 2, g_w * w1,
          jnp.where(lane == 3, g_w * w2, 0.0))))
    route_ref[...] = out


def _post_call(sets, wout, gffn, wr, br, tm):
    d = sets[0][2].shape[1]
    half = sets[0][0].shape[1]
    n_tiles = [x.shape[0] // tm for _, _, x in sets]
    bounds = tuple(int(v) for v in np.concatenate([[0], np.cumsum(n_tiles)]))
    t = bounds[-1] * tm
    const = lambda shape: pl.BlockSpec(shape, lambda i: (0,) * len(shape))
    row = lambda w: pl.BlockSpec((tm, w), lambda i: (i, 0))
    in_specs, operands = [], []
    for si, (oa, ob, x) in enumerate(sets):
        pin = lambda w, lo=bounds[si], n=n_tiles[si]: pl.BlockSpec((tm, w), lambda i: (jnp.clip(i - lo, 0, n - 1), 0))
        in_specs += [pin(half), pin(half), pin(d)]
        operands += [oa, ob, x]
    return pl.pallas_call(
        functools.partial(_post_kernel, bounds=bounds),
        grid=(bounds[-1],),
        in_specs=in_specs + [const(wout.shape), const(gffn.shape), const(wr.shape), const(br.shape)],
        out_specs=[row(d), pl.BlockSpec((tm * WORD_TILES, LANES), lambda i: (i, 0)), row(LANES)],
        out_shape=[jax.ShapeDtypeStruct((t, d), F32),
                   jax.ShapeDtypeStruct((t * WORD_TILES, LANES), jnp.uint32),
                   jax.ShapeDtypeStruct((t, LANES), F32)],
        compiler_params=_cparams(("parallel",)),
        name="post_proj_router",
    )(*operands, wout, gffn, wr, br)


def _tile_copy(src_hbm, idx_ref, r, buf, sem):
    k = WORD_TILES
    return pltpu.make_async_copy(src_hbm.at[pl.ds(pl.multiple_of(idx_ref[0, 0, r], k), k)],
                                 buf.at[pl.ds(pl.multiple_of(r * k, k), k)], sem)


def _row_gather(src_hbm, idx_ref, buf, sem, n_rows):
    def body(r, carry):
        _tile_copy(src_hbm, idx_ref, r, buf, sem).start()
        return carry
    lax.fori_loop(0, n_rows, body, 0, unroll=GATHER_UNROLL)


def _moe_kernel(blk_e_ref, n_used_ref, tok_cur_ref, tok_next_ref, hp_hbm, wg_ref, wu_ref, wd_ref,
                out_ref, buf0, buf1, sem, wgb, wub, wdb):
    i = pl.program_id(0)
    tb = buf0.shape[0] // WORD_TILES
    dh = WORD_TILES * LANES
    n_used = n_used_ref[0]

    @pl.when((i == 0) | (blk_e_ref[i] != blk_e_ref[jnp.maximum(i - 1, 0)]))
    def _():
        wgb[...] = wg_ref[0].astype(BF16)
        wub[...] = wu_ref[0].astype(BF16)
        wdb[...] = wd_ref[0].astype(BF16)

    @pl.when(i == 0)
    def _():
        _row_gather(hp_hbm, tok_cur_ref, buf0, sem.at[0], tb)

    def wait_rows(buf, s_):
        pltpu.make_async_copy(hp_hbm.at[pl.ds(0, tb * WORD_TILES)], buf, sem.at[s_]).wait()

    def block(cur, cur_s, nxt, nxt_s):
        wait_rows(cur, cur_s)
        for r in range(tb):
            _tile_copy(hp_hbm, tok_next_ref, r, nxt, sem.at[nxt_s]).start(priority=r % DMA_QUEUES)
        x_lo, x_hi = (v.astype(BF16) for v in _unpack_bf16_pairs(_load_token_tiles(cur, tb, WORD_TILES)))
        g = jnp.dot(x_lo, wgb[0:dh, :], preferred_element_type=F32)
        g = g + jnp.dot(x_hi, wgb[dh:2 * dh, :], preferred_element_type=F32)
        u = jnp.dot(x_lo, wub[0:dh, :], preferred_element_type=F32)
        u = u + jnp.dot(x_hi, wub[dh:2 * dh, :], preferred_element_type=F32)
        hid = (g * jax.nn.sigmoid(g)) * u
        out = jnp.dot(hid.astype(BF16), wdb[...], preferred_element_type=F32)
        _store_token_tiles(out_ref, _pack_bf16_pairs(out))

    used = i < n_used
    even = i % 2 == 0

    @pl.when(used & even)
    def _():
        block(buf0, 0, buf1, 1)

    @pl.when(used & jnp.logical_not(even))
    def _():
        block(buf1, 1, buf0, 0)

    @pl.when((i == n_used) & even)
    def _():
        wait_rows(buf0, 0)

    @pl.when((i == n_used) & jnp.logical_not(even))
    def _():
        wait_rows(buf1, 1)

    @pl.when(i >= n_used)
    def _():
        out_ref[...] = jnp.zeros(out_ref.shape, out_ref.dtype)


def _moe_call(blk_e, n_used, pad_tok, hp, wg, wu, wd, tb):
    nb = pad_tok.shape[0]
    dh = WORD_TILES * LANES
    d = 2 * dh
    de = wg.shape[-1]
    smem_blk = lambda f: pl.BlockSpec((1, 1, tb), f, memory_space=pltpu.SMEM)
    wspec = lambda shape: pl.BlockSpec(shape, lambda i, be, nu: (be[i], 0, 0))
    grid_spec = pltpu.PrefetchScalarGridSpec(
        num_scalar_prefetch=2,
        grid=(nb,),
        in_specs=[smem_blk(lambda i, be, nu: (i, 0, 0)),
                  smem_blk(lambda i, be, nu: (jnp.minimum(i + 1, nb - 1), 0, 0)),
                  pl.BlockSpec(memory_space=pl.ANY),
                  wspec((1, d, de)), wspec((1, d, de)), wspec((1, de, d))],
        out_specs=pl.BlockSpec((tb * WORD_TILES, LANES), lambda i, be, nu: (i, 0)),
        scratch_shapes=[pltpu.VMEM((tb * WORD_TILES, LANES), jnp.uint32),
                        pltpu.VMEM((tb * WORD_TILES, LANES), jnp.uint32),
                        pltpu.SemaphoreType.DMA((2,)),
                        pltpu.VMEM((d, de), BF16), pltpu.VMEM((d, de), BF16), pltpu.VMEM((de, d), BF16)],
    )
    return pl.pallas_call(
        _moe_kernel,
        grid_spec=grid_spec,
        out_shape=jax.ShapeDtypeStruct((nb * tb * WORD_TILES, LANES), jnp.uint32),
        compiler_params=_cparams(("arbitrary",)),
        name="moe_experts",
    )(blk_e, n_used, pad_tok, pad_tok, hp, wg, wu, wd)


def _final_kernel(*refs, bounds):
    n_set = len(bounds) - 1
    dest_cur_ref, dest_next_ref, x1_ref, route_ref, rows_hbm, gfin_ref = refs[:6]
    y_refs = refs[6:6 + n_set]
    buf0, buf1, sem = refs[6 + n_set:]
    i = pl.program_id(0)
    last = i == pl.num_programs(0) - 1
    n_rows = buf0.shape[0] // WORD_TILES
    tm = n_rows // TOP_K

    @pl.when(i == 0)
    def _():
        _row_gather(rows_hbm, dest_cur_ref, buf0, sem.at[0], n_rows)

    def wait_rows(buf, s_):
        pltpu.make_async_copy(rows_hbm.at[pl.ds(0, n_rows * WORD_TILES)], buf, sem.at[s_]).wait()

    def block(cur, cur_s, nxt, nxt_s):
        wait_rows(cur, cur_s)
        for r in range(n_rows):
            _tile_copy(rows_hbm, dest_next_ref, r, nxt, sem.at[nxt_s]).start(priority=r % DMA_QUEUES)
        route = route_ref[...]
        g0, g1 = route[:, 2:3], route[:, 3:4]
        a_lo, a_hi = _unpack_bf16_pairs(_load_token_tiles(cur, tm, WORD_TILES))
        b_lo, b_hi = _unpack_bf16_pairs(_load_token_tiles(cur, tm, WORD_TILES, r0=tm))
        moe = jnp.concatenate([a_lo * g0 + b_lo * g1, a_hi * g0 + b_hi * g1], axis=1)
        y = _rms(x1_ref[...] + moe, gfin_ref[...])
        for si in range(n_set):
            @pl.when((i >= bounds[si]) & (i < bounds[si + 1]))
            def _(si=si):
                y_refs[si][...] = y

    even = i % 2 == 0

    @pl.when(even)
    def _():
        block(buf0, 0, buf1, 1)

    @pl.when(jnp.logical_not(even))
    def _():
        block(buf1, 1, buf0, 0)

    @pl.when(last & even)
    def _():
        wait_rows(buf1, 1)

    @pl.when(last & jnp.logical_not(even))
    def _():
        wait_rows(buf0, 0)


def _final_call(dest, x1, route, rows, gfin, tm, n_tiles):
    t, d = x1.shape
    nt = t // tm
    bounds = tuple(int(v) for v in np.concatenate([[0], np.cumsum(n_tiles)]))
    assert bounds[-1] == nt
    smem_blk = lambda f: pl.BlockSpec((1, 1, TOP_K * tm), f, memory_space=pltpu.SMEM)
    out_specs = [pl.BlockSpec((tm, d), lambda i, lo=bounds[si], n=n_tiles[si]: (jnp.clip(i - lo, 0, n - 1), 0))
                 for si in range(len(n_tiles))]
    return pl.pallas_call(
        functools.partial(_final_kernel, bounds=bounds),
        grid=(nt,),
        in_specs=[smem_blk(lambda i: (i, 0, 0)),
                  smem_blk(lambda i: (jnp.minimum(i + 1, nt - 1), 0, 0)),
                  pl.BlockSpec((tm, d), lambda i: (i, 0)),
                  pl.BlockSpec((tm, LANES), lambda i: (i, 0)),
                  pl.BlockSpec(memory_space=pl.ANY),
                  pl.BlockSpec(gfin.shape, lambda i: (0, 0))],
        out_specs=out_specs,
        out_shape=[jax.ShapeDtypeStruct((n * tm, d), F32) for n in n_tiles],
        scratch_shapes=[pltpu.VMEM((TOP_K * tm * WORD_TILES, LANES), jnp.uint32),
                        pltpu.VMEM((TOP_K * tm * WORD_TILES, LANES), jnp.uint32),
                        pltpu.SemaphoreType.DMA((2,))],
        compiler_params=_cparams(("arbitrary",)),
        name="combine_final_norm",
    )(dest, dest, x1, route, rows, gfin)


def _deinterleave(n):
    return np.concatenate([np.arange(0, n, 2), np.arange(1, n, 2)])


def _rope_pad_cols(base):
    pad = -np.ones(LANES // 4, np.int64)
    return np.concatenate([base + np.arange(0, QK_ROPE, 2), pad, base + np.arange(1, QK_ROPE, 2), pad])


def _take_cols(w, cols):
    w_ext = jnp.concatenate([w, jnp.zeros((w.shape[0], 1), w.dtype)], axis=1)
    return w_ext[:, np.where(cols < 0, w.shape[1], cols)]


def _prep_weights(w_in, w_q_up, q_norm, k_norm):
    o_kr = Q_LORA + KV_LORA
    o_gq = o_kr + QK_ROPE
    o_gk = o_gq + GQA_HEADS * HEAD_DIM
    o_gv = o_gk + GQA_KV_HEADS * HEAD_DIM
    cols = [np.arange(0, o_kr), _rope_pad_cols(o_kr)]
    for hd in range(GQA_HEADS):
        cols.append(o_gq + hd * HEAD_DIM + _deinterleave(HEAD_DIM))
    for hd in range(GQA_KV_HEADS):
        cols.append(o_gk + hd * HEAD_DIM + _deinterleave(HEAD_DIM))
    cols.append(np.arange(o_gv, o_gv + GQA_KV_HEADS * HEAD_DIM))
    win = _take_cols(w_in, np.concatenate(cols)).astype(BF16)

    qcols = []
    for hd in range(MLA_HEADS):
        base = hd * (QK_NOPE + QK_ROPE)
        qcols += [base + np.arange(QK_NOPE), _rope_pad_cols(base + QK_NOPE)]
    wq = _take_cols(w_q_up, np.concatenate(qcols)).astype(BF16)
    perm = _deinterleave(HEAD_DIM)
    return win, wq, q_norm[perm][None, :], k_norm[perm][None, :]


def _rope_tables(s):
    rows = s // GRID_W
    row = np.repeat(np.arange(rows, dtype=np.float64), GRID_W)
    col = np.tile(np.arange(GRID_W, dtype=np.float64), rows)

    def cos_sin(rot_dim):
        n_pairs = rot_dim // 4
        freqs = ROPE_THETA ** (-np.arange(n_pairs, dtype=np.float64) * 2.0 / (rot_dim // 2))
        ang = np.concatenate([row[:, None] * freqs[None, :], col[:, None] * freqs[None, :]], axis=-1)
        return np.cos(ang), np.sin(ang)

    ca, sa = cos_sin(QK_ROPE)
    cb, sb = cos_sin(HEAD_DIM)
    tabs = (np.concatenate([ca, ca, ca, ca], -1), np.concatenate([-sa, -sa, sa, sa], -1),
            np.concatenate([cb, cb], -1), np.concatenate([-sb, sb], -1))
    return tuple(jnp.asarray(t.astype(np.float32)) for t in tabs)


def _pick(n, pref):
    return pref if n % pref == 0 else n


def _route_plan(route, tb):
    t = route.shape[0]
    a = t * TOP_K
    expert_id = jnp.concatenate([route[:, c] for c in range(TOP_K)]).astype(jnp.int32)
    onehot = (expert_id[:, None] == jnp.arange(N_EXPERTS, dtype=jnp.int32)[None, :]).astype(jnp.int32)
    csum = jnp.cumsum(onehot, axis=0)
    counts = csum[-1]
    rank = jnp.sum(onehot * csum, axis=1) - 1
    padded = ((counts + tb - 1) // tb) * tb
    pad_ends = jnp.cumsum(padded)
    pad_off = pad_ends - padded
    dest = pad_off[expert_id] + rank
    nb = (a + N_EXPERTS * tb) // tb
    blk_start = jnp.arange(nb, dtype=jnp.int32) * tb
    blk_e = jnp.minimum(jnp.sum(blk_start[:, None] >= pad_ends[None, :], axis=1), N_EXPERTS - 1)
    n_used = (pad_ends[-1] // tb).astype(jnp.int32).reshape(1)
    order = jnp.sort(expert_id * a + jnp.arange(a, dtype=jnp.int32)) % a
    row_e = jnp.repeat(blk_e, tb)
    row_rank = jnp.arange(nb * tb, dtype=jnp.int32) - pad_off[row_e]
    offsets = jnp.cumsum(counts) - counts
    src = jnp.clip(offsets[row_e] + row_rank, 0, a - 1)
    valid = (row_rank < counts[row_e]) & (jnp.arange(nb * tb, dtype=jnp.int32) < pad_ends[-1])
    pad_tok = jnp.where(valid, order[src] % t, 0)
    last_e = blk_e[jnp.maximum(n_used[0] - 1, 0)]
    blk_e = jnp.where(jnp.arange(nb) < n_used[0], blk_e, last_e).astype(jnp.int32)
    return (pad_tok * WORD_TILES).reshape(nb, 1, tb), (dest * WORD_TILES).reshape(TOP_K, t), blk_e, n_used


def _attention(x, p, tm):
    b, s, d = x.shape
    t = b * s
    qa, ka, va, qb, kb, vb = _pre_call(x, p["gattn"], p["win"], p["gqa"], p["wq"], p["gkv"], p["wkv"],
                                       p["gqn"], p["gkn"], *_rope_tables(s), tm)
    tk = 2 * tm if s % (4 * tm) == 0 else s // 2
    tq_a, tq_b = _pick(s, MLA_TQ), _pick(s, GQA_TQ)
    oa = _flash_call(qa, ka, va, tq_a, MLA_Q_TILES if s % (MLA_Q_TILES * tq_a) == 0 else 1, tk, "mla_flash")
    ob = _flash_call(qb, kb, vb, tq_b, 1, tk, "gqa_flash")
    return oa.reshape(t, -1), ob.reshape(t, -1), x.reshape(t, d)


def _channel_mix(sets, p, tm):
    x1, hp, route = _post_call(sets, p["wout"], p["gffn"], p["wr"], p["br"], 2 * POST_SUB)
    t = x1.shape[0]
    pad_tok, dest, blk_e, n_used = _route_plan(route, EXPERT_ROWS)
    rows = _moe_call(blk_e, n_used, pad_tok, hp, p["wg"], p["wu"], p["wd"], EXPERT_ROWS)
    nt = t // tm
    dest_blk = dest.reshape(TOP_K, nt, tm).transpose(1, 0, 2).reshape(nt, 1, TOP_K * tm)
    return _final_call(dest_blk, x1, route, rows, p["gfin"], tm, [x.shape[0] // tm for _, _, x in sets])


def kernel(x_prompt, x_sample, attn_norm, w_in, q_a_norm, w_q_up, kv_a_norm, w_kv_up, q_norm, k_norm, w_out,
           ffn_norm, w_group, b_group, w_expert, b_expert, w_gate, w_up, w_down, final_norm):
    assert attn_norm.shape[0] == 1, "single-layer trunk"
    win, wq, gqn, gkn = _prep_weights(w_in[0], w_q_up[0], q_norm[0], k_norm[0])
    d = w_in.shape[1]
    assert d == 2 * WORD_TILES * LANES
    n_pad = LANES - N_GROUPS - N_EXPERTS
    wr = jnp.concatenate([w_group[0], w_expert[0], jnp.zeros((d, n_pad), F32)], axis=1)
    br = jnp.concatenate([b_group[0], b_expert[0], jnp.zeros((n_pad,), F32)])[None, :]
    wr_hi = wr.astype(BF16)
    wr = jnp.concatenate([wr_hi, (wr - wr_hi.astype(F32)).astype(BF16)], axis=1)
    p = dict(gattn=attn_norm, win=win, gqa=q_a_norm, wq=wq, gkv=kv_a_norm, wkv=w_kv_up[0].astype(BF16),
             gqn=gqn, gkn=gkn, wout=w_out[0].astype(BF16), gffn=ffn_norm, wr=wr, br=br,
             wg=w_gate[0], wu=w_up[0], wd=w_down[0],
             gfin=final_norm[None, :])
    xs = (x_prompt, x_sample)
    tm = TOKEN_TILE
    assert all(x.shape[1] % (2 * POST_SUB) == 0 for x in xs)
    ys = _channel_mix([_attention(x, p, tm) for x in xs], p, tm)
    return tuple(y.reshape(x.shape) for y, x in zip(ys, xs))
```

```python
import functools
import math

import numpy as np
import jax
import jax.numpy as jnp
from jax import lax
from jax.experimental import pallas as pl
from jax.experimental.pallas import tpu as pltpu

F32 = jnp.float32
BF16 = jnp.bfloat16

EPS = 1e-6
ROPE_THETA = 10000.0
GRID_W = 64
LANES = 128
BF16_ROWS = 16
GATHER_UNROLL = 8
TOKEN_TILE = 256
POST_SUB = 256
EXPERT_ROWS = 256
MLA_TQ, MLA_Q_TILES = 1024, 4
GQA_TQ, GQA_Q_TILES = 512, 2
DMA_QUEUES = 2
WORD_TILES = 8

MLA_HEADS = 8
Q_LORA = 512
KV_LORA = 256
QK_NOPE = 128
QK_ROPE = 64
V_HEAD = 128
MLA_DK = 256
GQA_HEADS = 8
GQA_KV_HEADS = 2
GQA_REP = GQA_HEADS // GQA_KV_HEADS
HEAD_DIM = 128
N_GROUPS = 4
EXPERTS_PER_GROUP = 8
N_EXPERTS = N_GROUPS * EXPERTS_PER_GROUP
TOP_K = 2
LOG2E = math.log2(math.e)

VMEM_LIMIT = 56 * 1024 * 1024


def _cparams(sem):
    return pltpu.CompilerParams(dimension_semantics=sem, vmem_limit_bytes=VMEM_LIMIT)


def _rms(x, gain):
    return x * lax.rsqrt(jnp.mean(x * x, axis=-1, keepdims=True) + EPS) * gain


def _rope(x, cos_t, sin_t):
    return x * cos_t + pltpu.roll(x, LANES // 2, axis=1) * sin_t


def _pack_bf16_pairs(v):
    n = v.shape[-1] // 2
    w = pltpu.bitcast(v.astype(BF16).astype(F32), jnp.uint32)
    return w[:, n:2 * n] | (w[:, 0:n] >> 16)


def _store_token_tiles(ref, w):
    rows, n = w.shape
    k = n // LANES
    for c in range(k):
        ref[pl.ds(c, rows, stride=k), :] = w[:, c * LANES:(c + 1) * LANES]


def _load_token_tiles(ref, rows, k, r0=0):
    return jnp.concatenate([ref[pl.ds(r0 * k + c, rows, stride=k), :] for c in range(k)], axis=1)


def _unpack_bf16_pairs(w):
    return pltpu.bitcast(w << 16, F32), pltpu.bitcast(w & jnp.uint32(0xFFFF0000), F32)


def _pre_kernel(x_ref, gattn_ref, win_ref, gqa_ref, wq_ref, gkv_ref, wkv_ref, gqn_ref, gkn_ref,
                ca_ref, sa_ref, cb_ref, sb_ref,
                qa_ref, ka_ref, va_ref, qb_ref, kb_ref, vb_ref):
    x = x_ref[0]
    h = _rms(x, gattn_ref[...]).astype(BF16)
    proj = jnp.dot(h, win_ref[...], preferred_element_type=F32)
    o = 0
    q_lat = proj[:, o:o + Q_LORA]; o += Q_LORA
    kv_lat = proj[:, o:o + KV_LORA]; o += KV_LORA
    k_rope = proj[:, o:o + LANES]; o += LANES
    gq = proj[:, o:o + GQA_HEADS * HEAD_DIM]; o += GQA_HEADS * HEAD_DIM
    gk = proj[:, o:o + GQA_KV_HEADS * HEAD_DIM]; o += GQA_KV_HEADS * HEAD_DIM
    gv = proj[:, o:o + GQA_KV_HEADS * HEAD_DIM]

    ca, sa, cb, sb = ca_ref[...], sa_ref[...], cb_ref[...], sb_ref[...]

    q = jnp.dot(_rms(q_lat, gqa_ref[...]).astype(BF16), wq_ref[...], preferred_element_type=F32)
    kv = jnp.dot(_rms(kv_lat, gkv_ref[...]).astype(BF16), wkv_ref[...], preferred_element_type=F32)
    k_pe = _rope(k_rope, ca, sa).astype(BF16)
    sc_a = np.float32((QK_NOPE + QK_ROPE) ** -0.5 * LOG2E)
    for hd in range(MLA_HEADS):
        qh = q[:, hd * MLA_DK:(hd + 1) * MLA_DK]
        qa_ref[0, hd, 0, 0:LANES, :] = (qh[:, 0:LANES] * sc_a).T.astype(BF16)
        qa_ref[0, hd, 0, LANES:MLA_DK, :] = (_rope(qh[:, LANES:MLA_DK], ca, sa) * sc_a).T.astype(BF16)
        kvh = kv[:, hd * 2 * LANES:(hd + 1) * 2 * LANES]
        ka_ref[0, hd, :, 0:LANES] = kvh[:, 0:LANES].astype(BF16)
        ka_ref[0, hd, :, LANES:MLA_DK] = k_pe
        va_ref[0, hd, 0] = kvh[:, LANES:2 * LANES].T.astype(BF16)

    sc_b = np.float32(HEAD_DIM ** -0.5 * LOG2E)
    gqn, gkn = gqn_ref[...], gkn_ref[...]
    for hd in range(GQA_HEADS):
        qh = _rope(_rms(gq[:, hd * HEAD_DIM:(hd + 1) * HEAD_DIM], gqn), cb, sb)
        qb_ref[0, hd, 0] = (qh * sc_b).T.astype(BF16)
    for hd in range(GQA_KV_HEADS):
        kh = _rope(_rms(gk[:, hd * HEAD_DIM:(hd + 1) * HEAD_DIM], gkn), cb, sb)
        kb_ref[0, hd] = kh.astype(BF16)
        vb_ref[0, hd, 0] = gv[:, hd * HEAD_DIM:(hd + 1) * HEAD_DIM].T.astype(BF16)


def _pre_call(x, gattn, win, gqa, wq, gkv, wkv, gqn, gkn, ca, sa, cb, sb, tm):
    b, s, d = x.shape
    const = lambda shape: pl.BlockSpec(shape, lambda bi, si: (0,) * len(shape))
    tab = pl.BlockSpec((tm, LANES), lambda bi, si: (si, 0))
    head = lambda nh, w: pl.BlockSpec((1, nh, tm, w), lambda bi, si: (bi, 0, si, 0))
    head_t = lambda nh, w: pl.BlockSpec((1, nh, 1, w, tm), lambda bi, si: (bi, 0, si, 0, 0))
    return pl.pallas_call(
        _pre_kernel,
        grid=(b, s // tm),
        in_specs=[pl.BlockSpec((1, tm, d), lambda bi, si: (bi, si, 0)),
                  const(gattn.shape), const(win.shape), const(gqa.shape), const(wq.shape),
                  const(gkv.shape), const(wkv.shape), const(gqn.shape), const(gkn.shape),
                  tab, tab, tab, tab],
        out_specs=[head_t(MLA_HEADS, MLA_DK), head(MLA_HEADS, MLA_DK), head_t(MLA_HEADS, V_HEAD),
                   head_t(GQA_HEADS, HEAD_DIM), head(GQA_KV_HEADS, HEAD_DIM), head_t(GQA_KV_HEADS, HEAD_DIM)],
        out_shape=[jax.ShapeDtypeStruct((b, MLA_HEADS, s // tm, MLA_DK, tm), BF16),
                   jax.ShapeDtypeStruct((b, MLA_HEADS, s, MLA_DK), BF16),
                   jax.ShapeDtypeStruct((b, MLA_HEADS, s // tm, V_HEAD, tm), BF16),
                   jax.ShapeDtypeStruct((b, GQA_HEADS, s // tm, HEAD_DIM, tm), BF16),
                   jax.ShapeDtypeStruct((b, GQA_KV_HEADS, s, HEAD_DIM), BF16),
                   jax.ShapeDtypeStruct((b, GQA_KV_HEADS, s // tm, HEAD_DIM, tm), BF16)],
        compiler_params=_cparams(("parallel", "parallel")),
        name="pre_proj",
    )(x, gattn, win, gqa, wq, gkv, wkv, gqn, gkn, ca, sa, cb, sb)


def _flash_kernel(q_ref, k_ref, vt_ref, o_ref, acc_sc, s_sc, *, streams, tq, tk):
    s_len = k_ref.shape[2]
    dv, cw = vt_ref.shape[3], vt_ref.shape[4]
    qw = q_ref.shape[4]
    n_qc = tq // qw
    n_sub = tk // cw
    n_steps = s_len // tk
    unroll = 8 if n_steps % 16 == 0 else 4 if n_steps % 8 == 0 else 2
    n_trips = n_steps // unroll

    def scores(r, c0, j, slot):
        start = j * tk if isinstance(j, int) else pl.multiple_of(j * tk, tk)
        k = k_ref[0, 0, pl.ds(start, tk), :]
        st = jnp.concatenate([jnp.dot(k, q_ref[0, r, c0 + c], preferred_element_type=F32)
                              for c in range(n_qc)], axis=1)
        s_sc[slot] = st
        return jnp.max(st, axis=0, keepdims=True)

    ones = jnp.ones((BF16_ROWS, cw), BF16)

    def key_step(j, slot, m_prev, mx, next_scores):
        mx_next = next_scores()
        m_new = jnp.maximum(m_prev, mx)
        alpha = jnp.exp2(m_prev - m_new)
        pb = jnp.exp2(s_sc[slot] - m_new).astype(BF16)
        pv = None
        for c in range(n_sub):
            vt1 = jnp.concatenate([vt_ref[0, 0, j * n_sub + c], ones], axis=0)
            d = jnp.dot(vt1, pb[c * cw:(c + 1) * cw], preferred_element_type=F32)
            pv = d if pv is None else pv + d
        acc_sc[...] = alpha * acc_sc[...] + pv
        return m_new, mx_next

    mx = scores(streams[0][0], streams[0][1], 0, 0)
    for si, stream in enumerate(streams):
        r, c0, row, col = stream
        r_nx, c0_nx, j_nx = (streams[si + 1][0], streams[si + 1][1], 0) if si + 1 < len(streams) \
            else (r, c0, n_steps - 1)
        acc_sc[...] = jnp.zeros(acc_sc.shape, F32)

        def body(jj, carry, r=r, c0=c0, r_nx=r_nx, c0_nx=c0_nx, j_nx=j_nx):
            m_prev, mx = carry
            for u in range(unroll):
                j = jj * unroll + u
                if u + 1 < unroll:
                    nxt = functools.partial(scores, r, c0, j + 1, 1 - u % 2)
                else:
                    last = jj == n_trips - 1
                    nxt = functools.partial(scores, jnp.where(last, r_nx, r), jnp.where(last, c0_nx, c0),
                                            jnp.where(last, j_nx, j + 1), 1 - u % 2)
                m_prev, mx = key_step(j, u % 2, m_prev, mx, nxt)
            return m_prev, mx

        _, mx = lax.fori_loop(0, n_trips, body, (jnp.full((1, tq), -jnp.inf, F32), mx))
        out_t = acc_sc[0:dv, :] / acc_sc[dv:dv + 1, :]
        o_ref[0, row:row + tq, col:col + dv] = out_t.T.astype(o_ref.dtype)


def _flash_call(qt, k, vt, tq, n_tiles, tk, name):
    b, hq, _, dk, qw = qt.shape
    s = k.shape[2]
    hkv = k.shape[1]
    _, _, nchunk, dv, cw = vt.shape
    rep = hq // hkv
    tq_blk = tq * n_tiles
    assert tk % cw == 0 and (s // tk) % 2 == 0 and s % tq_blk == 0 and tq % qw == 0
    streams = tuple((r, ti * (tq // qw), ti * tq, r * dv) for r in range(rep) for ti in range(n_tiles))
    return pl.pallas_call(
        functools.partial(_flash_kernel, streams=streams, tq=tq, tk=tk),
        grid=(b, hkv, s // tq_blk),
        in_specs=[pl.BlockSpec((1, rep, tq_blk // qw, dk, qw), lambda bi, g, qi: (bi, g, qi, 0, 0)),
                  pl.BlockSpec((1, 1, s, dk), lambda bi, g, qi: (bi, g, 0, 0)),
                  pl.BlockSpec((1, 1, nchunk, dv, cw), lambda bi, g, qi: (bi, g, 0, 0, 0))],
        out_specs=pl.BlockSpec((1, tq_blk, rep * dv), lambda bi, g, qi: (bi, qi, g)),
        out_shape=jax.ShapeDtypeStruct((b, s, hq * dv), BF16),
        scratch_shapes=[pltpu.VMEM((dv + BF16_ROWS, tq), F32), pltpu.VMEM((2, tk, tq), F32)],
        compiler_params=_cparams(("parallel", "parallel", "arbitrary")),
        name=name,
    )(qt, k, vt)


def _post_kernel(*refs, bounds):
    n_set = len(bounds) - 1
    ins, (wout_ref, gffn_ref, wr_ref, br_ref, x1_ref, hp_ref, route_ref) = refs[:3 * n_set], refs[3 * n_set:]
    i = pl.program_id(0)
    for si in range(n_set):
        @pl.when((i >= bounds[si]) & (i < bounds[si + 1]))
        def _(si=si):
            _post_tile(*ins[3 * si:3 * si + 3], wout_ref, gffn_ref, wr_ref, br_ref, x1_ref, hp_ref, route_ref)


def _post_tile(oa_ref, ob_ref, x_ref, wout_ref, gffn_ref, wr_ref, br_ref, x1_ref, hp_ref, route_ref):
    for r0 in range(0, x_ref.shape[0], POST_SUB):
        rows = pl.ds(r0, POST_SUB)
        k = hp_ref.shape[0] // x_ref.shape[0]
        _post_rows(oa_ref.at[rows], ob_ref.at[rows], x_ref.at[rows], wout_ref, gffn_ref, wr_ref, br_ref,
                   x1_ref.at[rows], hp_ref.at[pl.ds(r0 * k, POST_SUB * k)], route_ref.at[rows])


def _post_rows(oa_ref, ob_ref, x_ref, wout_ref, gffn_ref, wr_ref, br_ref, x1_ref, hp_ref, route_ref):
    half = oa_ref.shape[-1]
    mix = jnp.dot(oa_ref[...], wout_ref[0:half, :], preferred_element_type=F32)
    mix = mix + jnp.dot(ob_ref[...], wout_ref[half:2 * half, :], preferred_element_type=F32)
    x1 = x_ref[...] + mix
    x1_ref[...] = x1
    h = _rms(x1, gffn_ref[...])

    h_hi = h.astype(BF16)
    h_hi32 = h_hi.astype(F32)
    _store_token_tiles(hp_ref, _pack_bf16_pairs(h))

    h_lo = (h - h_hi32).astype(BF16)
    lg2 = (jnp.dot(h_hi, wr_ref[...], preferred_element_type=F32)
           + jnp.dot(h_lo, wr_ref[...], preferred_element_type=F32))
    lg = lg2[:, 0:LANES] + lg2[:, LANES:2 * LANES] + br_ref[...]
    lane = lax.broadcasted_iota(jnp.int32, lg.shape, 1)
    neg = jnp.float32(-jnp.inf)
    big = jnp.int32(4 * LANES)
    gl = jnp.where(lane < N_GROUPS, lg, neg)
    gmax = jnp.max(gl, axis=-1, keepdims=True)
    gsum = jnp.sum(jnp.exp(gl - gmax), axis=-1, keepdims=True)
    g_w = 1.0 / gsum
    g_idx = jnp.min(jnp.where(gl == gmax, lane, big), axis=-1, keepdims=True)
    lo_lane = N_GROUPS + EXPERTS_PER_GROUP * g_idx
    el = jnp.where((lane >= lo_lane) & (lane < lo_lane + EXPERTS_PER_GROUP), lg, neg)
    m1 = jnp.max(el, axis=-1, keepdims=True)
    i1 = jnp.min(jnp.where(el == m1, lane, big), axis=-1, keepdims=True)
    el2 = jnp.where(lane == i1, neg, el)
    m2 = jnp.max(el2, axis=-1, keepdims=True)
    i2 = jnp.min(jnp.where(el2 == m2, lane, big), axis=-1, keepdims=True)
    e2 = jnp.exp(m2 - m1)
    w1 = 1.0 / (1.0 + e2)
    w2 = e2 / (1.0 + e2)
    out = jnp.where(lane == 0, (i1 - N_GROUPS).astype(F32),
          jnp.where(lane == 1, (i2 - N_GROUPS).astype(F32),
          jnp.where(lane == 2, g_w * w1,
          jnp.where(lane == 3, g_w * w2, 0.0))))
    route_ref[...] = out


def _post_call(sets, wout, gffn, wr, br, tm):
    d = sets[0][2].shape[1]
    half = sets[0][0].shape[1]
    n_tiles = [x.shape[0] // tm for _, _, x in sets]
    bounds = tuple(int(v) for v in np.concatenate([[0], np.cumsum(n_tiles)]))
    t = bounds[-1] * tm
    const = lambda shape: pl.BlockSpec(shape, lambda i: (0,) * len(shape))
    row = lambda w: pl.BlockSpec((tm, w), lambda i: (i, 0))
    in_specs, operands = [], []
    for si, (oa, ob, x) in enumerate(sets):
        pin = lambda w, lo=bounds[si], n=n_tiles[si]: pl.BlockSpec((tm, w), lambda i: (jnp.clip(i - lo, 0, n - 1), 0))
        in_specs += [pin(half), pin(half), pin(d)]
        operands += [oa, ob, x]
    return pl.pallas_call(
        functools.partial(_post_kernel, bounds=bounds),
        grid=(bounds[-1],),
        in_specs=in_specs + [const(wout.shape), const(gffn.shape), const(wr.shape), const(br.shape)],
        out_specs=[row(d), pl.BlockSpec((tm * WORD_TILES, LANES), lambda i: (i, 0)), row(LANES)],
        out_shape=[jax.ShapeDtypeStruct((t, d), F32),
                   jax.ShapeDtypeStruct((t * WORD_TILES, LANES), jnp.uint32),
                   jax.ShapeDtypeStruct((t, LANES), F32)],
        compiler_params=_cparams(("parallel",)),
        name="post_proj_router",
    )(*operands, wout, gffn, wr, br)


def _tile_copy(src_hbm, idx_ref, r, buf, sem):
    k = WORD_TILES
    return pltpu.make_async_copy(src_hbm.at[pl.ds(pl.multiple_of(idx_ref[0, 0, r], k), k)],
                                 buf.at[pl.ds(pl.multiple_of(r * k, k), k)], sem)


def _row_gather(src_hbm, idx_ref, buf, sem, n_rows):
    def body(r, carry):
        _tile_copy(src_hbm, idx_ref, r, buf, sem).start()
        return carry
    lax.fori_loop(0, n_rows, body, 0, unroll=GATHER_UNROLL)


def _moe_kernel(blk_e_ref, n_used_ref, tok_cur_ref, tok_next_ref, hp_hbm, wg_ref, wu_ref, wd_ref,
                out_ref, buf0, buf1, sem, wgb, wub, wdb):
    i = pl.program_id(0)
    tb = buf0.shape[0] // WORD_TILES
    dh = WORD_TILES * LANES
    n_used = n_used_ref[0]

    @pl.when((i == 0) | (blk_e_ref[i] != blk_e_ref[jnp.maximum(i - 1, 0)]))
    def _():
        wgb[...] = wg_ref[0].astype(BF16)
        wub[...] = wu_ref[0].astype(BF16)
        wdb[...] = wd_ref[0].astype(BF16)

    @pl.when(i == 0)
    def _():
        _row_gather(hp_hbm, tok_cur_ref, buf0, sem.at[0], tb)

    def wait_rows(buf, s_):
        pltpu.make_async_copy(hp_hbm.at[pl.ds(0, tb * WORD_TILES)], buf, sem.at[s_]).wait()

    def block(cur, cur_s, nxt, nxt_s):
        wait_rows(cur, cur_s)
        for r in range(tb):
            _tile_copy(hp_hbm, tok_next_ref, r, nxt, sem.at[nxt_s]).start(priority=r % DMA_QUEUES)
        x_lo, x_hi = (v.astype(BF16) for v in _unpack_bf16_pairs(_load_token_tiles(cur, tb, WORD_TILES)))
        g = jnp.dot(x_lo, wgb[0:dh, :], preferred_element_type=F32)
        g = g + jnp.dot(x_hi, wgb[dh:2 * dh, :], preferred_element_type=F32)
        u = jnp.dot(x_lo, wub[0:dh, :], preferred_element_type=F32)
        u = u + jnp.dot(x_hi, wub[dh:2 * dh, :], preferred_element_type=F32)
        hid = (g * jax.nn.sigmoid(g)) * u
        out = jnp.dot(hid.astype(BF16), wdb[...], preferred_element_type=F32)
        _store_token_tiles(out_ref, _pack_bf16_pairs(out))

    used = i < n_used
    even = i % 2 == 0

    @pl.when(used & even)
    def _():
        block(buf0, 0, buf1, 1)

    @pl.when(used & jnp.logical_not(even))
    def _():
        block(buf1, 1, buf0, 0)

    @pl.when((i == n_used) & even)
    def _():
        wait_rows(buf0, 0)

    @pl.when((i == n_used) & jnp.logical_not(even))
    def _():
        wait_rows(buf1, 1)

    @pl.when(i >= n_used)
    def _():
        out_ref[...] = jnp.zeros(out_ref.shape, out_ref.dtype)


def _moe_call(blk_e, n_used, pad_tok, hp, wg, wu, wd, tb):
    nb = pad_tok.shape[0]
    dh = WORD_TILES * LANES
    d = 2 * dh
    de = wg.shape[-1]
    smem_blk = lambda f: pl.BlockSpec((1, 1, tb), f, memory_space=pltpu.SMEM)
    wspec = lambda shape: pl.BlockSpec(shape, lambda i, be, nu: (be[i], 0, 0))
    grid_spec = pltpu.PrefetchScalarGridSpec(
        num_scalar_prefetch=2,
        grid=(nb,),
        in_specs=[smem_blk(lambda i, be, nu: (i, 0, 0)),
                  smem_blk(lambda i, be, nu: (jnp.minimum(i + 1, nb - 1), 0, 0)),
                  pl.BlockSpec(memory_space=pl.ANY),
                  wspec((1, d, de)), wspec((1, d, de)), wspec((1, de, d))],
        out_specs=pl.BlockSpec((tb * WORD_TILES, LANES), lambda i, be, nu: (i, 0)),
        scratch_shapes=[pltpu.VMEM((tb * WORD_TILES, LANES), jnp.uint32),
                        pltpu.VMEM((tb * WORD_TILES, LANES), jnp.uint32),
                        pltpu.SemaphoreType.DMA((2,)),
                        pltpu.VMEM((d, de), BF16), pltpu.VMEM((d, de), BF16), pltpu.VMEM((de, d), BF16)],
    )
    return pl.pallas_call(
        _moe_kernel,
        grid_spec=grid_spec,
        out_shape=jax.ShapeDtypeStruct((nb * tb * WORD_TILES, LANES), jnp.uint32),
        compiler_params=_cparams(("arbitrary",)),
        name="moe_experts",
    )(blk_e, n_used, pad_tok, pad_tok, hp, wg, wu, wd)


def _final_kernel(*refs, bounds):
    n_set = len(bounds) - 1
    dest_cur_ref, dest_next_ref, x1_ref, route_ref, rows_hbm, gfin_ref = refs[:6]
    y_refs = refs[6:6 + n_set]
    buf0, buf1, sem = refs[6 + n_set:]
    i = pl.program_id(0)
    last = i == pl.num_programs(0) - 1
    n_rows = buf0.shape[0] // WORD_TILES
    tm = n_rows // TOP_K

    @pl.when(i == 0)
    def _():
        _row_gather(rows_hbm, dest_cur_ref, buf0, sem.at[0], n_rows)

    def wait_rows(buf, s_):
        pltpu.make_async_copy(rows_hbm.at[pl.ds(0, n_rows * WORD_TILES)], buf, sem.at[s_]).wait()

    def block(cur, cur_s, nxt, nxt_s):
        wait_rows(cur, cur_s)
        for r in range(n_rows):
            _tile_copy(rows_hbm, dest_next_ref, r, nxt, sem.at[nxt_s]).start(priority=r % DMA_QUEUES)
        route = route_ref[...]
        g0, g1 = route[:, 2:3], route[:, 3:4]
        a_lo, a_hi = _unpack_bf16_pairs(_load_token_tiles(cur, tm, WORD_TILES))
        b_lo, b_hi = _unpack_bf16_pairs(_load_token_tiles(cur, tm, WORD_TILES, r0=tm))
        moe = jnp.concatenate([a_lo * g0 + b_lo * g1, a_hi * g0 + b_hi * g1], axis=1)
        y = _rms(x1_ref[...] + moe, gfin_ref[...])
        for si in range(n_set):
            @pl.when((i >= bounds[si]) & (i < bounds[si + 1]))
            def _(si=si):
                y_refs[si][...] = y

    even = i % 2 == 0

    @pl.when(even)
    def _():
        block(buf0, 0, buf1, 1)

    @pl.when(jnp.logical_not(even))
    def _():
        block(buf1, 1, buf0, 0)

    @pl.when(last & even)
    def _():
        wait_rows(buf1, 1)

    @pl.when(last & jnp.logical_not(even))
    def _():
        wait_rows(buf0, 0)


def _final_call(dest, x1, route, rows, gfin, tm, n_tiles):
    t, d = x1.shape
    nt = t // tm
    bounds = tuple(int(v) for v in np.concatenate([[0], np.cumsum(n_tiles)]))
    assert bounds[-1] == nt
    smem_blk = lambda f: pl.BlockSpec((1, 1, TOP_K * tm), f, memory_space=pltpu.SMEM)
    out_specs = [pl.BlockSpec((tm, d), lambda i, lo=bounds[si], n=n_tiles[si]: (jnp.clip(i - lo, 0, n - 1), 0))
                 for si in range(len(n_tiles))]
    return pl.pallas_call(
        functools.partial(_final_kernel, bounds=bounds),
        grid=(nt,),
        in_specs=[smem_blk(lambda i: (i, 0, 0)),
                  smem_blk(lambda i: (jnp.minimum(i + 1, nt - 1), 0, 0)),
                  pl.BlockSpec((tm, d), lambda i: (i, 0)),
                  pl.BlockSpec((tm, LANES), lambda i: (i, 0)),
                  pl.BlockSpec(memory_space=pl.ANY),
                  pl.BlockSpec(gfin.shape, lambda i: (0, 0))],
        out_specs=out_specs,
        out_shape=[jax.ShapeDtypeStruct((n * tm, d), F32) for n in n_tiles],
        scratch_shapes=[pltpu.VMEM((TOP_K * tm * WORD_TILES, LANES), jnp.uint32),
                        pltpu.VMEM((TOP_K * tm * WORD_TILES, LANES), jnp.uint32),
                        pltpu.SemaphoreType.DMA((2,))],
        compiler_params=_cparams(("arbitrary",)),
        name="combine_final_norm",
    )(dest, dest, x1, route, rows, gfin)


def _deinterleave(n):
    return np.concatenate([np.arange(0, n, 2), np.arange(1, n, 2)])


def _rope_pad_cols(base):
    pad = -np.ones(LANES // 4, np.int64)
    return np.concatenate([base + np.arange(0, QK_ROPE, 2), pad, base + np.arange(1, QK_ROPE, 2), pad])


def _take_cols(w, cols):
    w_ext = jnp.concatenate([w, jnp.zeros((w.shape[0], 1), w.dtype)], axis=1)
    return w_ext[:, np.where(cols < 0, w.shape[1], cols)]


def _prep_weights(w_in, w_q_up, q_norm, k_norm):
    o_kr = Q_LORA + KV_LORA
    o_gq = o_kr + QK_ROPE
    o_gk = o_gq + GQA_HEADS * HEAD_DIM
    o_gv = o_gk + GQA_KV_HEADS * HEAD_DIM
    cols = [np.arange(0, o_kr), _rope_pad_cols(o_kr)]
    for hd in range(GQA_HEADS):
        cols.append(o_gq + hd * HEAD_DIM + _deinterleave(HEAD_DIM))
    for hd in range(GQA_KV_HEADS):
        cols.append(o_gk + hd * HEAD_DIM + _deinterleave(HEAD_DIM))
    cols.append(np.arange(o_gv, o_gv + GQA_KV_HEADS * HEAD_DIM))
    win = _take_cols(w_in, np.concatenate(cols)).astype(BF16)

    qcols = []
    for hd in range(MLA_HEADS):
        base = hd * (QK_NOPE + QK_ROPE)
        qcols += [base + np.arange(QK_NOPE), _rope_pad_cols(base + QK_NOPE)]
    wq = _take_cols(w_q_up, np.concatenate(qcols)).astype(BF16)
    perm = _deinterleave(HEAD_DIM)
    return win, wq, q_norm[perm][None, :], k_norm[perm][None, :]


def _rope_tables(s):
    rows = s // GRID_W
    row = np.repeat(np.arange(rows, dtype=np.float64), GRID_W)
    col = np.tile(np.arange(GRID_W, dtype=np.float64), rows)

    def cos_sin(rot_dim):
        n_pairs = rot_dim // 4
        freqs = ROPE_THETA ** (-np.arange(n_pairs, dtype=np.float64) * 2.0 / (rot_dim // 2))
        ang = np.concatenate([row[:, None] * freqs[None, :], col[:, None] * freqs[None, :]], axis=-1)
        return np.cos(ang), np.sin(ang)

    ca, sa = cos_sin(QK_ROPE)
    cb, sb = cos_sin(HEAD_DIM)
    tabs = (np.concatenate([ca, ca, ca, ca], -1), np.concatenate([-sa, -sa, sa, sa], -1),
            np.concatenate([cb, cb], -1), np.concatenate([-sb, sb], -1))
    return tuple(jnp.asarray(t.astype(np.float32)) for t in tabs)


def _pick(n, pref):
    return pref if n % pref == 0 else n


def _route_plan(route, tb):
    t = route.shape[0]
    a = t * TOP_K
    expert_id = jnp.concatenate([route[:, c] for c in range(TOP_K)]).astype(jnp.int32)
    onehot = (expert_id[:, None] == jnp.arange(N_EXPERTS, dtype=jnp.int32)[None, :]).astype(jnp.int32)
    csum = jnp.cumsum(onehot, axis=0)
    counts = csum[-1]
    rank = jnp.sum(onehot * csum, axis=1) - 1
    padded = ((counts + tb - 1) // tb) * tb
    pad_ends = jnp.cumsum(padded)
    pad_off = pad_ends - padded
    dest = pad_off[expert_id] + rank
    nb = (a + N_EXPERTS * tb) // tb
    blk_start = jnp.arange(nb, dtype=jnp.int32) * tb
    blk_e = jnp.minimum(jnp.sum(blk_start[:, None] >= pad_ends[None, :], axis=1), N_EXPERTS - 1)
    n_used = (pad_ends[-1] // tb).astype(jnp.int32).reshape(1)
    order = jnp.sort(expert_id * a + jnp.arange(a, dtype=jnp.int32)) % a
    row_e = jnp.repeat(blk_e, tb)
    row_rank = jnp.arange(nb * tb, dtype=jnp.int32) - pad_off[row_e]
    offsets = jnp.cumsum(counts) - counts
    src = jnp.clip(offsets[row_e] + row_rank, 0, a - 1)
    valid = (row_rank < counts[row_e]) & (jnp.arange(nb * tb, dtype=jnp.int32) < pad_ends[-1])
    pad_tok = jnp.where(valid, order[src] % t, 0)
    last_e = blk_e[jnp.maximum(n_used[0] - 1, 0)]
    blk_e = jnp.where(jnp.arange(nb) < n_used[0], blk_e, last_e).astype(jnp.int32)
    return (pad_tok * WORD_TILES).reshape(nb, 1, tb), (dest * WORD_TILES).reshape(TOP_K, t), blk_e, n_used


def _attention(x, p, tm):
    b, s, d = x.shape
    t = b * s
    qa, ka, va, qb, kb, vb = _pre_call(x, p["gattn"], p["win"], p["gqa"], p["wq"], p["gkv"], p["wkv"],
                                       p["gqn"], p["gkn"], *_rope_tables(s), tm)
    tk = 2 * tm if s % (4 * tm) == 0 else s // 2
    tq_a, tq_b = _pick(s, MLA_TQ), _pick(s, GQA_TQ)
    oa = _flash_call(qa, ka, va, tq_a, MLA_Q_TILES if s % (MLA_Q_TILES * tq_a) == 0 else 1, tk, "mla_flash")
    ob = _flash_call(qb, kb, vb, tq_b, GQA_Q_TILES if s % (GQA_Q_TILES * tq_b) == 0 else 1, tk, "gqa_flash")
    return oa.reshape(t, -1), ob.reshape(t, -1), x.reshape(t, d)


def _channel_mix(sets, p, tm):
    x1, hp, route = _post_call(sets, p["wout"], p["gffn"], p["wr"], p["br"], 2 * POST_SUB)
    t = x1.shape[0]
    pad_tok, dest, blk_e, n_used = _route_plan(route, EXPERT_ROWS)
    rows = _moe_call(blk_e, n_used, pad_tok, hp, p["wg"], p["wu"], p["wd"], EXPERT_ROWS)
    nt = t // tm
    dest_blk = dest.reshape(TOP_K, nt, tm).transpose(1, 0, 2).reshape(nt, 1, TOP_K * tm)
    return _final_call(dest_blk, x1, route, rows, p["gfin"], tm, [x.shape[0] // tm for _, _, x in sets])


def kernel(x_prompt, x_sample, attn_norm, w_in, q_a_norm, w_q_up, kv_a_norm, w_kv_up, q_norm, k_norm, w_out,
           ffn_norm, w_group, b_group, w_expert, b_expert, w_gate, w_up, w_down, final_norm):
    assert attn_norm.shape[0] == 1, "single-layer trunk"
    win, wq, gqn, gkn = _prep_weights(w_in[0], w_q_up[0], q_norm[0], k_norm[0])
    d = w_in.shape[1]
    assert d == 2 * WORD_TILES * LANES
    n_pad = LANES - N_GROUPS - N_EXPERTS
    wr = jnp.concatenate([w_group[0], w_expert[0], jnp.zeros((d, n_pad), F32)], axis=1)
    br = jnp.concatenate([b_group[0], b_expert[0], jnp.zeros((n_pad,), F32)])[None, :]
    wr_hi = wr.astype(BF16)
    wr = jnp.concatenate([wr_hi, (wr - wr_hi.astype(F32)).astype(BF16)], axis=1)
    p = dict(gattn=attn_norm, win=win, gqa=q_a_norm, wq=wq, gkv=kv_a_norm, wkv=w_kv_up[0].astype(BF16),
             gqn=gqn, gkn=gkn, wout=w_out[0].astype(BF16), gffn=ffn_norm, wr=wr, br=br,
             wg=w_gate[0], wu=w_up[0], wd=w_down[0],
             gfin=final_norm[None, :])
    xs = (x_prompt, x_sample)
    tm = TOKEN_TILE
    assert all(x.shape[1] % (2 * POST_SUB) == 0 for x in xs)
    ys = _channel_mix([_attention(x, p, tm) for x in xs], p, tm)
    return tuple(y.reshape(x.shape) for y, x in zip(ys, xs))
```

```python
import functools
import math

import numpy as np
import jax
import jax.numpy as jnp
from jax import lax
from jax.experimental import pallas as pl
from jax.experimental.pallas import tpu as pltpu

F32 = jnp.float32
BF16 = jnp.bfloat16

EPS = 1e-6
ROPE_THETA = 10000.0
GRID_W = 64
LANES = 128
BF16_ROWS = 16
GATHER_UNROLL = 8
TOKEN_TILE = 256
POST_SUB = 256
EXPERT_ROWS = 256
MLA_TQ, MLA_Q_TILES = 1024, 8
GQA_TQ, GQA_Q_TILES = 512, 4
DMA_QUEUES = 2
WORD_TILES = 8

MLA_HEADS = 8
Q_LORA = 512
KV_LORA = 256
QK_NOPE = 128
QK_ROPE = 64
V_HEAD = 128
MLA_DK = 256
GQA_HEADS = 8
GQA_KV_HEADS = 2
GQA_REP = GQA_HEADS // GQA_KV_HEADS
HEAD_DIM = 128
N_GROUPS = 4
EXPERTS_PER_GROUP = 8
N_EXPERTS = N_GROUPS * EXPERTS_PER_GROUP
TOP_K = 2
LOG2E = math.log2(math.e)

VMEM_LIMIT = 56 * 1024 * 1024


def _cparams(sem):
    return pltpu.CompilerParams(dimension_semantics=sem, vmem_limit_bytes=VMEM_LIMIT)


def _rms(x, gain):
    return x * lax.rsqrt(jnp.mean(x * x, axis=-1, keepdims=True) + EPS) * gain


def _rope(x, cos_t, sin_t):
    return x * cos_t + pltpu.roll(x, LANES // 2, axis=1) * sin_t


def _pack_bf16_pairs(v):
    n = v.shape[-1] // 2
    w = pltpu.bitcast(v.astype(BF16).astype(F32), jnp.uint32)
    return w[:, n:2 * n] | (w[:, 0:n] >> 16)


def _store_token_tiles(ref, w):
    rows, n = w.shape
    k = n // LANES
    for c in range(k):
        ref[pl.ds(c, rows, stride=k), :] = w[:, c * LANES:(c + 1) * LANES]


def _load_token_tiles(ref, rows, k, r0=0):
    return jnp.concatenate([ref[pl.ds(r0 * k + c, rows, stride=k), :] for c in range(k)], axis=1)


def _unpack_bf16_pairs(w):
    return pltpu.bitcast(w << 16, F32), pltpu.bitcast(w & jnp.uint32(0xFFFF0000), F32)


def _pre_kernel(x_ref, gattn_ref, win_ref, gqa_ref, wq_ref, gkv_ref, wkv_ref, gqn_ref, gkn_ref,
                ca_ref, sa_ref, cb_ref, sb_ref,
                qa_ref, ka_ref, va_ref, qb_ref, kb_ref, vb_ref):
    x = x_ref[0]
    h = _rms(x, gattn_ref[...]).astype(BF16)
    proj = jnp.dot(h, win_ref[...], preferred_element_type=F32)
    o = 0
    q_lat = proj[:, o:o + Q_LORA]; o += Q_LORA
    kv_lat = proj[:, o:o + KV_LORA]; o += KV_LORA
    k_rope = proj[:, o:o + LANES]; o += LANES
    gq = proj[:, o:o + GQA_HEADS * HEAD_DIM]; o += GQA_HEADS * HEAD_DIM
    gk = proj[:, o:o + GQA_KV_HEADS * HEAD_DIM]; o += GQA_KV_HEADS * HEAD_DIM
    gv = proj[:, o:o + GQA_KV_HEADS * HEAD_DIM]

    ca, sa, cb, sb = ca_ref[...], sa_ref[...], cb_ref[...], sb_ref[...]

    q = jnp.dot(_rms(q_lat, gqa_ref[...]).astype(BF16), wq_ref[...], preferred_element_type=F32)
    kv = jnp.dot(_rms(kv_lat, gkv_ref[...]).astype(BF16), wkv_ref[...], preferred_element_type=F32)
    k_pe = _rope(k_rope, ca, sa).astype(BF16)
    sc_a = np.float32((QK_NOPE + QK_ROPE) ** -0.5 * LOG2E)
    for hd in range(MLA_HEADS):
        qh = q[:, hd * MLA_DK:(hd + 1) * MLA_DK]
        qa_ref[0, hd, 0, 0:LANES, :] = (qh[:, 0:LANES] * sc_a).T.astype(BF16)
        qa_ref[0, hd, 0, LANES:MLA_DK, :] = (_rope(qh[:, LANES:MLA_DK], ca, sa) * sc_a).T.astype(BF16)
        kvh = kv[:, hd * 2 * LANES:(hd + 1) * 2 * LANES]
        ka_ref[0, hd, :, 0:LANES] = kvh[:, 0:LANES].astype(BF16)
        ka_ref[0, hd, :, LANES:MLA_DK] = k_pe
        va_ref[0, hd, 0] = kvh[:, LANES:2 * LANES].T.astype(BF16)

    sc_b = np.float32(HEAD_DIM ** -0.5 * LOG2E)
    gqn, gkn = gqn_ref[...], gkn_ref[...]
    for hd in range(GQA_HEADS):
        qh = _rope(_rms(gq[:, hd * HEAD_DIM:(hd + 1) * HEAD_DIM], gqn), cb, sb)
        qb_ref[0, hd, 0] = (qh * sc_b).T.astype(BF16)
    for hd in range(GQA_KV_HEADS):
        kh = _rope(_rms(gk[:, hd * HEAD_DIM:(hd + 1) * HEAD_DIM], gkn), cb, sb)
        kb_ref[0, hd] = kh.astype(BF16)
        vb_ref[0, hd, 0] = gv[:, hd * HEAD_DIM:(hd + 1) * HEAD_DIM].T.astype(BF16)


def _pre_call(x, gattn, win, gqa, wq, gkv, wkv, gqn, gkn, ca, sa, cb, sb, tm):
    b, s, d = x.shape
    const = lambda shape: pl.BlockSpec(shape, lambda bi, si: (0,) * len(shape))
    tab = pl.BlockSpec((tm, LANES), lambda bi, si: (si, 0))
    head = lambda nh, w: pl.BlockSpec((1, nh, tm, w), lambda bi, si: (bi, 0, si, 0))
    head_t = lambda nh, w: pl.BlockSpec((1, nh, 1, w, tm), lambda bi, si: (bi, 0, si, 0, 0))
    return pl.pallas_call(
        _pre_kernel,
        grid=(b, s // tm),
        in_specs=[pl.BlockSpec((1, tm, d), lambda bi, si: (bi, si, 0)),
                  const(gattn.shape), const(win.shape), const(gqa.shape), const(wq.shape),
                  const(gkv.shape), const(wkv.shape), const(gqn.shape), const(gkn.shape),
                  tab, tab, tab, tab],
        out_specs=[head_t(MLA_HEADS, MLA_DK), head(MLA_HEADS, MLA_DK), head_t(MLA_HEADS, V_HEAD),
                   head_t(GQA_HEADS, HEAD_DIM), head(GQA_KV_HEADS, HEAD_DIM), head_t(GQA_KV_HEADS, HEAD_DIM)],
        out_shape=[jax.ShapeDtypeStruct((b, MLA_HEADS, s // tm, MLA_DK, tm), BF16),
                   jax.ShapeDtypeStruct((b, MLA_HEADS, s, MLA_DK), BF16),
                   jax.ShapeDtypeStruct((b, MLA_HEADS, s // tm, V_HEAD, tm), BF16),
                   jax.ShapeDtypeStruct((b, GQA_HEADS, s // tm, HEAD_DIM, tm), BF16),
                   jax.ShapeDtypeStruct((b, GQA_KV_HEADS, s, HEAD_DIM), BF16),
                   jax.ShapeDtypeStruct((b, GQA_KV_HEADS, s // tm, HEAD_DIM, tm), BF16)],
        compiler_params=_cparams(("parallel", "parallel")),
        name="pre_proj",
    )(x, gattn, win, gqa, wq, gkv, wkv, gqn, gkn, ca, sa, cb, sb)


def _flash_kernel(q_ref, k_ref, vt_ref, o_ref, acc_sc, s_sc, *, streams, tq, tk):
    s_len = k_ref.shape[2]
    dv, cw = vt_ref.shape[3], vt_ref.shape[4]
    qw = q_ref.shape[4]
    n_qc = tq // qw
    n_sub = tk // cw
    n_steps = s_len // tk
    unroll = 8 if n_steps % 16 == 0 else 4 if n_steps % 8 == 0 else 2
    n_trips = n_steps // unroll

    def scores(r, c0, j, slot):
        start = j * tk if isinstance(j, int) else pl.multiple_of(j * tk, tk)
        k = k_ref[0, 0, pl.ds(start, tk), :]
        st = jnp.concatenate([jnp.dot(k, q_ref[0, r, c0 + c], preferred_element_type=F32)
                              for c in range(n_qc)], axis=1)
        s_sc[slot] = st
        return jnp.max(st, axis=0, keepdims=True)

    ones = jnp.ones((BF16_ROWS, cw), BF16)

    def key_step(j, slot, m_prev, mx, next_scores):
        mx_next = next_scores()
        m_new = jnp.maximum(m_prev, mx)
        alpha = jnp.exp2(m_prev - m_new)
        pb = jnp.exp2(s_sc[slot] - m_new).astype(BF16)
        pv = None
        for c in range(n_sub):
            vt1 = jnp.concatenate([vt_ref[0, 0, j * n_sub + c], ones], axis=0)
            d = jnp.dot(vt1, pb[c * cw:(c + 1) * cw], preferred_element_type=F32)
            pv = d if pv is None else pv + d
        acc_sc[...] = alpha * acc_sc[...] + pv
        return m_new, mx_next

    mx = scores(streams[0][0], streams[0][1], 0, 0)
    for si, stream in enumerate(streams):
        r, c0, row, col = stream
        r_nx, c0_nx, j_nx = (streams[si + 1][0], streams[si + 1][1], 0) if si + 1 < len(streams) \
            else (r, c0, n_steps - 1)
        acc_sc[...] = jnp.zeros(acc_sc.shape, F32)

        def body(jj, carry, r=r, c0=c0, r_nx=r_nx, c0_nx=c0_nx, j_nx=j_nx):
            m_prev, mx = carry
            for u in range(unroll):
                j = jj * unroll + u
                if u + 1 < unroll:
                    nxt = functools.partial(scores, r, c0, j + 1, 1 - u % 2)
                else:
                    last = jj == n_trips - 1
                    nxt = functools.partial(scores, jnp.where(last, r_nx, r), jnp.where(last, c0_nx, c0),
                                            jnp.where(last, j_nx, j + 1), 1 - u % 2)
                m_prev, mx = key_step(j, u % 2, m_prev, mx, nxt)
            return m_prev, mx

        _, mx = lax.fori_loop(0, n_trips, body, (jnp.full((1, tq), -jnp.inf, F32), mx))
        out_t = acc_sc[0:dv, :] / acc_sc[dv:dv + 1, :]
        o_ref[0, row:row + tq, col:col + dv] = out_t.T.astype(o_ref.dtype)


def _flash_call(qt, k, vt, tq, n_tiles, tk, name):
    b, hq, _, dk, qw = qt.shape
    s = k.shape[2]
    hkv = k.shape[1]
    _, _, nchunk, dv, cw = vt.shape
    rep = hq // hkv
    tq_blk = tq * n_tiles
    assert tk % cw == 0 and (s // tk) % 2 == 0 and s % tq_blk == 0 and tq % qw == 0
    streams = tuple((r, ti * (tq // qw), ti * tq, r * dv) for r in range(rep) for ti in range(n_tiles))
    return pl.pallas_call(
        functools.partial(_flash_kernel, streams=streams, tq=tq, tk=tk),
        grid=(b, hkv, s // tq_blk),
        in_specs=[pl.BlockSpec((1, rep, tq_blk // qw, dk, qw), lambda bi, g, qi: (bi, g, qi, 0, 0)),
                  pl.BlockSpec((1, 1, s, dk), lambda bi, g, qi: (bi, g, 0, 0)),
                  pl.BlockSpec((1, 1, nchunk, dv, cw), lambda bi, g, qi: (bi, g, 0, 0, 0))],
        out_specs=pl.BlockSpec((1, tq_blk, rep * dv), lambda bi, g, qi: (bi, qi, g)),
        out_shape=jax.ShapeDtypeStruct((b, s, hq * dv), BF16),
        scratch_shapes=[pltpu.VMEM((dv + BF16_ROWS, tq), F32), pltpu.VMEM((2, tk, tq), F32)],
        compiler_params=_cparams(("parallel", "parallel", "arbitrary")),
        name=name,
    )(qt, k, vt)


def _post_kernel(*refs, bounds):
    n_set = len(bounds) - 1
    ins, (wout_ref, gffn_ref, wr_ref, br_ref, x1_ref, hp_ref, route_ref) = refs[:3 * n_set], refs[3 * n_set:]
    i = pl.program_id(0)
    for si in range(n_set):
        @pl.when((i >= bounds[si]) & (i < bounds[si + 1]))
        def _(si=si):
            _post_tile(*ins[3 * si:3 * si + 3], wout_ref, gffn_ref, wr_ref, br_ref, x1_ref, hp_ref, route_ref)


def _post_tile(oa_ref, ob_ref, x_ref, wout_ref, gffn_ref, wr_ref, br_ref, x1_ref, hp_ref, route_ref):
    for r0 in range(0, x_ref.shape[0], POST_SUB):
        rows = pl.ds(r0, POST_SUB)
        k = hp_ref.shape[0] // x_ref.shape[0]
        _post_rows(oa_ref.at[rows], ob_ref.at[rows], x_ref.at[rows], wout_ref, gffn_ref, wr_ref, br_ref,
                   x1_ref.at[rows], hp_ref.at[pl.ds(r0 * k, POST_SUB * k)], route_ref.at[rows])


def _post_rows(oa_ref, ob_ref, x_ref, wout_ref, gffn_ref, wr_ref, br_ref, x1_ref, hp_ref, route_ref):
    half = oa_ref.shape[-1]
    mix = jnp.dot(oa_ref[...], wout_ref[0:half, :], preferred_element_type=F32)
    mix = mix + jnp.dot(ob_ref[...], wout_ref[half:2 * half, :], preferred_element_type=F32)
    x1 = x_ref[...] + mix
    x1_ref[...] = x1
    h = _rms(x1, gffn_ref[...])

    h_hi = h.astype(BF16)
    h_hi32 = h_hi.astype(F32)
    _store_token_tiles(hp_ref, _pack_bf16_pairs(h))

    h_lo = (h - h_hi32).astype(BF16)
    lg2 = (jnp.dot(h_hi, wr_ref[...], preferred_element_type=F32)
           + jnp.dot(h_lo, wr_ref[...], preferred_element_type=F32))
    lg = lg2[:, 0:LANES] + lg2[:, LANES:2 * LANES] + br_ref[...]
    lane = lax.broadcasted_iota(jnp.int32, lg.shape, 1)
    neg = jnp.float32(-jnp.inf)
    big = jnp.int32(4 * LANES)
    gl = jnp.where(lane < N_GROUPS, lg, neg)
    gmax = jnp.max(gl, axis=-1, keepdims=True)
    gsum = jnp.sum(jnp.exp(gl - gmax), axis=-1, keepdims=True)
    g_w = 1.0 / gsum
    g_idx = jnp.min(jnp.where(gl == gmax, lane, big), axis=-1, keepdims=True)
    lo_lane = N_GROUPS + EXPERTS_PER_GROUP * g_idx
    el = jnp.where((lane >= lo_lane) & (lane < lo_lane + EXPERTS_PER_GROUP), lg, neg)
    m1 = jnp.max(el, axis=-1, keepdims=True)
    i1 = jnp.min(jnp.where(el == m1, lane, big), axis=-1, keepdims=True)
    el2 = jnp.where(lane == i1, neg, el)
    m2 = jnp.max(el2, axis=-1, keepdims=True)
    i2 = jnp.min(jnp.where(el2 == m2, lane, big), axis=-1, keepdims=True)
    e2 = jnp.exp(m2 - m1)
    w1 = 1.0 / (1.0 + e2)
    w2 = e2 / (1.0 + e2)
    out = jnp.where(lane == 0, (i1 - N_GROUPS).astype(F32),
          jnp.where(lane == 1, (i2 - N_GROUPS).astype(F32),
          jnp.where(lane == 2, g_w * w1,
          jnp.where(lane == 3, g_w * w2, 0.0))))
    route_ref[...] = out


def _post_call(sets, wout, gffn, wr, br, tm):
    d = sets[0][2].shape[1]
    half = sets[0][0].shape[1]
    n_tiles = [x.shape[0] // tm for _, _, x in sets]
    bounds = tuple(int(v) for v in np.concatenate([[0], np.cumsum(n_tiles)]))
    t = bounds[-1] * tm
    const = lambda shape: pl.BlockSpec(shape, lambda i: (0,) * len(shape))
    row = lambda w: pl.BlockSpec((tm, w), lambda i: (i, 0))
    in_specs, operands = [], []
    for si, (oa, ob, x) in enumerate(sets):
        pin = lambda w, lo=bounds[si], n=n_tiles[si]: pl.BlockSpec((tm, w), lambda i: (jnp.clip(i - lo, 0, n - 1), 0))
        in_specs += [pin(half), pin(half), pin(d)]
        operands += [oa, ob, x]
    return pl.pallas_call(
        functools.partial(_post_kernel, bounds=bounds),
        grid=(bounds[-1],),
        in_specs=in_specs + [const(wout.shape), const(gffn.shape), const(wr.shape), const(br.shape)],
        out_specs=[row(d), pl.BlockSpec((tm * WORD_TILES, LANES), lambda i: (i, 0)), row(LANES)],
        out_shape=[jax.ShapeDtypeStruct((t, d), F32),
                   jax.ShapeDtypeStruct((t * WORD_TILES, LANES), jnp.uint32),
                   jax.ShapeDtypeStruct((t, LANES), F32)],
        compiler_params=_cparams(("parallel",)),
        name="post_proj_router",
    )(*operands, wout, gffn, wr, br)


def _tile_copy(src_hbm, idx_ref, r, buf, sem):
    k = WORD_TILES
    return pltpu.make_async_copy(src_hbm.at[pl.ds(pl.multiple_of(idx_ref[0, 0, r], k), k)],
                                 buf.at[pl.ds(pl.multiple_of(r * k, k), k)], sem)


def _row_gather(src_hbm, idx_ref, buf, sem, n_rows):
    def body(r, carry):
        _tile_copy(src_hbm, idx_ref, r, buf, sem).start()
        return carry
    lax.fori_loop(0, n_rows, body, 0, unroll=GATHER_UNROLL)


def _moe_kernel(blk_e_ref, n_used_ref, tok_cur_ref, tok_next_ref, hp_hbm, wg_ref, wu_ref, wd_ref,
                out_ref, buf0, buf1, sem, wgb, wub, wdb):
    i = pl.program_id(0)
    tb = buf0.shape[0] // WORD_TILES
    dh = WORD_TILES * LANES
    n_used = n_used_ref[0]

    @pl.when((i == 0) | (blk_e_ref[i] != blk_e_ref[jnp.maximum(i - 1, 0)]))
    def _():
        wgb[...] = wg_ref[0].astype(BF16)
        wub[...] = wu_ref[0].astype(BF16)
        wdb[...] = wd_ref[0].astype(BF16)

    @pl.when(i == 0)
    def _():
        _row_gather(hp_hbm, tok_cur_ref, buf0, sem.at[0], tb)

    def wait_rows(buf, s_):
        pltpu.make_async_copy(hp_hbm.at[pl.ds(0, tb * WORD_TILES)], buf, sem.at[s_]).wait()

    def block(cur, cur_s, nxt, nxt_s):
        wait_rows(cur, cur_s)
        for r in range(tb):
            _tile_copy(hp_hbm, tok_next_ref, r, nxt, sem.at[nxt_s]).start(priority=r % DMA_QUEUES)
        x_lo, x_hi = (v.astype(BF16) for v in _unpack_bf16_pairs(_load_token_tiles(cur, tb, WORD_TILES)))
        g = jnp.dot(x_lo, wgb[0:dh, :], preferred_element_type=F32)
        g = g + jnp.dot(x_hi, wgb[dh:2 * dh, :], preferred_element_type=F32)
        u = jnp.dot(x_lo, wub[0:dh, :], preferred_element_type=F32)
        u = u + jnp.dot(x_hi, wub[dh:2 * dh, :], preferred_element_type=F32)
        hid = (g * jax.nn.sigmoid(g)) * u
        out = jnp.dot(hid.astype(BF16), wdb[...], preferred_element_type=F32)
        _store_token_tiles(out_ref, _pack_bf16_pairs(out))

    used = i < n_used
    even = i % 2 == 0

    @pl.when(used & even)
    def _():
        block(buf0, 0, buf1, 1)

    @pl.when(used & jnp.logical_not(even))
    def _():
        block(buf1, 1, buf0, 0)

    @pl.when((i == n_used) & even)
    def _():
        wait_rows(buf0, 0)

    @pl.when((i == n_used) & jnp.logical_not(even))
    def _():
        wait_rows(buf1, 1)

    @pl.when(i >= n_used)
    def _():
        out_ref[...] = jnp.zeros(out_ref.shape, out_ref.dtype)


def _moe_call(blk_e, n_used, pad_tok, hp, wg, wu, wd, tb):
    nb = pad_tok.shape[0]
    dh = WORD_TILES * LANES
    d = 2 * dh
    de = wg.shape[-1]
    smem_blk = lambda f: pl.BlockSpec((1, 1, tb), f, memory_space=pltpu.SMEM)
    wspec = lambda shape: pl.BlockSpec(shape, lambda i, be, nu: (be[i], 0, 0))
    grid_spec = pltpu.PrefetchScalarGridSpec(
        num_scalar_prefetch=2,
        grid=(nb,),
        in_specs=[smem_blk(lambda i, be, nu: (i, 0, 0)),
                  smem_blk(lambda i, be, nu: (jnp.minimum(i + 1, nb - 1), 0, 0)),
                  pl.BlockSpec(memory_space=pl.ANY),
                  wspec((1, d, de)), wspec((1, d, de)), wspec((1, de, d))],
        out_specs=pl.BlockSpec((tb * WORD_TILES, LANES), lambda i, be, nu: (i, 0)),
        scratch_shapes=[pltpu.VMEM((tb * WORD_TILES, LANES), jnp.uint32),
                        pltpu.VMEM((tb * WORD_TILES, LANES), jnp.uint32),
                        pltpu.SemaphoreType.DMA((2,)),
                        pltpu.VMEM((d, de), BF16), pltpu.VMEM((d, de), BF16), pltpu.VMEM((de, d), BF16)],
    )
    return pl.pallas_call(
        _moe_kernel,
        grid_spec=grid_spec,
        out_shape=jax.ShapeDtypeStruct((nb * tb * WORD_TILES, LANES), jnp.uint32),
        compiler_params=_cparams(("arbitrary",)),
        name="moe_experts",
    )(blk_e, n_used, pad_tok, pad_tok, hp, wg, wu, wd)


def _final_kernel(*refs, bounds):
    n_set = len(bounds) - 1
    dest_cur_ref, dest_next_ref, x1_ref, route_ref, rows_hbm, gfin_ref = refs[:6]
    y_refs = refs[6:6 + n_set]
    buf0, buf1, sem = refs[6 + n_set:]
    i = pl.program_id(0)
    last = i == pl.num_programs(0) - 1
    n_rows = buf0.shape[0] // WORD_TILES
    tm = n_rows // TOP_K

    @pl.when(i == 0)
    def _():
        _row_gather(rows_hbm, dest_cur_ref, buf0, sem.at[0], n_rows)

    def wait_rows(buf, s_):
        pltpu.make_async_copy(rows_hbm.at[pl.ds(0, n_rows * WORD_TILES)], buf, sem.at[s_]).wait()

    def block(cur, cur_s, nxt, nxt_s):
        wait_rows(cur, cur_s)
        for r in range(n_rows):
            _tile_copy(rows_hbm, dest_next_ref, r, nxt, sem.at[nxt_s]).start(priority=r % DMA_QUEUES)
        route = route_ref[...]
        g0, g1 = route[:, 2:3], route[:, 3:4]
        a_lo, a_hi = _unpack_bf16_pairs(_load_token_tiles(cur, tm, WORD_TILES))
        b_lo, b_hi = _unpack_bf16_pairs(_load_token_tiles(cur, tm, WORD_TILES, r0=tm))
        moe = jnp.concatenate([a_lo * g0 + b_lo * g1, a_hi * g0 + b_hi * g1], axis=1)
        y = _rms(x1_ref[...] + moe, gfin_ref[...])
        for si in range(n_set):
            @pl.when((i >= bounds[si]) & (i < bounds[si + 1]))
            def _(si=si):
                y_refs[si][...] = y

    even = i % 2 == 0

    @pl.when(even)
    def _():
        block(buf0, 0, buf1, 1)

    @pl.when(jnp.logical_not(even))
    def _():
        block(buf1, 1, buf0, 0)

    @pl.when(last & even)
    def _():
        wait_rows(buf1, 1)

    @pl.when(last & jnp.logical_not(even))
    def _():
        wait_rows(buf0, 0)


def _final_call(dest, x1, route, rows, gfin, tm, n_tiles):
    t, d = x1.shape
    nt = t // tm
    bounds = tuple(int(v) for v in np.concatenate([[0], np.cumsum(n_tiles)]))
    assert bounds[-1] == nt
    smem_blk = lambda f: pl.BlockSpec((1, 1, TOP_K * tm), f, memory_space=pltpu.SMEM)
    out_specs = [pl.BlockSpec((tm, d), lambda i, lo=bounds[si], n=n_tiles[si]: (jnp.clip(i - lo, 0, n - 1), 0))
                 for si in range(len(n_tiles))]
    return pl.pallas_call(
        functools.partial(_final_kernel, bounds=bounds),
        grid=(nt,),
        in_specs=[smem_blk(lambda i: (i, 0, 0)),
                  smem_blk(lambda i: (jnp.minimum(i + 1, nt - 1), 0, 0)),
                  pl.BlockSpec((tm, d), lambda i: (i, 0)),
                  pl.BlockSpec((tm, LANES), lambda i: (i, 0)),
                  pl.BlockSpec(memory_space=pl.ANY),
                  pl.BlockSpec(gfin.shape, lambda i: (0, 0))],
        out_specs=out_specs,
        out_shape=[jax.ShapeDtypeStruct((n * tm, d), F32) for n in n_tiles],
        scratch_shapes=[pltpu.VMEM((TOP_K * tm * WORD_TILES, LANES), jnp.uint32),
                        pltpu.VMEM((TOP_K * tm * WORD_TILES, LANES), jnp.uint32),
                        pltpu.SemaphoreType.DMA((2,))],
        compiler_params=_cparams(("arbitrary",)),
        name="combine_final_norm",
    )(dest, dest, x1, route, rows, gfin)


def _deinterleave(n):
    return np.concatenate([np.arange(0, n, 2), np.arange(1, n, 2)])


def _rope_pad_cols(base):
    pad = -np.ones(LANES // 4, np.int64)
    return np.concatenate([base + np.arange(0, QK_ROPE, 2), pad, base + np.arange(1, QK_ROPE, 2), pad])


def _take_cols(w, cols):
    w_ext = jnp.concatenate([w, jnp.zeros((w.shape[0], 1), w.dtype)], axis=1)
    return w_ext[:, np.where(cols < 0, w.shape[1], cols)]


def _prep_weights(w_in, w_q_up, q_norm, k_norm):
    o_kr = Q_LORA + KV_LORA
    o_gq = o_kr + QK_ROPE
    o_gk = o_gq + GQA_HEADS * HEAD_DIM
    o_gv = o_gk + GQA_KV_HEADS * HEAD_DIM
    cols = [np.arange(0, o_kr), _rope_pad_cols(o_kr)]
    for hd in range(GQA_HEADS):
        cols.append(o_gq + hd * HEAD_DIM + _deinterleave(HEAD_DIM))
    for hd in range(GQA_KV_HEADS):
        cols.append(o_gk + hd * HEAD_DIM + _deinterleave(HEAD_DIM))
    cols.append(np.arange(o_gv, o_gv + GQA_KV_HEADS * HEAD_DIM))
    win = _take_cols(w_in, np.concatenate(cols)).astype(BF16)

    qcols = []
    for hd in range(MLA_HEADS):
        base = hd * (QK_NOPE + QK_ROPE)
        qcols += [base + np.arange(QK_NOPE), _rope_pad_cols(base + QK_NOPE)]
    wq = _take_cols(w_q_up, np.concatenate(qcols)).astype(BF16)
    perm = _deinterleave(HEAD_DIM)
    return win, wq, q_norm[perm][None, :], k_norm[perm][None, :]


def _rope_tables(s):
    rows = s // GRID_W
    row = np.repeat(np.arange(rows, dtype=np.float64), GRID_W)
    col = np.tile(np.arange(GRID_W, dtype=np.float64), rows)

    def cos_sin(rot_dim):
        n_pairs = rot_dim // 4
        freqs = ROPE_THETA ** (-np.arange(n_pairs, dtype=np.float64) * 2.0 / (rot_dim // 2))
        ang = np.concatenate([row[:, None] * freqs[None, :], col[:, None] * freqs[None, :]], axis=-1)
        return np.cos(ang), np.sin(ang)

    ca, sa = cos_sin(QK_ROPE)
    cb, sb = cos_sin(HEAD_DIM)
    tabs = (np.concatenate([ca, ca, ca, ca], -1), np.concatenate([-sa, -sa, sa, sa], -1),
            np.concatenate([cb, cb], -1), np.concatenate([-sb, sb], -1))
    return tuple(jnp.asarray(t.astype(np.float32)) for t in tabs)


def _pick(n, pref):
    return pref if n % pref == 0 else n


def _route_plan(route, tb):
    t = route.shape[0]
    a = t * TOP_K
    expert_id = jnp.concatenate([route[:, c] for c in range(TOP_K)]).astype(jnp.int32)
    onehot = (expert_id[:, None] == jnp.arange(N_EXPERTS, dtype=jnp.int32)[None, :]).astype(jnp.int32)
    csum = jnp.cumsum(onehot, axis=0)
    counts = csum[-1]
    rank = jnp.sum(onehot * csum, axis=1) - 1
    padded = ((counts + tb - 1) // tb) * tb
    pad_ends = jnp.cumsum(padded)
    pad_off = pad_ends - padded
    dest = pad_off[expert_id] + rank
    nb = (a + N_EXPERTS * tb) // tb
    blk_start = jnp.arange(nb, dtype=jnp.int32) * tb
    blk_e = jnp.minimum(jnp.sum(blk_start[:, None] >= pad_ends[None, :], axis=1), N_EXPERTS - 1)
    n_used = (pad_ends[-1] // tb).astype(jnp.int32).reshape(1)
    order = jnp.sort(expert_id * a + jnp.arange(a, dtype=jnp.int32)) % a
    row_e = jnp.repeat(blk_e, tb)
    row_rank = jnp.arange(nb * tb, dtype=jnp.int32) - pad_off[row_e]
    offsets = jnp.cumsum(counts) - counts
    src = jnp.clip(offsets[row_e] + row_rank, 0, a - 1)
    valid = (row_rank < counts[row_e]) & (jnp.arange(nb * tb, dtype=jnp.int32) < pad_ends[-1])
    pad_tok = jnp.where(valid, order[src] % t, 0)
    last_e = blk_e[jnp.maximum(n_used[0] - 1, 0)]
    blk_e = jnp.where(jnp.arange(nb) < n_used[0], blk_e, last_e).astype(jnp.int32)
    return (pad_tok * WORD_TILES).reshape(nb, 1, tb), (dest * WORD_TILES).reshape(TOP_K, t), blk_e, n_used


def _attention(x, p, tm):
    b, s, d = x.shape
    t = b * s
    qa, ka, va, qb, kb, vb = _pre_call(x, p["gattn"], p["win"], p["gqa"], p["wq"], p["gkv"], p["wkv"],
                                       p["gqn"], p["gkn"], *_rope_tables(s), tm)
    tk = 2 * tm if s % (4 * tm) == 0 else s // 2
    tq_a, tq_b = _pick(s, MLA_TQ), _pick(s, GQA_TQ)
    oa = _flash_call(qa, ka, va, tq_a, MLA_Q_TILES if s % (MLA_Q_TILES * tq_a) == 0 else 1, tk, "mla_flash")
    ob = _flash_call(qb, kb, vb, tq_b, GQA_Q_TILES if s % (GQA_Q_TILES * tq_b) == 0 else 1, tk, "gqa_flash")
    return oa.reshape(t, -1), ob.reshape(t, -1), x.reshape(t, d)


def _channel_mix(sets, p, tm):
    x1, hp, route = _post_call(sets, p["wout"], p["gffn"], p["wr"], p["br"], 2 * POST_SUB)
    t = x1.shape[0]
    pad_tok, dest, blk_e, n_used = _route_plan(route, EXPERT_ROWS)
    rows = _moe_call(blk_e, n_used, pad_tok, hp, p["wg"], p["wu"], p["wd"], EXPERT_ROWS)
    nt = t // tm
    dest_blk = dest.reshape(TOP_K, nt, tm).transpose(1, 0, 2).reshape(nt, 1, TOP_K * tm)
    return _final_call(dest_blk, x1, route, rows, p["gfin"], tm, [x.shape[0] // tm for _, _, x in sets])


def kernel(x_prompt, x_sample, attn_norm, w_in, q_a_norm, w_q_up, kv_a_norm, w_kv_up, q_norm, k_norm, w_out,
           ffn_norm, w_group, b_group, w_expert, b_expert, w_gate, w_up, w_down, final_norm):
    assert attn_norm.shape[0] == 1, "single-layer trunk"
    win, wq, gqn, gkn = _prep_weights(w_in[0], w_q_up[0], q_norm[0], k_norm[0])
    d = w_in.shape[1]
    assert d == 2 * WORD_TILES * LANES
    n_pad = LANES - N_GROUPS - N_EXPERTS
    wr = jnp.concatenate([w_group[0], w_expert[0], jnp.zeros((d, n_pad), F32)], axis=1)
    br = jnp.concatenate([b_group[0], b_expert[0], jnp.zeros((n_pad,), F32)])[None, :]
    wr_hi = wr.astype(BF16)
    wr = jnp.concatenate([wr_hi, (wr - wr_hi.astype(F32)).astype(BF16)], axis=1)
    p = dict(gattn=attn_norm, win=win, gqa=q_a_norm, wq=wq, gkv=kv_a_norm, wkv=w_kv_up[0].astype(BF16),
             gqn=gqn, gkn=gkn, wout=w_out[0].astype(BF16), gffn=ffn_norm, wr=wr, br=br,
             wg=w_gate[0], wu=w_up[0], wd=w_down[0],
             gfin=final_norm[None, :])
    xs = (x_prompt, x_sample)
    tm = TOKEN_TILE
    assert all(x.shape[1] % (2 * POST_SUB) == 0 for x in xs)
    ys = _channel_mix([_attention(x, p, tm) for x in xs], p, tm)
    return tuple(y.reshape(x.shape) for y, x in zip(ys, xs))
```
